```python
import math
import jax, jax.numpy as jnp
from jax import lax
import numpy as np

D_MODEL = 1024
BATCH = 8
SEQ = 4096
DEPTH = 4

GRID_W = 64
CTX_LEN = 256
N_MIXERS = 3
D_FF = 4 * D_MODEL
DEEPNORM_ALPHA = (2 * DEPTH) ** 0.25
DEEPNORM_BETA = (8 * DEPTH) ** -0.25
LN_EPS = 1e-5
RMS_EPS = 1e-6

GLA_HEADS = 4
GLA_DK = D_MODEL // 2
GLA_DV = D_MODEL
GLA_HK = GLA_DK // GLA_HEADS
GLA_HV = GLA_DV // GLA_HEADS
GLA_RANK = 16
GLA_GATE_NORM = 16.0
GLA_CHUNK = 64
GLA_IN = 2 * GLA_DK + 2 * GLA_DV + 2 * GLA_RANK

SSD_DI = 2 * D_MODEL
SSD_HEADDIM = 64
SSD_HEADS = SSD_DI // SSD_HEADDIM
SSD_GROUPS = 8
SSD_REP = SSD_HEADS // SSD_GROUPS
SSD_STATE = 128
SSD_CONV = 5
SSD_CHUNK = 64
SSD_GN = SSD_GROUPS * SSD_STATE
SSD_CONV_DIM = SSD_DI + 2 * SSD_GN
SSD_IN = SSD_DI + SSD_CONV_DIM + 2 * SSD_HEADS

HY_ORDER = 2
HY_SHORT = 3
HY_EMB = 33
HY_FW = 64
HY_DECAY_TARGET = 1e-2
HY_FAST_DECAY = 0.3
HY_SLOW_DECAY = 1.5
HY_FILTER_GAIN = 0.05

kernel_name = 'hybrid_gla_ssd_hyena_prefix_dit'


def layer_norm(h, g, b):
    hf = h.astype(jnp.float32)
    mu = jnp.mean(hf, -1, keepdims=True)
    var = jnp.mean(jnp.square(hf - mu), -1, keepdims=True)
    return ((hf - mu) * lax.rsqrt(var + LN_EPS) * g + b).astype(h.dtype)


def rms_norm(h, g):
    hf = h.astype(jnp.float32)
    return (hf * lax.rsqrt(jnp.mean(hf * hf, -1, keepdims=True) + RMS_EPS) * g).astype(h.dtype)


def modulate(h, shift, scale):
    return h * (1 + scale) + shift


def snake(h):
    b, l, ch = h.shape
    rows = l // GRID_W
    g = h.reshape(b, rows, GRID_W, ch)
    odd = (jnp.arange(rows) % 2 == 1)[None, :, None, None]
    return jnp.where(odd, g[:, :, ::-1], g).reshape(b, l, ch)


def dwconv(u, w, bias):
    k, ch = w.shape
    out = lax.conv_general_dilated(u, w.astype(u.dtype)[:, None, :], window_strides=(1,),
                                   padding=[(k // 2, k // 2)],
                                   dimension_numbers=('NWC', 'WIO', 'NWC'),
                                   feature_group_count=ch)
    return out + bias


def to_heads(t, dh):
    b, l, _ = t.shape
    return t.reshape(b, l, -1, dh).transpose(0, 2, 1, 3)


def sq_relu_mlp(h, w1, w2):
    return jnp.square(jax.nn.relu(h @ w1)) @ w2


def gla_chunk_scan(q, k, v, log_a, s0):
    b, h, l, _ = q.shape
    nc = l // GLA_CHUNK

    def chunks(t):
        return jnp.moveaxis(t.reshape(b, h, nc, GLA_CHUNK, t.shape[-1]), 2, 0)

    mask = jnp.tril(jnp.ones((GLA_CHUNK, GLA_CHUNK), dtype=bool))[:, :, None]

    def step(s, inp):
        qi, ki, vi, gi = inp
        cum = jnp.cumsum(gi, axis=2)
        diff = cum[:, :, :, None, :] - cum[:, :, None, :, :]
        decay = jnp.exp(jnp.where(mask, diff, -jnp.inf))
        attn = jnp.einsum('bhid,bhjd,bhijd->bhij', qi, ki, decay)
        o = jnp.einsum('bhij,bhjv->bhiv', attn, vi) + jnp.einsum('bhid,bhdv->bhiv', qi * jnp.exp(cum), s)
        last = cum[:, :, -1:, :]
        s_new = s * jnp.exp(last[:, :, 0, :, None]) + jnp.einsum('bhjd,bhjv->bhdv', ki * jnp.exp(last - cum), vi)
        return s_new, o

    s_fin, oc = lax.scan(step, s0, tuple(map(chunks, (q, k, v, log_a))))
    return jnp.moveaxis(oc, 0, 2).reshape(b, h, l, v.shape[-1]), s_fin


def gla_mixer(uc, ul, w_in, w_a2, b_a2, norm_g, w_out, need_ctx):
    split_at = [GLA_DK, 2 * GLA_DK, 2 * GLA_DK + GLA_DV, 2 * GLA_DK + 2 * GLA_DV]

    def run(h, s_init, need_y):
        b, l, _ = h.shape
        q, k, v, g, a = jnp.split(h @ w_in, split_at, axis=-1)
        q = to_heads(q, GLA_HK) * GLA_HK ** -0.5
        k = to_heads(k, GLA_HK)
        v = to_heads(v, GLA_HV)
        logit = jnp.einsum('blzr,zrd->zbld', a.reshape(b, l, 2, GLA_RANK), w_a2) + b_a2[:, None, None, :]
        log_a = jax.nn.log_sigmoid(logit.astype(jnp.float32)) / GLA_GATE_NORM
        o_f, s_f = gla_chunk_scan(q, k, v, to_heads(log_a[0], GLA_HK), s_init[0])
        o_b, s_b = gla_chunk_scan(q[:, :, ::-1], k[:, :, ::-1], v[:, :, ::-1],
                                  to_heads(log_a[1], GLA_HK)[:, :, ::-1], s_init[1])
        if not need_y:
            return None, (s_f, s_b)
        o = rms_norm(o_f + o_b[:, :, ::-1], norm_g)
        o = o.transpose(0, 2, 1, 3).reshape(b, l, GLA_DV)
        return (o * jax.nn.silu(g)) @ w_out, (s_f, s_b)

    zeros = jnp.zeros((uc.shape[0], GLA_HEADS, GLA_HK, GLA_HV), jnp.float32)
    yc, ctx_states = run(uc, (zeros, zeros), need_ctx)
    yl, _ = run(ul, ctx_states, True)
    return yc, yl


def ssd_chunk_scan(xs, dt, a, bm, cm, s0):
    b, l = xs.shape[:2]
    nc = l // SSD_CHUNK

    def chunks(t):
        return jnp.moveaxis(t.reshape(b, nc, SSD_CHUNK, *t.shape[2:]), 1, 0)

    mask = jnp.tril(jnp.ones((SSD_CHUNK, SSD_CHUNK), dtype=bool))[None, :, :, None, None]

    def step(s, inp):
        xi, dti, bi, ci = inp
        cum = jnp.cumsum(dti * a, axis=1)
        seg = cum[:, :, None] - cum[:, None, :]
        decay = jnp.exp(jnp.where(mask, seg, -jnp.inf))
        xdt = xi * dti[..., None]
        cb = jnp.einsum('bign,bjgn->bijg', ci, bi)
        y = jnp.einsum('bijg,bijgr,bjgrp->bigrp', cb, decay, xdt)
        y = y + jnp.einsum('bign,bigr,bgrpn->bigrp', ci, jnp.exp(cum), s)
        last = cum[:, -1]
        s_new = s * jnp.exp(last)[..., None, None] + jnp.einsum(
            'bjgn,bjgr,bjgrp->bgrpn', bi, jnp.exp(last[:, None] - cum), xdt)
        return s_new, y

    s_fin, ys = lax.scan(step, s0, tuple(map(chunks, (xs, dt, bm, cm))))
    return jnp.moveaxis(ys, 0, 1).reshape(xs.shape), s_fin


def ssd_mixer(uc, ul, w_in, conv_w, conv_b, dt_bias, a_log, d_skip, norm_g, w_out, need_ctx):
    a = (-jnp.exp(a_log.astype(jnp.float32))).reshape(2, SSD_GROUPS, SSD_REP)
    d_gr = d_skip.reshape(SSD_GROUPS, SSD_REP, 1)

    def run(h, s_init, need_y):
        b, l, _ = h.shape
        z, xbc, dt = jnp.split(h @ w_in, [SSD_DI, SSD_DI + SSD_CONV_DIM], axis=-1)
        xbc = jax.nn.silu(dwconv(xbc, conv_w, conv_b))
        xs, bm, cm = jnp.split(xbc, [SSD_DI, SSD_DI + SSD_GN], axis=-1)
        xs = xs.reshape(b, l, SSD_GROUPS, SSD_REP, SSD_HEADDIM)
        bm = bm.reshape(b, l, SSD_GROUPS, SSD_STATE)
        cm = cm.reshape(b, l, SSD_GROUPS, SSD_STATE)
        dt = jax.nn.softplus(dt.astype(jnp.float32).reshape(b, l, 2, SSD_GROUPS, SSD_REP)
                             + dt_bias.reshape(2, SSD_GROUPS, SSD_REP))
        y_f, s_f = ssd_chunk_scan(xs, dt[:, :, 0], a[0], bm, cm, s_init[0])
        y_b, s_b = ssd_chunk_scan(xs[:, ::-1], dt[:, ::-1, 1], a[1], bm[:, ::-1], cm[:, ::-1], s_init[1])
        if not need_y:
            return None, (s_f, s_b)
        y = (y_f + y_b[:, ::-1] + xs * d_gr).reshape(b, l, SSD_DI)
        return rms_norm(y * jax.nn.silu(z), norm_g) @ w_out, (s_f, s_b)

    zeros = jnp.zeros((uc.shape[0], SSD_GROUPS, SSD_REP, SSD_HEADDIM, SSD_STATE), jnp.float32)
    yc, ctx_states = run(uc, (zeros, zeros), need_ctx)
    yl, _ = run(ul, ctx_states, True)
    return yc, yl


def hyena_pos_emb(l):
    bands = (HY_EMB - 1) // 2
    t = jnp.linspace(0.0, 1.0, l)[:, None]
    w = 2 * math.pi * jnp.arange(l, dtype=jnp.float32)[:, None] / l
    ang = jnp.linspace(1e-4, bands - 1, bands)[None, :] * w
    return jnp.concatenate([t, jnp.cos(ang), -jnp.sin(ang)], axis=-1)


def hyena_filters(l, f_w1, f_b1, f_w2, f_b2, f_w3, f_b3, f_w4, f_freq):
    act = lambda t: jnp.sin(f_freq * t)
    hh = act(hyena_pos_emb(l) @ f_w1 + f_b1)
    hh = act(hh @ f_w2 + f_b2)
    hh = act(hh @ f_w3 + f_b3)
    hh = (hh @ f_w4).reshape(l, HY_ORDER, 2, D_MODEL)
    t = jnp.linspace(0.0, 1.0, l)[:, None]
    deltas = jnp.abs(jnp.linspace(math.log(HY_DECAY_TARGET) / HY_FAST_DECAY,
                                  math.log(HY_DECAY_TARGET) / HY_SLOW_DECAY, D_MODEL))
    hh = hh * jnp.exp(-t * deltas)[:, None, None, :]
    fwd, bwd = hh[:, :, 0], hh[:, :, 1]
    return jnp.concatenate([fwd, jnp.zeros_like(fwd[:1]), bwd[:0:-1]], axis=0)


def fft_long_conv(u, h2l, bias):
    l = u.shape[1]
    uf = jnp.fft.rfft(u.astype(jnp.float32), n=2 * l, axis=1)
    hf = jnp.fft.rfft(h2l.astype(jnp.float32), n=2 * l, axis=0)
    y = jnp.fft.irfft(uf * hf, n=2 * l, axis=1)[:, :l]
    return (y + u * bias).astype(u.dtype)


def hyena_mixer(uc, ul, w_in, conv_w, conv_b, f_w1, f_b1, f_w2, f_b2, f_w3, f_b3, f_w4, f_freq,
                h_bias, w_out, need_ctx):
    def run(h):
        l = h.shape[1]
        v, x1, x2 = jnp.split(dwconv(h @ w_in, conv_w, conv_b), 3, axis=-1)
        filt = hyena_filters(l, f_w1, f_b1, f_w2, f_b2, f_w3, f_b3, f_w4, f_freq)
        z = x1 * fft_long_conv(v, filt[:, 0], h_bias[0])
        return (x2 * fft_long_conv(z, filt[:, 1], h_bias[1])) @ w_out

    yc = run(uc) if need_ctx else None
    return yc, run(ul)


def setup_inputs(seed: int = 0) -> dict:
    key = jax.random.key(seed)
    ks = iter(jax.random.split(key, 40))
    nrm = lambda shape, s: jax.random.normal(next(ks), shape, jnp.float32) * s
    uni = lambda shape, lo, hi: jax.random.uniform(next(ks), shape, jnp.float32, minval=lo, maxval=hi)
    inv_softplus = lambda y: y + jnp.log(-jnp.expm1(-y))
    D = D_MODEL
    ng, ns, nh = (len(range(m, DEPTH, N_MIXERS)) for m in range(N_MIXERS))
    return {
        'x': nrm((BATCH, SEQ, D), 1.0),
        'c': nrm((BATCH, D), 1.0),
        'ctx': nrm((BATCH, CTX_LEN, D), 1.0),
        'c_ctx': nrm((D,), 1.0),
        'ada_w': nrm((DEPTH, D, 6 * D), D ** -0.5),
        'ada_b': nrm((DEPTH, 6 * D), 0.02),
        'ln_g': 1.0 + nrm((DEPTH, 2, D), 0.02),
        'ln_b': nrm((DEPTH, 2, D), 0.02),
        'ffn_w1': nrm((DEPTH, D, D_FF), D ** -0.5),
        'ffn_w2': nrm((DEPTH, D_FF, D), D_FF ** -0.5 * DEEPNORM_BETA),
        'gla_w_in': nrm((ng, D, GLA_IN), D ** -0.5),
        'gla_w_a2': nrm((ng, 2, GLA_RANK, GLA_DK), GLA_RANK ** -0.5),
        'gla_b_a2': nrm((ng, 2, GLA_DK), 0.1),
        'gla_norm': 1.0 + nrm((ng, GLA_HV), 0.02),
        'gla_w_out': nrm((ng, GLA_DV, D), GLA_DV ** -0.5 * DEEPNORM_BETA),
        'ssd_w_in': nrm((ns, D, SSD_IN), D ** -0.5),
        'ssd_conv_w': nrm((ns, SSD_CONV, SSD_CONV_DIM), SSD_CONV ** -0.5),
        'ssd_conv_b': nrm((ns, SSD_CONV_DIM), 0.02),
        'ssd_dt_bias': inv_softplus(jnp.exp(uni((ns, 2, SSD_HEADS), math.log(1e-3), math.log(1e-1)))),
        'ssd_a_log': jnp.log(uni((ns, 2, SSD_HEADS), 1.0, 16.0)),
        'ssd_d': 1.0 + nrm((ns, SSD_HEADS), 0.1),
        'ssd_norm': 1.0 + nrm((ns, SSD_DI), 0.02),
        'ssd_w_out': nrm((ns, SSD_DI, D), SSD_DI ** -0.5 * DEEPNORM_BETA),
        'hy_w_in': nrm((nh, D, 3 * D), D ** -0.5),
        'hy_conv_w': nrm((nh, HY_SHORT, 3 * D), HY_SHORT ** -0.5),
        'hy_conv_b': nrm((nh, 3 * D), 0.02),
        'hy_f_w1': nrm((nh, HY_EMB, HY_FW), HY_EMB ** -0.5),
        'hy_f_b1': nrm((nh, HY_FW), 0.1),
        'hy_f_w2': nrm((nh, HY_FW, HY_FW), HY_FW ** -0.5),
        'hy_f_b2': nrm((nh, HY_FW), 0.1),
        'hy_f_w3': nrm((nh, HY_FW, HY_FW), HY_FW ** -0.5),
        'hy_f_b3': nrm((nh, HY_FW), 0.1),
        'hy_f_w4': nrm((nh, HY_FW, HY_ORDER * 2 * D), HY_FW ** -0.5 * HY_FILTER_GAIN),
        'hy_f_freq': 1.0 + nrm((nh, HY_FW), 0.1),
        'hy_bias': nrm((nh, HY_ORDER, D), 1.0),
        'hy_w_out': nrm((nh, D, D), D ** -0.5 * DEEPNORM_BETA),
    }


def reference(x, c, ctx, c_ctx, ada_w, ada_b, ln_g, ln_b, ffn_w1, ffn_w2,
              gla_w_in, gla_w_a2, gla_b_a2, gla_norm, gla_w_out,
              ssd_w_in, ssd_conv_w, ssd_conv_b, ssd_dt_bias, ssd_a_log, ssd_d, ssd_norm, ssd_w_out,
              hy_w_in, hy_conv_w, hy_conv_b, hy_f_w1, hy_f_b1, hy_f_w2, hy_f_b2, hy_f_w3, hy_f_b3,
              hy_f_w4, hy_f_freq, hy_bias, hy_w_out):
    hl, hc = x, ctx
    s_lat = jax.nn.silu(c)
    s_ctx = jax.nn.silu(c_ctx)
    for i in range(DEPTH):
        kind, j = i % N_MIXERS, i // N_MIXERS
        need_ctx = i < DEPTH - 1
        m_l = jnp.split((s_lat @ ada_w[i] + ada_b[i])[:, None, :], 6, axis=-1)
        m_c = jnp.split(s_ctx @ ada_w[i] + ada_b[i], 6, axis=-1)
        ul = snake(modulate(hl, m_l[0], m_l[1]))
        uc = modulate(hc, m_c[0], m_c[1])
        if kind == 0:
            yc, yl = gla_mixer(uc, ul, gla_w_in[j], gla_w_a2[j], gla_b_a2[j], gla_norm[j], gla_w_out[j], need_ctx)
        elif kind == 1:
            yc, yl = ssd_mixer(uc, ul, ssd_w_in[j], ssd_conv_w[j], ssd_conv_b[j], ssd_dt_bias[j],
                               ssd_a_log[j], ssd_d[j], ssd_norm[j], ssd_w_out[j], need_ctx)
        else:
            yc, yl = hyena_mixer(uc, ul, hy_w_in[j], hy_conv_w[j], hy_conv_b[j], hy_f_w1[j], hy_f_b1[j],
                                 hy_f_w2[j], hy_f_b2[j], hy_f_w3[j], hy_f_b3[j], hy_f_w4[j], hy_f_freq[j],
                                 hy_bias[j], hy_w_out[j], need_ctx)
        hl = layer_norm(DEEPNORM_ALPHA * hl + m_l[2] * snake(yl), ln_g[i, 0], ln_b[i, 0])
        hl = layer_norm(DEEPNORM_ALPHA * hl + m_l[5] * sq_relu_mlp(modulate(hl, m_l[3], m_l[4]), ffn_w1[i], ffn_w2[i]),
                        ln_g[i, 1], ln_b[i, 1])
        if need_ctx:
            hc = layer_norm(DEEPNORM_ALPHA * hc + m_c[2] * yc, ln_g[i, 0], ln_b[i, 0])
            hc = layer_norm(DEEPNORM_ALPHA * hc + m_c[5] * sq_relu_mlp(modulate(hc, m_c[3], m_c[4]), ffn_w1[i], ffn_w2[i]),
                            ln_g[i, 1], ln_b[i, 1])
    return hl
```

```python
import functools
import math

import numpy as np
import jax
import jax.numpy as jnp
from jax import lax
from jax.experimental import pallas as pl
from jax.experimental.pallas import tpu as pltpu

F32 = jnp.float32
BF16 = jnp.bfloat16

D_MODEL = 1024
DEPTH = 4
GRID_W = 64
N_MIXERS = 3
D_FF = 4 * D_MODEL
DEEPNORM_ALPHA = (2 * DEPTH) ** 0.25
LN_EPS = 1e-5
RMS_EPS = 1e-6

GLA_HEADS = 4
GLA_DK = D_MODEL // 2
GLA_DV = D_MODEL
GLA_HK = GLA_DK // GLA_HEADS
GLA_HV = GLA_DV // GLA_HEADS
GLA_RANK = 16
GLA_GATE_NORM = 16.0
GLA_CHUNK = 64
GLA_SUB = 16
GLA_MAIN = 2 * GLA_DK + 2 * GLA_DV

SSD_DI = 2 * D_MODEL
SSD_HEADDIM = 64
SSD_HEADS = SSD_DI // SSD_HEADDIM
SSD_GROUPS = 8
SSD_REP = SSD_HEADS // SSD_GROUPS
SSD_STATE = 128
SSD_CONV = 5
SSD_CHUNK = 128
SSD_GN = SSD_GROUPS * SSD_STATE
SSD_CONV_DIM = SSD_DI + 2 * SSD_GN
SSD_MAIN = SSD_DI + SSD_CONV_DIM
SSD_GW = SSD_REP * SSD_HEADDIM

HY_ORDER = 2
HY_SHORT = 3
HY_EMB = 33
HY_FW = 64
HY_DECAY_TARGET = 1e-2
HY_FAST_DECAY = 0.3
HY_SLOW_DECAY = 1.5

LANE = 128
SUBLANE = 8
VMEM_LIMIT = 56 * 1024 * 1024


def _cparams(sem):
    return pltpu.CompilerParams(dimension_semantics=sem, vmem_limit_bytes=VMEM_LIMIT)


def _dot(a, b):
    return jnp.dot(a.astype(BF16), b.astype(BF16), preferred_element_type=F32)


def _dot_nt(a, b):
    return lax.dot_general(a.astype(BF16), b.astype(BF16), (((1,), (1,)), ((), ())),
                           preferred_element_type=F32)


def _split3(x):
    hi = x.astype(BF16)
    r1 = x - hi.astype(F32)
    mid = r1.astype(BF16)
    lo = (r1 - mid.astype(F32)).astype(BF16)
    return hi, mid, lo


def _dot_exact_l(m01, x):
    hi, mid, lo = _split3(x)
    d = lambda p: jnp.dot(m01, p, preferred_element_type=F32)
    return d(hi) + d(mid) + d(lo)


def _dot_exact_r(x, m01):
    hi, mid, lo = _split3(x)
    d = lambda p: jnp.dot(p, m01, preferred_element_type=F32)
    return d(hi) + d(mid) + d(lo)


def _dot_exact_nt(m01, x):
    hi, mid, lo = _split3(x)
    d = lambda p: lax.dot_general(m01, p, (((1,), (1,)), ((), ())), preferred_element_type=F32)
    return d(hi) + d(mid) + d(lo)


def _dot_f32(a, b):
    ah, am, al = _split3(a)
    bh, bm, bl = _split3(b)
    d = lambda p, q: jnp.dot(p, q, preferred_element_type=F32)
    return (d(ah, bh) + (d(ah, bm) + d(am, bh)) + (d(ah, bl) + d(al, bh) + d(am, bm)))


def _silu(x):
    return x * jax.nn.sigmoid(x)


def _softplus(x):
    return jnp.maximum(x, 0.0) + jnp.log1p(jnp.exp(-jnp.abs(x)))


def _log_sigmoid(x):
    return -_softplus(-x)


def _layer_norm(h, g, b):
    mu = jnp.mean(h, -1, keepdims=True)
    d = h - mu
    var = jnp.mean(d * d, -1, keepdims=True)
    return d * lax.rsqrt(var + LN_EPS) * g + b


def _res_ln(x, gate, y, g, b):
    return _layer_norm(DEEPNORM_ALPHA * x + gate * y, g, b)


def _ada_kernel(c_ref, w_ref, b_ref, o_ref):
    s = _silu(c_ref[...])
    o_ref[0] = _dot_f32(s, w_ref[0]) + b_ref[0]


def _ada(cvec, ada_w, ada_b):
    tn = 1536
    n = 6 * D_MODEL
    return pl.pallas_call(
        _ada_kernel,
        grid=(DEPTH, n // tn),
        in_specs=[pl.BlockSpec((16, D_MODEL), lambda i, j: (0, 0)),
                  pl.BlockSpec((1, D_MODEL, tn), lambda i, j: (i, 0, j)),
                  pl.BlockSpec((1, 1, tn), lambda i, j: (i, 0, j))],
        out_specs=pl.BlockSpec((1, 16, tn), lambda i, j: (i, 0, j)),
        out_shape=jax.ShapeDtypeStruct((DEPTH, 16, n), F32),
        compiler_params=_cparams(("arbitrary", "arbitrary")),
        name="ada",
    )(cvec, ada_w, ada_b.reshape(DEPTH, 1, n))


def _proj_kernel(x_ref, m_ref, w_ref, o_ref):
    m = m_ref[0]
    u = x_ref[0] * (1.0 + m[1:2]) + m[0:1]
    o_ref[0] = _dot(u, w_ref[...])


def _proj(x, mods, w, tn):
    b, t, d = x.shape
    n = w.shape[1]
    tm = min(t, 512)
    return pl.pallas_call(
        _proj_kernel,
        grid=(b, t // tm, n // tn),
        in_specs=[pl.BlockSpec((1, tm, d), lambda i, j, k: (i, j, 0)),
                  pl.BlockSpec((1, 8, d), lambda i, j, k: (i, 0, 0)),
                  pl.BlockSpec((d, tn), lambda i, j, k: (0, k))],
        out_specs=pl.BlockSpec((1, tm, tn), lambda i, j, k: (i, j, k)),
        out_shape=jax.ShapeDtypeStruct((b, t, n), F32),
        compiler_params=_cparams(("arbitrary", "arbitrary", "arbitrary")),
        name="proj",
    )(x, mods, w)


def _ffn_kernel(x_ref, m_ref, w1_ref, w2_ref, g_ref, b_ref, o_ref, acc_ref):
    f = pl.program_id(2)
    x = x_ref[0]
    m = m_ref[0]

    @pl.when(f == 0)
    def _():
        acc_ref[...] = jnp.zeros_like(acc_ref)

    u = x * (1.0 + m[4:5]) + m[3:4]
    a = jnp.square(jnp.maximum(_dot(u, w1_ref[...]), 0.0))
    acc_ref[...] += _dot(a, w2_ref[...])

    @pl.when(f == pl.num_programs(2) - 1)
    def _():
        o_ref[0] = _res_ln(x, m[5:6], acc_ref[...], g_ref[...], b_ref[...])


def _ffn(x, mods, w1, w2, g, bb):
    b, t, d = x.shape
    tm = min(t, 1024)
    tf = 512
    return pl.pallas_call(
        _ffn_kernel,
        grid=(b, t // tm, D_FF // tf),
        in_specs=[pl.BlockSpec((1, tm, d), lambda i, j, k: (i, j, 0)),
                  pl.BlockSpec((1, 8, d), lambda i, j, k: (i, 0, 0)),
                  pl.BlockSpec((d, tf), lambda i, j, k: (0, k)),
                  pl.BlockSpec((tf, d), lambda i, j, k: (k, 0)),
                  pl.BlockSpec((1, d), lambda i, j, k: (0, 0)),
                  pl.BlockSpec((1, d), lambda i, j, k: (0, 0))],
        out_specs=pl.BlockSpec((1, tm, d), lambda i, j, k: (i, j, 0)),
        out_shape=jax.ShapeDtypeStruct((b, t, d), F32),
        scratch_shapes=[pltpu.VMEM((tm, d), F32)],
        compiler_params=_cparams(("arbitrary", "arbitrary", "arbitrary")),
        name="ffn",
    )(x, mods, w1, w2, g, bb)


def _gla_out_kernel(o_ref, gate_ref, x_ref, m_ref, ng_ref, w_ref, g_ref, b_ref, out_ref):
    o = o_ref[0]
    ng = ng_ref[...]
    parts = []
    for h in range(GLA_HEADS):
        oh = o[:, h * GLA_HV:(h + 1) * GLA_HV]
        r = lax.rsqrt(jnp.mean(oh * oh, -1, keepdims=True) + RMS_EPS)
        parts.append(oh * r * ng)
    z = jnp.concatenate(parts, axis=-1) * _silu(gate_ref[0])
    y = _dot(z, w_ref[...])
    out_ref[0] = _res_ln(x_ref[0], m_ref[0][2:3], y, g_ref[...], b_ref[...])


def _gla_out(o, pmain, x, mods, ng, w, g, bb):
    b, t, d = x.shape
    tm = min(t, 512)
    gate_blk = (2 * GLA_DK) // GLA_DV + 1
    return pl.pallas_call(
        _gla_out_kernel,
        grid=(b, t // tm),
        in_specs=[pl.BlockSpec((1, tm, GLA_DV), lambda i, j: (i, j, 0)),
                  pl.BlockSpec((1, tm, GLA_DV), lambda i, j: (i, j, gate_blk)),
                  pl.BlockSpec((1, tm, d), lambda i, j: (i, j, 0)),
                  pl.BlockSpec((1, 8, d), lambda i, j: (i, 0, 0)),
                  pl.BlockSpec((1, GLA_HV), lambda i, j: (0, 0)),
                  pl.BlockSpec((GLA_DV, d), lambda i, j: (0, 0)),
                  pl.BlockSpec((1, d), lambda i, j: (0, 0)),
                  pl.BlockSpec((1, d), lambda i, j: (0, 0))],
        out_specs=pl.BlockSpec((1, tm, d), lambda i, j: (i, j, 0)),
        out_shape=jax.ShapeDtypeStruct((b, t, d), F32),
        compiler_params=_cparams(("arbitrary", "arbitrary")),
        name="gla_out",
    )(o, pmain, x, mods, ng, w, g, bb)


def _ssd_out_kernel(y_ref, z_ref, x_ref, m_ref, ng_ref, w_ref, g_ref, b_ref, out_ref):
    yz = y_ref[0] * _silu(z_ref[0])
    r = lax.rsqrt(jnp.mean(yz * yz, -1, keepdims=True) + RMS_EPS)
    y = _dot(yz * r * ng_ref[...], w_ref[...])
    out_ref[0] = _res_ln(x_ref[0], m_ref[0][2:3], y, g_ref[...], b_ref[...])


def _ssd_out(y, pmain, x, mods, ng, w, g, bb):
    b, t, d = x.shape
    tm = min(t, 512)
    return pl.pallas_call(
        _ssd_out_kernel,
        grid=(b, t // tm),
        in_specs=[pl.BlockSpec((1, tm, SSD_DI), lambda i, j: (i, j, 0)),
                  pl.BlockSpec((1, tm, SSD_DI), lambda i, j: (i, j, 0)),
                  pl.BlockSpec((1, tm, d), lambda i, j: (i, j, 0)),
                  pl.BlockSpec((1, 8, d), lambda i, j: (i, 0, 0)),
                  pl.BlockSpec((1, SSD_DI), lambda i, j: (0, 0)),
                  pl.BlockSpec((SSD_DI, d), lambda i, j: (0, 0)),
                  pl.BlockSpec((1, d), lambda i, j: (0, 0)),
                  pl.BlockSpec((1, d), lambda i, j: (0, 0))],
        out_specs=pl.BlockSpec((1, tm, d), lambda i, j: (i, j, 0)),
        out_shape=jax.ShapeDtypeStruct((b, t, d), F32),
        compiler_params=_cparams(("arbitrary", "arbitrary")),
        name="ssd_out",
    )(y, pmain, x, mods, ng, w, g, bb)


def _hy_out_kernel(z_ref, x_ref, m_ref, w_ref, g_ref, b_ref, out_ref):
    y = _dot(z_ref[0], w_ref[...])
    out_ref[0] = _res_ln(x_ref[0], m_ref[0][2:3], y, g_ref[...], b_ref[...])


def _hy_out(z, x, mods, w, g, bb):
    b, t, d = x.shape
    tm = min(t, 512)
    return pl.pallas_call(
        _hy_out_kernel,
        grid=(b, t // tm),
        in_specs=[pl.BlockSpec((1, tm, d), lambda i, j: (i, j, 0)),
                  pl.BlockSpec((1, tm, d), lambda i, j: (i, j, 0)),
                  pl.BlockSpec((1, 8, d), lambda i, j: (i, 0, 0)),
                  pl.BlockSpec((d, d), lambda i, j: (0, 0)),
                  pl.BlockSpec((1, d), lambda i, j: (0, 0)),
                  pl.BlockSpec((1, d), lambda i, j: (0, 0))],
        out_specs=pl.BlockSpec((1, tm, d), lambda i, j: (i, j, 0)),
        out_shape=jax.ShapeDtypeStruct((b, t, d), F32),
        compiler_params=_cparams(("arbitrary", "arbitrary")),
        name="hy_out",
    )(z, x, mods, w, g, bb)


def _gla_consts():
    c, s = GLA_CHUNK, GLA_SUB
    i = np.arange(c)
    tril = (i[:, None] >= i[None, :]).astype(np.float32)
    triu = (i[:, None] <= i[None, :]).astype(np.float32)
    ns = c // s
    rows = np.arange(ns * s * s)
    blk, jj, ii = rows // (s * s), (rows // s) % s, rows % s
    col = np.arange(c)
    hit = col[None, :] == (blk * s + jj)[:, None]
    sel_f = (hit & (ii >= jj)[:, None]).astype(np.float32)
    sel_b = (hit & (ii <= jj)[:, None]).astype(np.float32)
    eye = np.eye(GLA_HV, dtype=np.float32)
    return (jnp.asarray(tril, BF16), jnp.asarray(triu, BF16), jnp.asarray(sel_f), jnp.asarray(sel_b),
            jnp.asarray(eye, BF16))


def _gla_chunk(q, k, v, ga, s_ref, tri, sel, eye, fwd):
    c, s = GLA_CHUNK, GLA_SUB
    ns = c // s
    cum = _dot_exact_l(tri, ga)
    tot = cum[c - 1:c] if fwd else cum[0:1]
    qt = q * jnp.exp(cum)
    kt = k * jnp.exp(tot - cum)
    st = s_ref[...]
    o = _dot_nt(qt, st)
    v_t = _dot_nt(eye, v)
    s_ref[...] = st * jnp.exp(tot) + _dot(v_t, kt)
    rowi = lax.broadcasted_iota(jnp.int32, (c, 1), 0)
    blocks = []
    for bi in range(ns):
        lo = bi * s
        if fwd:
            if bi == 0:
                blocks.append(jnp.zeros((s, c), F32))
                continue
            ref = cum[lo - 1:lo]
            live = rowi < lo
        else:
            if bi == ns - 1:
                blocks.append(jnp.zeros((s, c), F32))
                continue
            ref = cum[lo + s:lo + s + 1]
            live = rowi >= lo + s
        qh = q[lo:lo + s] * jnp.exp(cum[lo:lo + s] - ref)
        kh = jnp.where(live, k * jnp.exp(jnp.minimum(ref - cum, 0.0)), 0.0)
        blocks.append(_dot_nt(qh, kh))
    attn = jnp.concatenate(blocks, axis=0)
    pieces = []
    for bi in range(ns):
        lo = bi * s
        qb, cb = q[lo:lo + s], cum[lo:lo + s]
        for j in range(s):
            pieces.append(qb * jnp.exp(jnp.minimum(cb - cum[lo + j:lo + j + 1], 0.0)))
    qs = jnp.concatenate(pieces, axis=0)
    mb = _dot_nt(qs, k) * sel
    dblocks = []
    for bi in range(ns):
        acc = None
        for j in range(s):
            r0 = (bi * s + j) * s
            piece = mb[r0:r0 + s]
            acc = piece if acc is None else acc + piece
        dblocks.append(acc)
    attn = attn + jnp.concatenate(dblocks, axis=0)
    return o + _dot(attn, v)


def _gla_kernel(q_ref, k_ref, v_ref, a_ref, w2_ref, b2_ref, s0_ref,
                tl_ref, tu_ref, self_ref, selb_ref, eye_ref,
                o_ref, s_ref, gaf, gab, sf, sb, *, t):
    c = GLA_CHUNK
    nc = t // c
    pb = min(t, 512)

    def prep(i, carry):
        r = pl.multiple_of(i * pb, pb)
        a = a_ref[0, pl.ds(r, pb), :]
        for z, ga in ((0, gaf), (1, gab)):
            logit = _dot_f32(a, w2_ref[z]) + b2_ref[z]
            ga[pl.ds(r, pb), :] = _log_sigmoid(logit) * (1.0 / GLA_GATE_NORM)
        o_ref[0, pl.ds(r, pb), :] = jnp.zeros((pb, GLA_HV), F32)
        return carry

    lax.fori_loop(0, t // pb, prep, 0)
    sf[...] = s0_ref[0, 0, 0]
    sb[...] = s0_ref[0, 0, 1]
    scale = GLA_HK ** -0.5

    def body(ci, carry):
        rf = pl.multiple_of(ci * c, c)
        rb = pl.multiple_of((nc - 1 - ci) * c, c)
        for r, ga, st, tri_ref, sel_ref, fwd in ((rf, gaf, sf, tl_ref, self_ref, True),
                                                 (rb, gab, sb, tu_ref, selb_ref, False)):
            q = q_ref[0, pl.ds(r, c), :] * scale
            k = k_ref[0, pl.ds(r, c), :]
            v = v_ref[0, pl.ds(r, c), :]
            o = _gla_chunk(q, k, v, ga[pl.ds(r, c), :], st, tri_ref[...], sel_ref[...], eye_ref[...], fwd)
            o_ref[0, pl.ds(r, c), :] += o
        return carry

    lax.fori_loop(0, nc, body, 0)
    s_ref[0, 0, 0] = sf[...]
    s_ref[0, 0, 1] = sb[...]


def _gla_scan(pmain, pa, w2p, b2p, s0):
    b, t, _ = pmain.shape
    h = GLA_HEADS
    consts = _gla_consts()
    kblk = GLA_DK // GLA_HK
    vblk = (2 * GLA_DK) // GLA_HV
    cspec = lambda a: pl.BlockSpec(a.shape, lambda i, j: (0,) * a.ndim)
    return pl.pallas_call(
        functools.partial(_gla_kernel, t=t),
        grid=(b, h),
        in_specs=[pl.BlockSpec((1, t, GLA_HK), lambda i, j: (i, 0, j)),
                  pl.BlockSpec((1, t, GLA_HK), lambda i, j: (i, 0, kblk + j)),
                  pl.BlockSpec((1, t, GLA_HV), lambda i, j: (i, 0, vblk + j)),
                  pl.BlockSpec((1, t, LANE), lambda i, j: (i, 0, 0)),
                  pl.BlockSpec((2, LANE, GLA_HK), lambda i, j: (0, 0, j)),
                  pl.BlockSpec((2, 1, GLA_HK), lambda i, j: (0, 0, j)),
                  pl.BlockSpec((1, 1, 2, GLA_HV, GLA_HK), lambda i, j: (i, j, 0, 0, 0))]
                 + [cspec(a) for a in consts],
        out_specs=[pl.BlockSpec((1, t, GLA_HV), lambda i, j: (i, 0, j)),
                   pl.BlockSpec((1, 1, 2, GLA_HV, GLA_HK), lambda i, j: (i, j, 0, 0, 0))],
        out_shape=[jax.ShapeDtypeStruct((b, t, GLA_DV), F32),
                   jax.ShapeDtypeStruct((b, h, 2, GLA_HV, GLA_HK), F32)],
        scratch_shapes=[pltpu.VMEM((t, GLA_HK), F32), pltpu.VMEM((t, GLA_HK), F32),
                        pltpu.VMEM((GLA_HV, GLA_HK), F32), pltpu.VMEM((GLA_HV, GLA_HK), F32)],
        compiler_params=_cparams(("arbitrary", "arbitrary")),
        name="gla_scan",
    )(pmain, pmain, pmain, pa, w2p, b2p, s0, *consts)


def _ssd_consts():
    c = SSD_CHUNK
    i = np.arange(c)
    tril = (i[:, None] >= i[None, :]).astype(np.float32)
    triu = (i[:, None] <= i[None, :]).astype(np.float32)
    eye = np.eye(SSD_STATE, dtype=np.float32)
    sel16 = np.eye(16, LANE, dtype=np.float32)
    pg = np.zeros((SSD_GROUPS, LANE, LANE), np.float32)
    for g in range(SSD_GROUPS):
        for q in range(2 * SSD_REP):
            z, r = divmod(q, SSD_REP)
            pg[g, z * SSD_HEADS + g * SSD_REP + r, q] = 1.0
    e4 = np.zeros((2, LANE, SSD_GW), np.float32)
    for z in range(2):
        for r in range(SSD_REP):
            e4[z, z * SSD_REP + r, r * SSD_HEADDIM:(r + 1) * SSD_HEADDIM] = 1.0
    lm = np.zeros((SSD_REP * c, SSD_GW), np.float32)
    for r in range(SSD_REP):
        lm[r * c:(r + 1) * c, r * SSD_HEADDIM:(r + 1) * SSD_HEADDIM] = 1.0
    return (jnp.asarray(tril, BF16), jnp.asarray(triu, BF16), jnp.asarray(eye, BF16),
            jnp.asarray(sel16, BF16), jnp.asarray(pg, BF16), jnp.asarray(e4, BF16), jnp.asarray(lm))


def _conv_block(in_ref, w, bias, i, nblk, rows, t, taps):
    r = pl.multiple_of(i * rows, rows)
    cur = in_ref[0, pl.ds(r, rows), :]
    rp = pl.multiple_of(jnp.maximum(r - SUBLANE, 0), SUBLANE)
    rn = pl.multiple_of(jnp.minimum(r + rows, t - SUBLANE), SUBLANE)
    prev = jnp.where(i > 0, in_ref[0, pl.ds(rp, SUBLANE), :], 0.0)
    nxt = jnp.where(i < nblk - 1, in_ref[0, pl.ds(rn, SUBLANE), :], 0.0)
    ext = jnp.concatenate([prev, cur, nxt], axis=0)
    half = taps // 2
    acc = bias
    for j in range(taps):
        off = SUBLANE - half + j
        acc = acc + w[j:j + 1] * ext[off:off + rows]
    return r, acc


def _ssd_chunk(xg, bc, cc, ds, das, s_ref, tri, eye, sel16, e4z, lm, z, fwd):
    c = SSD_CHUNK
    cum = _dot_exact_l(tri, das)
    cum_t = _dot_exact_nt(sel16, cum)
    dt_e = _dot_exact_r(ds, e4z)
    cum_e = _dot_exact_r(cum, e4z)
    tot_e = cum_e[c - 1:c] if fwd else cum_e[0:1]
    xdt = xg * dt_e
    cb = _dot_nt(cc, bc)
    ii = lax.broadcasted_iota(jnp.int32, (c, c), 0)
    jj = lax.broadcasted_iota(jnp.int32, (c, c), 1)
    mask = (ii >= jj) if fwd else (ii <= jj)
    ms, xs = [], []
    for r in range(SSD_REP):
        q = z * SSD_REP + r
        seg = cum[:, q:q + 1] - cum_t[q:q + 1, :]
        ms.append(cb * jnp.where(mask, jnp.exp(jnp.minimum(seg, 0.0)), 0.0))
        xs.append(xdt)
    mcat = jnp.concatenate(ms, axis=1)
    xbd = jnp.concatenate(xs, axis=0) * lm
    st = s_ref[...]
    y = _dot(mcat, xbd) + _dot(cc, st) * jnp.exp(cum_e)
    bc_t = _dot_nt(eye, bc)
    s_ref[...] = st * jnp.exp(tot_e) + _dot(bc_t, xdt * jnp.exp(tot_e - cum_e))
    return y


def _ssd_kernel(x_ref, bm_ref, cm_ref, dt_ref, wx_ref, wb_ref, wc_ref, bx_ref, bb_ref, bcb_ref,
                dtb_ref, alog_ref, dsk_ref, s0_ref,
                tl_ref, tu_ref, eye_ref, sel16_ref, pg_ref, e4_ref, lm_ref,
                y_ref, s_ref, xc, bcs, ccs, dsel, dasel, sf, sb, *, t):
    c = SSD_CHUNK
    nc = t // c
    pb = min(t, 512)
    nblk = t // pb

    def prep(i, carry):
        for in_ref, w_ref, b_ref, out in ((x_ref, wx_ref, bx_ref, xc), (bm_ref, wb_ref, bb_ref, bcs),
                                          (cm_ref, wc_ref, bcb_ref, ccs)):
            r, acc = _conv_block(in_ref, w_ref[...], b_ref[...], i, nblk, pb, t, SSD_CONV)
            out[pl.ds(r, pb), :] = _silu(acc)
        r = pl.multiple_of(i * pb, pb)
        dtp = _softplus(dt_ref[0, pl.ds(r, pb), :] + dtb_ref[...])
        a = -jnp.exp(alog_ref[...])
        dsel[pl.ds(r, pb), :] = _dot_exact_r(dtp, pg_ref[0])
        dasel[pl.ds(r, pb), :] = _dot_exact_r(dtp * a, pg_ref[0])
        y_ref[0, pl.ds(r, pb), :] = xc[pl.ds(r, pb), :] * dsk_ref[...]
        return carry

    lax.fori_loop(0, nblk, prep, 0)
    sf[...] = s0_ref[0, 0, 0]
    sb[...] = s0_ref[0, 0, 1]

    def body(ci, carry):
        rf = pl.multiple_of(ci * c, c)
        rb = pl.multiple_of((nc - 1 - ci) * c, c)
        for r, st, tri_ref, z, fwd in ((rf, sf, tl_ref, 0, True), (rb, sb, tu_ref, 1, False)):
            y = _ssd_chunk(xc[pl.ds(r, c), :], bcs[pl.ds(r, c), :], ccs[pl.ds(r, c), :],
                           dsel[pl.ds(r, c), :], dasel[pl.ds(r, c), :], st,
                           tri_ref[...], eye_ref[...], sel16_ref[...], e4_ref[z], lm_ref[...], z, fwd)
            y_ref[0, pl.ds(r, c), :] += y
        return carry

    lax.fori_loop(0, nc, body, 0)
    s_ref[0, 0, 0] = sf[...]
    s_ref[0, 0, 1] = sb[...]


def _ssd_scan(pmain, pdt, conv_w, conv_b, dtb, alog, dskip, s0):
    b, t, _ = pmain.shape
    g = SSD_GROUPS
    tril, triu, eye, sel16, pg, e4, lm = _ssd_consts()
    xblk = SSD_DI // SSD_GW
    bblk = (2 * SSD_DI) // SSD_STATE
    cblk = bblk + SSD_GN // SSD_STATE
    wbblk = SSD_DI // SSD_STATE
    wcblk = wbblk + SSD_GN // SSD_STATE
    cspec = lambda a: pl.BlockSpec(a.shape, lambda i, j: (0,) * a.ndim)
    return pl.pallas_call(
        functools.partial(_ssd_kernel, t=t),
        grid=(b, g),
        in_specs=[pl.BlockSpec((1, t, SSD_GW), lambda i, j: (i, 0, xblk + j)),
                  pl.BlockSpec((1, t, SSD_STATE), lambda i, j: (i, 0, bblk + j)),
                  pl.BlockSpec((1, t, SSD_STATE), lambda i, j: (i, 0, cblk + j)),
                  pl.BlockSpec((1, t, LANE), lambda i, j: (i, 0, 0)),
                  pl.BlockSpec((SSD_CONV, SSD_GW), lambda i, j: (0, j)),
                  pl.BlockSpec((SSD_CONV, SSD_STATE), lambda i, j: (0, wbblk + j)),
                  pl.BlockSpec((SSD_CONV, SSD_STATE), lambda i, j: (0, wcblk + j)),
                  pl.BlockSpec((1, SSD_GW), lambda i, j: (0, j)),
                  pl.BlockSpec((1, SSD_STATE), lambda i, j: (0, wbblk + j)),
                  pl.BlockSpec((1, SSD_STATE), lambda i, j: (0, wcblk + j)),
                  pl.BlockSpec((1, LANE), lambda i, j: (0, 0)),
                  pl.BlockSpec((1, LANE), lambda i, j: (0, 0)),
                  pl.BlockSpec((1, SSD_GW), lambda i, j: (0, j)),
                  pl.BlockSpec((1, 1, 2, SSD_STATE, SSD_GW), lambda i, j: (i, j, 0, 0, 0)),
                  cspec(tril), cspec(triu), cspec(eye), cspec(sel16),
                  pl.BlockSpec((1, LANE, LANE), lambda i, j: (j, 0, 0)),
                  cspec(e4), cspec(lm)],
        out_specs=[pl.BlockSpec((1, t, SSD_GW), lambda i, j: (i, 0, j)),
                   pl.BlockSpec((1, 1, 2, SSD_STATE, SSD_GW), lambda i, j: (i, j, 0, 0, 0))],
        out_shape=[jax.ShapeDtypeStruct((b, t, SSD_DI), F32),
                   jax.ShapeDtypeStruct((b, g, 2, SSD_STATE, SSD_GW), F32)],
        scratch_shapes=[pltpu.VMEM((t, SSD_GW), F32), pltpu.VMEM((t, SSD_STATE), F32),
                        pltpu.VMEM((t, SSD_STATE), F32), pltpu.VMEM((t, LANE), F32),
                        pltpu.VMEM((t, LANE), F32),
                        pltpu.VMEM((SSD_STATE, SSD_GW), F32), pltpu.VMEM((SSD_STATE, SSD_GW), F32)],
        compiler_params=_cparams(("arbitrary", "arbitrary")),
        name="ssd_scan",
    )(pmain, pmain, pmain, pdt, conv_w, conv_w, conv_w, conv_b, conv_b, conv_b,
      dtb, alog, dskip, s0, tril, triu, eye, sel16, pg, e4, lm)


def _conv3_kernel(p_ref, w_ref, b_ref, *out_refs, t):
    pb = min(t, 512)
    nblk = t // pb

    def blk(i, carry):
        r, acc = _conv_block(p_ref, w_ref[...], b_ref[...], i, nblk, pb, t, HY_SHORT)
        out_refs[0][0, pl.ds(r, pb), :] = acc
        if len(out_refs) > 1:
            out_refs[1][0, pl.ds(r, pb), :] = acc.astype(BF16)
        return carry

    lax.fori_loop(0, nblk, blk, 0)


def _conv3(p, w, bias, col0, ncols, with_bf16):
    b, t, _ = p.shape
    tn = 256
    off = col0 // tn
    out_shape = [jax.ShapeDtypeStruct((b, t, ncols), F32)]
    out_specs = [pl.BlockSpec((1, t, tn), lambda i, j: (i, 0, j))]
    if with_bf16:
        out_shape.append(jax.ShapeDtypeStruct((b, t, ncols), BF16))
        out_specs.append(pl.BlockSpec((1, t, tn), lambda i, j: (i, 0, j)))
    return pl.pallas_call(
        functools.partial(_conv3_kernel, t=t),
        grid=(b, ncols // tn),
        in_specs=[pl.BlockSpec((1, t, tn), lambda i, j: (i, 0, off + j)),
                  pl.BlockSpec((HY_SHORT, tn), lambda i, j: (0, off + j)),
                  pl.BlockSpec((1, tn), lambda i, j: (0, off + j))],
        out_specs=out_specs,
        out_shape=out_shape,
        compiler_params=_cparams(("arbitrary", "arbitrary")),
        name="hy_conv3",
    )(p, w, bias)


def _hy_pos_emb(l):
    bands = (HY_EMB - 1) // 2
    t = np.linspace(0.0, 1.0, l)[:, None]
    w = 2 * math.pi * np.arange(l, dtype=np.float64)[:, None] / l
    ang = np.linspace(1e-4, bands - 1, bands)[None, :] * w
    z = np.concatenate([t, np.cos(ang), -np.sin(ang)], axis=-1)
    zp = np.zeros((l, LANE), np.float32)
    zp[:, :HY_EMB] = z
    return zp


def _filt_kernel(z_ref, w1_ref, b1_ref, w2_ref, b2_ref, w3_ref, b3_ref, w4_ref, fr_ref, dl_ref, o_ref, *, l, tl):
    fr = fr_ref[...]
    h = jnp.sin(fr * (_dot_f32(z_ref[...], w1_ref[...]) + b1_ref[...]))
    h = jnp.sin(fr * (_dot_f32(h, w2_ref[...]) + b2_ref[...]))
    h = jnp.sin(fr * (_dot_f32(h, w3_ref[...]) + b3_ref[...]))
    hh = _dot_f32(h, w4_ref[...])
    row = lax.broadcasted_iota(jnp.int32, (tl, 1), 0) + pl.program_id(0) * tl
    tt = row.astype(F32) * (1.0 / (l - 1))
    hh = hh * jnp.exp(-tt * dl_ref[...])
    drop = jnp.logical_and(row == 0, (pl.program_id(1) % 2) == 1)
    o_ref[...] = jnp.where(drop, 0.0, hh)


def _hy_filters(l, w1, b1, w2, b2, w3, b3, w4, freq):
    d = D_MODEL
    tl = min(l, 512)
    pad2 = lambda a: jnp.zeros((LANE, LANE), F32).at[:a.shape[0], :a.shape[1]].set(a)
    pad1 = lambda a: jnp.zeros((1, LANE), F32).at[0, :a.shape[0]].set(a)
    w4p = jnp.zeros((LANE, HY_ORDER * 2 * d), F32).at[:HY_FW].set(w4)
    deltas = np.abs(np.linspace(math.log(HY_DECAY_TARGET) / HY_FAST_DECAY,
                                math.log(HY_DECAY_TARGET) / HY_SLOW_DECAY, d))
    dl = jnp.asarray(np.tile(deltas, HY_ORDER * 2)[None, :], F32)
    full = lambda a: pl.BlockSpec(a.shape, lambda i, j: (0, 0))
    small = [pad2(w1), pad1(b1), pad2(w2), pad1(b2), pad2(w3), pad1(b3)]
    return pl.pallas_call(
        functools.partial(_filt_kernel, l=l, tl=tl),
        grid=(l // tl, HY_ORDER * 2),
        in_specs=[pl.BlockSpec((tl, LANE), lambda i, j: (i, 0))] + [full(a) for a in small]
                 + [pl.BlockSpec((LANE, d), lambda i, j: (0, j)), pl.BlockSpec((1, LANE), lambda i, j: (0, 0)),
                    pl.BlockSpec((1, d), lambda i, j: (0, j))],
        out_specs=pl.BlockSpec((tl, d), lambda i, j: (i, j)),
        out_shape=jax.ShapeDtypeStruct((l, HY_ORDER * 2 * d), F32),
        compiler_params=_cparams(("arbitrary", "arbitrary")),
        name="hy_filt",
    )(jnp.asarray(_hy_pos_emb(l)), *small, w4p, pad1(freq), dl)


def _dft_tables_np(l):
    n = 2 * l
    k = np.arange(l)
    ang = 2 * math.pi * ((k[:, None] * k[None, :]) % n) / n
    cm = np.cos(ang)
    sm = -np.sin(ang)
    alt = np.where(k % 2 == 0, 1.0, -1.0)
    s_fwd = sm.copy()
    s_fwd[0, :] = alt
    s_inv = sm.copy()
    s_inv[:, 0] = alt
    return cm, s_fwd, s_inv


def _dft_tab_kernel(ca_ref, sa_ref, cb_ref, sb_ref, c_ref, sf_ref, si_ref, *, rb):
    a = pl.program_id(0)
    ca, sa = ca_ref[0], sa_ref[0]
    cb, sb = cb_ref[...], sb_ref[...]
    cm = ca * cb - sa * sb
    sm = -(sa * cb + ca * sb)
    l = cm.shape[1]
    row = lax.broadcasted_iota(jnp.int32, (rb, l), 0)
    col = lax.broadcasted_iota(jnp.int32, (rb, l), 1)
    alt_col = jnp.where(col % 2 == 0, 1.0, -1.0)
    alt_row = jnp.where(row % 2 == 0, 1.0, -1.0)
    c_ref[...] = cm.astype(BF16)
    sf_ref[...] = jnp.where(jnp.logical_and(row == 0, a == 0), alt_col, sm).astype(BF16)
    si_ref[...] = jnp.where(col == 0, alt_row, sm).astype(BF16)


def _dft_tables(l):
    if l <= 512:
        return tuple(jnp.asarray(m, BF16) for m in _dft_tables_np(l))
    rb = 64
    na = l // rb
    n = 2 * l
    nn = np.arange(l)
    aa = np.arange(na)
    bb = np.arange(rb)
    ang_a = 2 * math.pi * ((aa[:, None] * rb * nn[None, :]) % n) / n
    ang_b = 2 * math.pi * ((bb[:, None] * nn[None, :]) % n) / n
    ca = jnp.asarray(np.cos(ang_a)[:, None, :], F32)
    sa = jnp.asarray(np.sin(ang_a)[:, None, :], F32)
    cb = jnp.asarray(np.cos(ang_b), F32)
    sb = jnp.asarray(np.sin(ang_b), F32)
    rowspec = pl.BlockSpec((1, 1, l), lambda a: (a, 0, 0))
    tabspec = pl.BlockSpec((rb, l), lambda a: (0, 0))
    outspec = pl.BlockSpec((rb, l), lambda a: (a, 0))
    return tuple(pl.pallas_call(
        functools.partial(_dft_tab_kernel, rb=rb),
        grid=(na,),
        in_specs=[rowspec, rowspec, tabspec, tabspec],
        out_specs=[outspec] * 3,
        out_shape=[jax.ShapeDtypeStruct((l, l), BF16)] * 3,
        compiler_params=_cparams(("arbitrary",)),
        name="dft_tables",
    )(ca, sa, cb, sb))


def _spec_filt_kernel(c_ref, s_ref, a_ref, b_ref, h_ref, accr, acci, accn, *, l):
    kk = pl.program_id(2)

    @pl.when(kk == 0)
    def _():
        accr[...] = jnp.zeros_like(accr)
        acci[...] = jnp.zeros_like(acci)
        accn[...] = jnp.zeros_like(accn)

    a, bw = a_ref[...], b_ref[...]
    sm = (a + bw).astype(BF16)
    df = (a - bw).astype(BF16)
    accr[...] += jnp.dot(c_ref[...], sm, preferred_element_type=F32)
    acci[...] += jnp.dot(s_ref[...], df, preferred_element_type=F32)
    accn[...] += jnp.dot(s_ref[...], sm, preferred_element_type=F32)

    @pl.when(kk == pl.num_programs(2) - 1)
    def _():
        tm = accr.shape[0]
        row0 = (lax.broadcasted_iota(jnp.int32, (tm, 1), 0) + pl.program_id(0) * tm) == 0
        scale = jnp.where(row0, 0.5 / l, 1.0 / l)
        h_ref[0] = accr[...] * scale
        h_ref[1] = jnp.where(row0, accn[...], acci[...]) * scale


def _spec_filt(cm, s_fwd, hh, l):
    d = D_MODEL
    tm, tn, tk = min(l, 512), 512, min(l, 1024)
    nd = d // tn
    return pl.pallas_call(
        functools.partial(_spec_filt_kernel, l=l),
        grid=(l // tm, HY_ORDER * nd, l // tk),
        in_specs=[pl.BlockSpec((tm, tk), lambda m, j, k: (m, k)),
                  pl.BlockSpec((tm, tk), lambda m, j, k: (m, k)),
                  pl.BlockSpec((tk, tn), lambda m, j, k: (k, (j // nd) * 2 * nd + j % nd)),
                  pl.BlockSpec((tk, tn), lambda m, j, k: (k, (j // nd) * 2 * nd + nd + j % nd))],
        out_specs=pl.BlockSpec((2, tm, tn), lambda m, j, k: (0, m, j)),
        out_shape=jax.ShapeDtypeStruct((2, l, HY_ORDER * d), F32),
        scratch_shapes=[pltpu.VMEM((tm, tn), F32)] * 3,
        compiler_params=_cparams(("arbitrary", "arbitrary", "arbitrary")),
        name="hy_spec_filt",
    )(cm, s_fwd, hh, hh)


def _spec_sig_kernel(c_ref, s_ref, u_ref, h_ref, y_ref, accr, acci):
    kk = pl.program_id(2)

    @pl.when(kk == 0)
    def _():
        accr[...] = jnp.zeros_like(accr)
        acci[...] = jnp.zeros_like(acci)

    u = u_ref[0]
    accr[...] += jnp.dot(c_ref[...], u, preferred_element_type=F32)
    acci[...] += jnp.dot(s_ref[...], u, preferred_element_type=F32)

    @pl.when(kk == pl.num_programs(2) - 1)
    def _():
        tm = accr.shape[0]
        row0 = (lax.broadcasted_iota(jnp.int32, (tm, 1), 0) + pl.program_id(0) * tm) == 0
        xr, xi = accr[...], acci[...]
        hr, hi = h_ref[0], h_ref[1]
        y_ref[0] = (xr * hr - jnp.where(row0, 0.0, xi * hi)).astype(BF16)
        y_ref[1] = jnp.where(row0, xi * hi, xr * hi + xi * hr).astype(BF16)


def _spec_sig(cm, s_fwd, ub, hspec, order, l):
    b, _, d = ub.shape
    tm, tn, tk = min(l, 1024), 512, min(l, 1024)
    nd = d // tn
    return pl.pallas_call(
        _spec_sig_kernel,
        grid=(l // tm, b * nd, l // tk),
        in_specs=[pl.BlockSpec((tm, tk), lambda m, j, k: (m, k)),
                  pl.BlockSpec((tm, tk), lambda m, j, k: (m, k)),
                  pl.BlockSpec((1, tk, tn), lambda m, j, k: (j // nd, k, j % nd)),
                  pl.BlockSpec((2, tm, tn), lambda m, j, k: (0, m, order * nd + j % nd))],
        out_specs=pl.BlockSpec((2, tm, tn), lambda m, j, k: (0, m, j)),
        out_shape=jax.ShapeDtypeStruct((2, l, b * d), BF16),
        scratch_shapes=[pltpu.VMEM((tm, tn), F32)] * 2,
        compiler_params=_cparams(("arbitrary", "arbitrary", "arbitrary")),
        name="hy_spec_sig",
    )(cm, s_fwd, ub, hspec)


def _inv_kernel(c_ref, s_ref, y_ref, u_ref, g_ref, bias_ref, *refs):
    out_refs, acc = refs[:-1], refs[-1]
    kk = pl.program_id(2)

    @pl.when(kk == 0)
    def _():
        acc[...] = jnp.zeros_like(acc)

    acc[...] += (jnp.dot(c_ref[...], y_ref[0], preferred_element_type=F32)
                 + jnp.dot(s_ref[...], y_ref[1], preferred_element_type=F32))

    @pl.when(kk == pl.num_programs(2) - 1)
    def _():
        res = g_ref[0] * (acc[...] + u_ref[0] * bias_ref[...])
        out_refs[0][0] = res
        if len(out_refs) > 1:
            out_refs[1][0] = res.astype(BF16)


def _spec_inv(cm, s_inv, yspec, u, ucol, gate, gcol, bias, with_bf16, l):
    b = u.shape[0]
    d = D_MODEL
    tt, tn, tk = min(l, 1024), 512, min(l, 1024)
    nd = d // tn
    uo, go = ucol // tn, gcol // tn
    out_shape = [jax.ShapeDtypeStruct((b, l, d), F32)]
    out_specs = [pl.BlockSpec((1, tt, tn), lambda t, j, k: (j // nd, t, j % nd))]
    if with_bf16:
        out_shape.append(jax.ShapeDtypeStruct((b, l, d), BF16))
        out_specs.append(pl.BlockSpec((1, tt, tn), lambda t, j, k: (j // nd, t, j % nd)))
    return pl.pallas_call(
        _inv_kernel,
        grid=(l // tt, b * nd, l // tk),
        in_specs=[pl.BlockSpec((tt, tk), lambda t, j, k: (t, k)),
                  pl.BlockSpec((tt, tk), lambda t, j, k: (t, k)),
                  pl.BlockSpec((2, tk, tn), lambda t, j, k: (0, k, j)),
                  pl.BlockSpec((1, tt, tn), lambda t, j, k: (j // nd, t, uo + j % nd)),
                  pl.BlockSpec((1, tt, tn), lambda t, j, k: (j // nd, t, go + j % nd)),
                  pl.BlockSpec((1, tn), lambda t, j, k: (0, j % nd))],
        out_specs=out_specs,
        out_shape=out_shape,
        scratch_shapes=[pltpu.VMEM((tt, tn), F32)],
        compiler_params=_cparams(("arbitrary", "arbitrary", "arbitrary")),
        name="hy_spec_inv",
    )(cm, s_inv, yspec, u, gate, bias)


def _hyena_run(x, mods, w_in, conv_w, conv_b, fw, h_bias):
    b, l, d = x.shape
    p = _proj(x, mods, w_in, 512)
    v, vb = _conv3(p, conv_w, conv_b, 0, d, True)
    x12 = _conv3(p, conv_w, conv_b, d, 2 * d, False)[0]
    cm, s_fwd, s_inv = _dft_tables(l)
    hh = _hy_filters(l, *fw)
    hspec = _spec_filt(cm, s_fwd, hh, l)
    y1 = _spec_sig(cm, s_fwd, vb, hspec, 0, l)
    z, zb = _spec_inv(cm, s_inv, y1, v, 0, x12, 0, h_bias[0:1], True, l)
    y2 = _spec_sig(cm, s_fwd, zb, hspec, 1, l)
    return _spec_inv(cm, s_inv, y2, z, 0, x12, d, h_bias[1:2], False, l)[0]


def _snake(h):
    b, l, ch = h.shape
    rows = l // GRID_W
    g = h.reshape(b, rows, GRID_W, ch)
    odd = (jnp.arange(rows) % 2 == 1)[None, :, None, None]
    return jnp.where(odd, g[:, :, ::-1], g).reshape(b, l, ch)


def _pad_cols(w, n):
    return jnp.zeros((w.shape[0], n), w.dtype).at[:, :w.shape[1]].set(w)


def kernel(x, c, ctx, c_ctx, ada_w, ada_b, ln_g, ln_b, ffn_w1, ffn_w2, gla_w_in, gla_w_a2, gla_b_a2, gla_norm, gla_w_out, ssd_w_in, ssd_conv_w, ssd_conv_b, ssd_dt_bias, ssd_a_log, ssd_d, ssd_norm, ssd_w_out, hy_w_in, hy_conv_w, hy_conv_b, hy_f_w1, hy_f_b1, hy_f_w2, hy_f_b2, hy_f_w3, hy_f_b3, hy_f_w4, hy_f_freq, hy_bias, hy_w_out):
    bsz, _, d = x.shape
    hl = _snake(x)
    hc = ctx
    cvec = jnp.zeros((16, d), F32).at[:bsz].set(c).at[bsz].set(c_ctx)
    mods = _ada(cvec, ada_w, ada_b).reshape(DEPTH, 16, 6, d)
    mods = jnp.concatenate([mods, jnp.zeros((DEPTH, 16, 2, d), F32)], axis=2)

    for i in range(DEPTH):
        kind, j = i % N_MIXERS, i // N_MIXERS
        need_ctx = i < DEPTH - 1
        ml = mods[i, :bsz]
        mc = jnp.broadcast_to(mods[i, bsz][None], (bsz, 8, d))
        g0, b0 = ln_g[i, 0][None], ln_b[i, 0][None]
        g1, b1 = ln_g[i, 1][None], ln_b[i, 1][None]
        w1 = ffn_w1[i].astype(BF16)
        w2 = ffn_w2[i].astype(BF16)
        streams = [(hc, mc, True), (hl, ml, False)]
        if kind == 0:
            w_main = gla_w_in[j][:, :GLA_MAIN].astype(BF16)
            w_a = _pad_cols(gla_w_in[j][:, GLA_MAIN:], LANE).astype(BF16)
            w2p = jnp.zeros((2, LANE, GLA_DK), F32)
            for z in range(2):
                w2p = w2p.at[z, z * GLA_RANK:(z + 1) * GLA_RANK].set(gla_w_a2[j, z])
            b2p = gla_b_a2[j][:, None, :]
            w_out = gla_w_out[j].astype(BF16)
            ng = gla_norm[j][None]
            state = jnp.zeros((bsz, GLA_HEADS, 2, GLA_HV, GLA_HK), F32)
            new = []
            for h, m, is_ctx in streams:
                pmain = _proj(h, m, w_main, 512)
                pa = _proj(h, m, w_a, LANE)
                o, st = _gla_scan(pmain, pa, w2p, b2p, state)
                if is_ctx:
                    state = st
                if is_ctx and not need_ctx:
                    new.append(h)
                    continue
                new.append(_gla_out(o, pmain, h, m, ng, w_out, g0, b0))
            hc, hl = new
        elif kind == 1:
            w_main = ssd_w_in[j][:, :SSD_MAIN].astype(BF16)
            w_dt = _pad_cols(ssd_w_in[j][:, SSD_MAIN:], LANE).astype(BF16)
            dtb = _pad_cols(ssd_dt_bias[j].reshape(1, -1), LANE)
            alog = _pad_cols(ssd_a_log[j].reshape(1, -1), LANE)
            dskip = jnp.repeat(ssd_d[j], SSD_HEADDIM)[None]
            cw = ssd_conv_w[j]
            cbias = ssd_conv_b[j][None]
            w_out = ssd_w_out[j].astype(BF16)
            ng = ssd_norm[j][None]
            state = jnp.zeros((bsz, SSD_GROUPS, 2, SSD_STATE, SSD_GW), F32)
            new = []
            for h, m, is_ctx in streams:
                pmain = _proj(h, m, w_main, 512)
                pdt = _proj(h, m, w_dt, LANE)
                y, st = _ssd_scan(pmain, pdt, cw, cbias, dtb, alog, dskip, state)
                if is_ctx:
                    state = st
                if is_ctx and not need_ctx:
                    new.append(h)
                    continue
                new.append(_ssd_out(y, pmain, h, m, ng, w_out, g0, b0))
            hc, hl = new
        else:
            w_in = hy_w_in[j].astype(BF16)
            w_out = hy_w_out[j].astype(BF16)
            fw = (hy_f_w1[j], hy_f_b1[j], hy_f_w2[j], hy_f_b2[j], hy_f_w3[j], hy_f_b3[j], hy_f_w4[j], hy_f_freq[j])
            new = []
            for h, m, is_ctx in streams:
                if is_ctx and not need_ctx:
                    new.append(h)
                    continue
                zz = _hyena_run(h, m, w_in, hy_conv_w[j], hy_conv_b[j][None], fw, hy_bias[j])
                new.append(_hy_out(zz, h, m, w_out, g0, b0))
            hc, hl = new
        hl = _ffn(hl, ml, w1, w2, g1, b1)
        if need_ctx:
            hc = _ffn(hc, mc, w1, w2, g1, b1)
    return _snake(hl)
```

```python
import functools
import math

import numpy as np
import jax
import jax.numpy as jnp
from jax import lax
from jax.experimental import pallas as pl
from jax.experimental.pallas import tpu as pltpu

F32 = jnp.float32
BF16 = jnp.bfloat16

D_MODEL = 1024
DEPTH = 4
GRID_W = 64
N_MIXERS = 3
D_FF = 4 * D_MODEL
DEEPNORM_ALPHA = (2 * DEPTH) ** 0.25
LN_EPS = 1e-5
RMS_EPS = 1e-6

GLA_HEADS = 4
GLA_DK = D_MODEL // 2
GLA_DV = D_MODEL
GLA_HK = GLA_DK // GLA_HEADS
GLA_HV = GLA_DV // GLA_HEADS
GLA_RANK = 16
GLA_GATE_NORM = 16.0
GLA_CHUNK = 64
GLA_UNROLL = 2
GLA_NEG = -1e30
LOG2E = math.log2(math.e)
GLA_MAIN = 2 * GLA_DK + 2 * GLA_DV

SSD_DI = 2 * D_MODEL
SSD_HEADDIM = 64
SSD_HEADS = SSD_DI // SSD_HEADDIM
SSD_GROUPS = 8
SSD_REP = SSD_HEADS // SSD_GROUPS
SSD_STATE = 128
SSD_CONV = 5
SSD_CHUNK = 128
SSD_UNROLL = 2
SSD_GN = SSD_GROUPS * SSD_STATE
SSD_CONV_DIM = SSD_DI + 2 * SSD_GN
SSD_MAIN = SSD_DI + SSD_CONV_DIM
SSD_GW = SSD_REP * SSD_HEADDIM

HY_ORDER = 2
HY_SHORT = 3
HY_EMB = 33
HY_FW = 64
HY_DECAY_TARGET = 1e-2
HY_FAST_DECAY = 0.3
HY_SLOW_DECAY = 1.5

LANE = 128
SUBLANE = 8
VMEM_LIMIT = 56 * 1024 * 1024


def _cparams(sem):
    return pltpu.CompilerParams(dimension_semantics=sem, vmem_limit_bytes=VMEM_LIMIT)


def _dot(a, b):
    return jnp.dot(a.astype(BF16), b.astype(BF16), preferred_element_type=F32)


def _dot_nt(a, b):
    return lax.dot_general(a.astype(BF16), b.astype(BF16), (((1,), (1,)), ((), ())),
                           preferred_element_type=F32)


def _split3(x):
    hi = x.astype(BF16)
    r1 = x - hi.astype(F32)
    mid = r1.astype(BF16)
    lo = (r1 - mid.astype(F32)).astype(BF16)
    return hi, mid, lo


def _dot_exact_l(m01, x):
    hi, mid, lo = _split3(x)
    d = lambda p: jnp.dot(m01, p, preferred_element_type=F32)
    return d(hi) + d(mid) + d(lo)


def _dot_exact_r(x, m01):
    hi, mid, lo = _split3(x)
    d = lambda p: jnp.dot(p, m01, preferred_element_type=F32)
    return d(hi) + d(mid) + d(lo)


def _dot_split2_r(x, m01):
    hi = x.astype(BF16)
    mid = (x - hi.astype(F32)).astype(BF16)
    d = lambda p: jnp.dot(p, m01, preferred_element_type=F32)
    return d(hi) + d(mid)


def _dot_exact_nt(m01, x):
    hi, mid, lo = _split3(x)
    d = lambda p: lax.dot_general(m01, p, (((1,), (1,)), ((), ())), preferred_element_type=F32)
    return d(hi) + d(mid) + d(lo)


def _dot_f32(a, b):
    ah, am, al = _split3(a)
    bh, bm, bl = _split3(b)
    d = lambda p, q: jnp.dot(p, q, preferred_element_type=F32)
    return (d(ah, bh) + (d(ah, bm) + d(am, bh)) + (d(ah, bl) + d(al, bh) + d(am, bm)))


def _silu(x):
    return x * jax.nn.sigmoid(x)


def _softplus(x):
    return jnp.maximum(x, 0.0) + jnp.log1p(jnp.exp(-jnp.abs(x)))


def _log_sigmoid(x):
    return -_softplus(-x)


def _layer_norm(h, g, b):
    mu = jnp.mean(h, -1, keepdims=True)
    d = h - mu
    var = jnp.mean(d * d, -1, keepdims=True)
    return d * lax.rsqrt(var + LN_EPS) * g + b


def _res_ln(x, gate, y, g, b):
    return _layer_norm(DEEPNORM_ALPHA * x + gate * y, g, b)


def _ada_kernel(c_ref, w_ref, b_ref, o_ref):
    s = _silu(c_ref[...])
    o_ref[0] = _dot_f32(s, w_ref[0]) + b_ref[0]


def _ada(cvec, ada_w, ada_b):
    tn = 1536
    n = 6 * D_MODEL
    return pl.pallas_call(
        _ada_kernel,
        grid=(DEPTH, n // tn),
        in_specs=[pl.BlockSpec((16, D_MODEL), lambda i, j: (0, 0)),
                  pl.BlockSpec((1, D_MODEL, tn), lambda i, j: (i, 0, j)),
                  pl.BlockSpec((1, 1, tn), lambda i, j: (i, 0, j))],
        out_specs=pl.BlockSpec((1, 16, tn), lambda i, j: (i, 0, j)),
        out_shape=jax.ShapeDtypeStruct((DEPTH, 16, n), F32),
        compiler_params=_cparams(("arbitrary", "arbitrary")),
        name="ada",
    )(cvec, ada_w, ada_b.reshape(DEPTH, 1, n))


def _proj_kernel(x_ref, m_ref, w_ref, o_ref):
    m = m_ref[0]
    u = x_ref[0] * (1.0 + m[1:2]) + m[0:1]
    o_ref[0] = _dot(u, w_ref[...])


def _proj(x, mods, w, tn):
    b, t, d = x.shape
    n = w.shape[1]
    tm = min(t, 512)
    return pl.pallas_call(
        _proj_kernel,
        grid=(b, t // tm, n // tn),
        in_specs=[pl.BlockSpec((1, tm, d), lambda i, j, k: (i, j, 0)),
                  pl.BlockSpec((1, 8, d), lambda i, j, k: (i, 0, 0)),
                  pl.BlockSpec((d, tn), lambda i, j, k: (0, k))],
        out_specs=pl.BlockSpec((1, tm, tn), lambda i, j, k: (i, j, k)),
        out_shape=jax.ShapeDtypeStruct((b, t, n), F32),
        compiler_params=_cparams(("arbitrary", "arbitrary", "arbitrary")),
        name="proj",
    )(x, mods, w)


def _ffn_kernel(x_ref, m_ref, w1_ref, w2_ref, g_ref, b_ref, o_ref, acc_ref):
    f = pl.program_id(2)
    x = x_ref[0]
    m = m_ref[0]

    @pl.when(f == 0)
    def _():
        acc_ref[...] = jnp.zeros_like(acc_ref)

    u = x * (1.0 + m[4:5]) + m[3:4]
    a = jnp.square(jnp.maximum(_dot(u, w1_ref[...]), 0.0))
    acc_ref[...] += _dot(a, w2_ref[...])

    @pl.when(f == pl.num_programs(2) - 1)
    def _():
        o_ref[0] = _res_ln(x, m[5:6], acc_ref[...], g_ref[...], b_ref[...])


def _ffn(x, mods, w1, w2, g, bb):
    b, t, d = x.shape
    tm = min(t, 1024)
    tf = 512
    return pl.pallas_call(
        _ffn_kernel,
        grid=(b, t // tm, D_FF // tf),
        in_specs=[pl.BlockSpec((1, tm, d), lambda i, j, k: (i, j, 0)),
                  pl.BlockSpec((1, 8, d), lambda i, j, k: (i, 0, 0)),
                  pl.BlockSpec((d, tf), lambda i, j, k: (0, k)),
                  pl.BlockSpec((tf, d), lambda i, j, k: (k, 0)),
                  pl.BlockSpec((1, d), lambda i, j, k: (0, 0)),
                  pl.BlockSpec((1, d), lambda i, j, k: (0, 0))],
        out_specs=pl.BlockSpec((1, tm, d), lambda i, j, k: (i, j, 0)),
        out_shape=jax.ShapeDtypeStruct((b, t, d), F32),
        scratch_shapes=[pltpu.VMEM((tm, d), F32)],
        compiler_params=_cparams(("arbitrary", "arbitrary", "arbitrary")),
        name="ffn",
    )(x, mods, w1, w2, g, bb)


def _gla_out_kernel(o_ref, gate_ref, x_ref, m_ref, ng_ref, w_ref, g_ref, b_ref, out_ref):
    o = o_ref[0]
    ng = ng_ref[...]
    parts = []
    for h in range(GLA_HEADS):
        oh = o[:, h * GLA_HV:(h + 1) * GLA_HV]
        r = lax.rsqrt(jnp.mean(oh * oh, -1, keepdims=True) + RMS_EPS)
        parts.append(oh * r * ng)
    z = jnp.concatenate(parts, axis=-1) * _silu(gate_ref[0])
    y = _dot(z, w_ref[...])
    out_ref[0] = _res_ln(x_ref[0], m_ref[0][2:3], y, g_ref[...], b_ref[...])


def _gla_out(o, pmain, x, mods, ng, w, g, bb):
    b, t, d = x.shape
    tm = min(t, 512)
    gate_blk = (2 * GLA_DK) // GLA_DV + 1
    return pl.pallas_call(
        _gla_out_kernel,
        grid=(b, t // tm),
        in_specs=[pl.BlockSpec((1, tm, GLA_DV), lambda i, j: (i, j, 0)),
                  pl.BlockSpec((1, tm, GLA_DV), lambda i, j: (i, j, gate_blk)),
                  pl.BlockSpec((1, tm, d), lambda i, j: (i, j, 0)),
                  pl.BlockSpec((1, 8, d), lambda i, j: (i, 0, 0)),
                  pl.BlockSpec((1, GLA_HV), lambda i, j: (0, 0)),
                  pl.BlockSpec((GLA_DV, d), lambda i, j: (0, 0)),
                  pl.BlockSpec((1, d), lambda i, j: (0, 0)),
                  pl.BlockSpec((1, d), lambda i, j: (0, 0))],
        out_specs=pl.BlockSpec((1, tm, d), lambda i, j: (i, j, 0)),
        out_shape=jax.ShapeDtypeStruct((b, t, d), F32),
        compiler_params=_cparams(("arbitrary", "arbitrary")),
        name="gla_out",
    )(o, pmain, x, mods, ng, w, g, bb)


def _ssd_out_kernel(y_ref, z_ref, x_ref, m_ref, ng_ref, w_ref, g_ref, b_ref, out_ref):
    yz = y_ref[0] * _silu(z_ref[0])
    r = lax.rsqrt(jnp.mean(yz * yz, -1, keepdims=True) + RMS_EPS)
    y = _dot(yz * r * ng_ref[...], w_ref[...])
    out_ref[0] = _res_ln(x_ref[0], m_ref[0][2:3], y, g_ref[...], b_ref[...])


def _ssd_out(y, pmain, x, mods, ng, w, g, bb):
    b, t, d = x.shape
    tm = min(t, 512)
    return pl.pallas_call(
        _ssd_out_kernel,
        grid=(b, t // tm),
        in_specs=[pl.BlockSpec((1, tm, SSD_DI), lambda i, j: (i, j, 0)),
                  pl.BlockSpec((1, tm, SSD_DI), lambda i, j: (i, j, 0)),
                  pl.BlockSpec((1, tm, d), lambda i, j: (i, j, 0)),
                  pl.BlockSpec((1, 8, d), lambda i, j: (i, 0, 0)),
                  pl.BlockSpec((1, SSD_DI), lambda i, j: (0, 0)),
                  pl.BlockSpec((SSD_DI, d), lambda i, j: (0, 0)),
                  pl.BlockSpec((1, d), lambda i, j: (0, 0)),
                  pl.BlockSpec((1, d), lambda i, j: (0, 0))],
        out_specs=pl.BlockSpec((1, tm, d), lambda i, j: (i, j, 0)),
        out_shape=jax.ShapeDtypeStruct((b, t, d), F32),
        compiler_params=_cparams(("arbitrary", "arbitrary")),
        name="ssd_out",
    )(y, pmain, x, mods, ng, w, g, bb)


def _hy_out_kernel(z_ref, x_ref, m_ref, w_ref, g_ref, b_ref, out_ref):
    y = _dot(z_ref[0], w_ref[...])
    out_ref[0] = _res_ln(x_ref[0], m_ref[0][2:3], y, g_ref[...], b_ref[...])


def _hy_out(z, x, mods, w, g, bb):
    b, t, d = x.shape
    tm = min(t, 512)
    return pl.pallas_call(
        _hy_out_kernel,
        grid=(b, t // tm),
        in_specs=[pl.BlockSpec((1, tm, d), lambda i, j: (i, j, 0)),
                  pl.BlockSpec((1, tm, d), lambda i, j: (i, j, 0)),
                  pl.BlockSpec((1, 8, d), lambda i, j: (i, 0, 0)),
                  pl.BlockSpec((d, d), lambda i, j: (0, 0)),
                  pl.BlockSpec((1, d), lambda i, j: (0, 0)),
                  pl.BlockSpec((1, d), lambda i, j: (0, 0))],
        out_specs=pl.BlockSpec((1, tm, d), lambda i, j: (i, j, 0)),
        out_shape=jax.ShapeDtypeStruct((b, t, d), F32),
        compiler_params=_cparams(("arbitrary", "arbitrary")),
        name="hy_out",
    )(z, x, mods, w, g, bb)


def _gla_consts(fwd):
    c = GLA_CHUNK
    i = np.arange(c)
    tri = ((i[:, None] >= i[None, :]) if fwd else (i[:, None] <= i[None, :])).astype(np.float32)
    halves = [c >> (s + 1) for s in range(int(math.log2(c)))]
    nl = len(halves)
    wst = np.zeros(((nl + 1) * c, c), np.float32)
    wst[:c] = tri
    negq = np.zeros((nl * c, GLA_HK), np.float32)
    negk = np.zeros((nl * c, GLA_HK), np.float32)
    msk = np.zeros(((nl + 1) * c, c), np.float32)
    msk[:c] = np.eye(c)
    for lv, half in enumerate(halves):
        blk = i // (2 * half)
        upper = (i % (2 * half)) >= half
        ref = blk * 2 * half + (half - 1 if fwd else half)
        wst[(lv + 1) * c:(lv + 2) * c] = tri - tri[ref]
        qside = upper if fwd else ~upper
        negq[lv * c:(lv + 1) * c] = np.where(qside, 0.0, GLA_NEG)[:, None]
        negk[lv * c:(lv + 1) * c] = np.where(~qside, 0.0, GLA_NEG)[:, None]
        msk[(lv + 1) * c:(lv + 2) * c] = ((blk[:, None] == blk[None, :]) & qside[:, None] & (~qside)[None, :])
    return (jnp.asarray(wst, BF16), jnp.asarray(negq), jnp.asarray(negk), jnp.asarray(msk))


def _gla_group(chains, eye):
    c = GLA_CHUNK
    for ch in chains:
        wst = ch["consts"][0]
        ch["est"] = _dot_split2_l(wst, ch["ga"])
    for ch in chains:
        _, negq, negk, _ = ch["consts"]
        q, k, est = ch["q"], ch["k"], ch["est"]
        nl = negq.shape[0] // c
        cum = est[0:c]
        tot = cum[c - 1:c] if ch["fwd"] else cum[0:1]
        ch["qt"] = (q * jnp.exp2(cum)).astype(BF16)
        ch["kt"] = (k * jnp.exp2(tot - cum)).astype(BF16)
        ch["dec"] = jnp.exp2(tot)
        qs, ks = [q.astype(BF16)], [k.astype(BF16)]
        for lv in range(nl):
            e = est[(lv + 1) * c:(lv + 2) * c]
            sl = slice(lv * c, (lv + 1) * c)
            qs.append((q * jnp.exp2(e + negq[sl])).astype(BF16))
            ks.append((k * jnp.exp2(negk[sl] - e)).astype(BF16))
        ch["qs"], ch["ks"] = qs, ks
    for ch in chains:
        ch["ps"] = [_dot_nt(a, b) for a, b in zip(ch["qs"], ch["ks"])]
    for ch in chains:
        msk = ch["consts"][3]
        attn = None
        for lv, p in enumerate(ch["ps"]):
            term = p * msk[lv * c:(lv + 1) * c]
            attn = term if attn is None else attn + term
        ch["attn"] = attn.astype(BF16)
    for ch in chains:
        vb = ch["v"].astype(BF16)
        ch["o"] = jnp.dot(ch["attn"], vb, preferred_element_type=F32)
        v_t = _dot_nt(eye, vb)
        ch["upd"] = jnp.dot(v_t.astype(BF16), ch["kt"], preferred_element_type=F32)


def _dot_split2_l(m, x):
    hi = x.astype(BF16)
    mid = (x - hi.astype(F32)).astype(BF16)
    d = lambda p: jnp.dot(m, p, preferred_element_type=F32)
    return d(hi) + d(mid)


def _gla_kernel(q_ref, k_ref, v_ref, a_ref, w2_ref, b2_ref, s0_ref,
                wf_ref, nqf_ref, nkf_ref, mf_ref, wb_ref, nqb_ref, nkb_ref, mb_ref, eye_ref,
                o_ref, s_ref, gaf, gab, sf, sb, *, t):
    c = GLA_CHUNK
    un = GLA_UNROLL
    nc = t // c
    pb = min(t, 512)

    def prep(i, carry):
        r = pl.multiple_of(i * pb, pb)
        a = a_ref[0, pl.ds(r, pb), :]
        for z, ga in ((0, gaf), (1, gab)):
            logit = _dot_f32(a, w2_ref[z]) + b2_ref[z]
            ga[pl.ds(r, pb), :] = _log_sigmoid(logit) * (LOG2E / GLA_GATE_NORM)
        o_ref[0, pl.ds(r, pb), :] = jnp.zeros((pb, GLA_HV), F32)
        return carry

    lax.fori_loop(0, t // pb, prep, 0)
    sf[...] = s0_ref[0, 0, 0]
    sb[...] = s0_ref[0, 0, 1]
    scale = GLA_HK ** -0.5
    fconst = (wf_ref, nqf_ref, nkf_ref, mf_ref)
    bconst = (wb_ref, nqb_ref, nkb_ref, mb_ref)

    def body(ci, carry):
        chains = []
        for fwd, ga, consts in ((True, gaf, fconst), (False, gab, bconst)):
            cvals = tuple(x[...] for x in consts)
            for u in range(un):
                idx = ci * un + u
                r = pl.multiple_of((idx if fwd else nc - 1 - idx) * c, c)
                chains.append(dict(r=r, fwd=fwd, consts=cvals, ga=ga[pl.ds(r, c), :],
                                   q=q_ref[0, pl.ds(r, c), :] * scale, k=k_ref[0, pl.ds(r, c), :],
                                   v=v_ref[0, pl.ds(r, c), :]))
        _gla_group(chains, eye_ref[...])
        for fwd, st_ref in ((True, sf), (False, sb)):
            st = st_ref[...]
            for ch in chains:
                if ch["fwd"] != fwd:
                    continue
                o = ch["o"] + lax.dot_general(ch["qt"], st.astype(BF16), (((1,), (1,)), ((), ())),
                                              preferred_element_type=F32)
                o_ref[0, pl.ds(ch["r"], c), :] += o
                st = st * ch["dec"] + ch["upd"]
            st_ref[...] = st
        return carry

    lax.fori_loop(0, nc // un, body, 0)
    s_ref[0, 0, 0] = sf[...]
    s_ref[0, 0, 1] = sb[...]


def _gla_scan(pmain, pa, w2p, b2p, s0):
    b, t, _ = pmain.shape
    h = GLA_HEADS
    assert t % (GLA_CHUNK * GLA_UNROLL) == 0
    consts = _gla_consts(True) + _gla_consts(False) + (jnp.asarray(np.eye(GLA_HV), BF16),)
    kblk = GLA_DK // GLA_HK
    vblk = (2 * GLA_DK) // GLA_HV
    cspec = lambda a: pl.BlockSpec(a.shape, lambda i, j: (0,) * a.ndim)
    return pl.pallas_call(
        functools.partial(_gla_kernel, t=t),
        grid=(b, h),
        in_specs=[pl.BlockSpec((1, t, GLA_HK), lambda i, j: (i, 0, j)),
                  pl.BlockSpec((1, t, GLA_HK), lambda i, j: (i, 0, kblk + j)),
                  pl.BlockSpec((1, t, GLA_HV), lambda i, j: (i, 0, vblk + j)),
                  pl.BlockSpec((1, t, LANE), lambda i, j: (i, 0, 0)),
                  pl.BlockSpec((2, LANE, GLA_HK), lambda i, j: (0, 0, j)),
                  pl.BlockSpec((2, 1, GLA_HK), lambda i, j: (0, 0, j)),
                  pl.BlockSpec((1, 1, 2, GLA_HV, GLA_HK), lambda i, j: (i, j, 0, 0, 0))]
                 + [cspec(a) for a in consts],
        out_specs=[pl.BlockSpec((1, t, GLA_HV), lambda i, j: (i, 0, j)),
                   pl.BlockSpec((1, 1, 2, GLA_HV, GLA_HK), lambda i, j: (i, j, 0, 0, 0))],
        out_shape=[jax.ShapeDtypeStruct((b, t, GLA_DV), F32),
                   jax.ShapeDtypeStruct((b, h, 2, GLA_HV, GLA_HK), F32)],
        scratch_shapes=[pltpu.VMEM((t, GLA_HK), F32), pltpu.VMEM((t, GLA_HK), F32),
                        pltpu.VMEM((GLA_HV, GLA_HK), F32), pltpu.VMEM((GLA_HV, GLA_HK), F32)],
        compiler_params=_cparams(("arbitrary", "arbitrary")),
        name="gla_scan",
    )(pmain, pmain, pmain, pa, w2p, b2p, s0, *consts)


def _ssd_consts():
    c = SSD_CHUNK
    i = np.arange(c)
    tril = (i[:, None] >= i[None, :]).astype(np.float32)
    triu = (i[:, None] <= i[None, :]).astype(np.float32)
    eye = np.eye(SSD_STATE, dtype=np.float32)
    sel16 = np.eye(16, LANE, dtype=np.float32)
    pg = np.zeros((SSD_GROUPS, LANE, LANE), np.float32)
    for g in range(SSD_GROUPS):
        for q in range(2 * SSD_REP):
            z, r = divmod(q, SSD_REP)
            pg[g, z * SSD_HEADS + g * SSD_REP + r, q] = 1.0
    e4 = np.zeros((2, LANE, SSD_GW), np.float32)
    for z in range(2):
        for r in range(SSD_REP):
            e4[z, z * SSD_REP + r, r * SSD_HEADDIM:(r + 1) * SSD_HEADDIM] = 1.0
    lm = np.zeros((SSD_REP * c, SSD_GW), np.float32)
    for r in range(SSD_REP):
        lm[r * c:(r + 1) * c, r * SSD_HEADDIM:(r + 1) * SSD_HEADDIM] = 1.0
    return (jnp.asarray(tril, BF16), jnp.asarray(triu, BF16), jnp.asarray(tril), jnp.asarray(triu),
            jnp.asarray(eye, BF16), jnp.asarray(sel16, BF16), jnp.asarray(pg, BF16), jnp.asarray(e4, BF16),
            jnp.asarray(lm))


def _conv_block(in_ref, w, bias, i, nblk, rows, t, taps):
    r = pl.multiple_of(i * rows, rows)
    cur = in_ref[0, pl.ds(r, rows), :]
    rp = pl.multiple_of(jnp.maximum(r - SUBLANE, 0), SUBLANE)
    rn = pl.multiple_of(jnp.minimum(r + rows, t - SUBLANE), SUBLANE)
    prev = jnp.where(i > 0, in_ref[0, pl.ds(rp, SUBLANE), :], 0.0)
    nxt = jnp.where(i < nblk - 1, in_ref[0, pl.ds(rn, SUBLANE), :], 0.0)
    ext = jnp.concatenate([prev, cur, nxt], axis=0)
    half = taps // 2
    acc = bias
    for j in range(taps):
        off = SUBLANE - half + j
        acc = acc + w[j:j + 1] * ext[off:off + rows]
    return r, acc


def _ssd_group(chains, eye, sel16, lm):
    c = SSD_CHUNK
    for ch in chains:
        ch["cum"] = _dot_exact_l(ch["tri"], ch["das"])
        ch["dt_e"] = _dot_split2_r(ch["ds"], ch["e4z"])
        ch["ccb"], bcb = ch["cc"].astype(BF16), ch["bc"].astype(BF16)
        ch["cb"] = _dot_nt(ch["ccb"], bcb)
        ch["bc_t"] = _dot_nt(eye, bcb).astype(BF16)
    for ch in chains:
        ch["cum_t"] = _dot_exact_nt(sel16, ch["cum"])
        ch["cum_e"] = _dot_split2_r(ch["cum"], ch["e4z"])
    for ch in chains:
        cum, cum_t, cum_e = ch["cum"], ch["cum_t"], ch["cum_e"]
        tot_e = cum_e[c - 1:c] if ch["fwd"] else cum_e[0:1]
        xdt = ch["xg"] * ch["dt_e"]
        cb = ch["cb"] * ch["mask"]
        ms = []
        for r in range(SSD_REP):
            q = ch["z"] * SSD_REP + r
            seg = cum[:, q:q + 1] - cum_t[q:q + 1, :]
            ms.append((cb * jnp.exp2(jnp.minimum(seg, 0.0))).astype(BF16))
        ch["mcat"] = jnp.concatenate(ms, axis=1)
        ch["xbd"] = (jnp.concatenate([xdt] * SSD_REP, axis=0) * lm).astype(BF16)
        ch["w"] = (xdt * jnp.exp2(tot_e - cum_e)).astype(BF16)
        ch["dece"] = jnp.exp2(cum_e)
        ch["dec"] = jnp.exp2(tot_e)
    for ch in chains:
        ch["y"] = jnp.dot(ch["mcat"], ch["xbd"], preferred_element_type=F32)
        ch["upd"] = jnp.dot(ch["bc_t"], ch["w"], preferred_element_type=F32)


def _ssd_kernel(x_ref, bm_ref, cm_ref, dt_ref, wx_ref, wb_ref, wc_ref, bx_ref, bb_ref, bcb_ref,
                dtb_ref, alog_ref, dsk_ref, s0_ref,
                tl_ref, tu_ref, ml_ref, mu_ref, eye_ref, sel16_ref, pg_ref, e4_ref, lm_ref,
                y_ref, s_ref, xc, bcs, ccs, dsel, dasel, sf, sb, *, t):
    c = SSD_CHUNK
    un = SSD_UNROLL
    nc = t // c
    pb = c
    nblk = t // pb

    def prep(i, carry):
        for in_ref, w_ref, b_ref, out in ((x_ref, wx_ref, bx_ref, xc), (bm_ref, wb_ref, bb_ref, bcs),
                                          (cm_ref, wc_ref, bcb_ref, ccs)):
            r, acc = _conv_block(in_ref, w_ref[...], b_ref[...], i, nblk, pb, t, SSD_CONV)
            out[pl.ds(r, pb), :] = _silu(acc)
        r = pl.multiple_of(i * pb, pb)
        dtp = _softplus(dt_ref[0, pl.ds(r, pb), :] + dtb_ref[...])
        a2 = jnp.exp(alog_ref[...]) * (-LOG2E)
        dsel[pl.ds(r, pb), :] = _dot_exact_r(dtp, pg_ref[0])
        dasel[pl.ds(r, pb), :] = _dot_exact_r(dtp * a2, pg_ref[0])
        y_ref[0, pl.ds(r, pb), :] = xc[pl.ds(r, pb), :] * dsk_ref[...]
        return carry

    lax.fori_loop(0, nblk, prep, 0)
    sf[...] = s0_ref[0, 0, 0]
    sb[...] = s0_ref[0, 0, 1]

    def body(ci, carry):
        chains = []
        for fwd, z, tri_ref, mk_ref in ((True, 0, tl_ref, ml_ref), (False, 1, tu_ref, mu_ref)):
            tri, mask, e4z = tri_ref[...], mk_ref[...], e4_ref[z]
            for u in range(un):
                idx = ci * un + u
                r = pl.multiple_of((idx if fwd else nc - 1 - idx) * c, c)
                chains.append(dict(r=r, fwd=fwd, z=z, tri=tri, mask=mask, e4z=e4z,
                                   xg=xc[pl.ds(r, c), :], bc=bcs[pl.ds(r, c), :], cc=ccs[pl.ds(r, c), :],
                                   ds=dsel[pl.ds(r, c), :], das=dasel[pl.ds(r, c), :]))
        _ssd_group(chains, eye_ref[...], sel16_ref[...], lm_ref[...])
        for fwd, st_ref in ((True, sf), (False, sb)):
            st = st_ref[...]
            for ch in chains:
                if ch["fwd"] != fwd:
                    continue
                y = ch["y"] + jnp.dot(ch["ccb"], st.astype(BF16), preferred_element_type=F32) * ch["dece"]
                y_ref[0, pl.ds(ch["r"], c), :] += y
                st = st * ch["dec"] + ch["upd"]
            st_ref[...] = st
        return carry

    lax.fori_loop(0, nc // un, body, 0)
    s_ref[0, 0, 0] = sf[...]
    s_ref[0, 0, 1] = sb[...]


def _ssd_scan(pmain, pdt, conv_w, conv_b, dtb, alog, dskip, s0):
    b, t, _ = pmain.shape
    g = SSD_GROUPS
    assert t % (SSD_CHUNK * SSD_UNROLL) == 0
    tril, triu, mtril, mtriu, eye, sel16, pg, e4, lm = _ssd_consts()
    xblk = SSD_DI // SSD_GW
    bblk = (2 * SSD_DI) // SSD_STATE
    cblk = bblk + SSD_GN // SSD_STATE
    wbblk = SSD_DI // SSD_STATE
    wcblk = wbblk + SSD_GN // SSD_STATE
    cspec = lambda a: pl.BlockSpec(a.shape, lambda i, j: (0,) * a.ndim)
    return pl.pallas_call(
        functools.partial(_ssd_kernel, t=t),
        grid=(b, g),
        in_specs=[pl.BlockSpec((1, t, SSD_GW), lambda i, j: (i, 0, xblk + j)),
                  pl.BlockSpec((1, t, SSD_STATE), lambda i, j: (i, 0, bblk + j)),
                  pl.BlockSpec((1, t, SSD_STATE), lambda i, j: (i, 0, cblk + j)),
                  pl.BlockSpec((1, t, LANE), lambda i, j: (i, 0, 0)),
                  pl.BlockSpec((SSD_CONV, SSD_GW), lambda i, j: (0, j)),
                  pl.BlockSpec((SSD_CONV, SSD_STATE), lambda i, j: (0, wbblk + j)),
                  pl.BlockSpec((SSD_CONV, SSD_STATE), lambda i, j: (0, wcblk + j)),
                  pl.BlockSpec((1, SSD_GW), lambda i, j: (0, j)),
                  pl.BlockSpec((1, SSD_STATE), lambda i, j: (0, wbblk + j)),
                  pl.BlockSpec((1, SSD_STATE), lambda i, j: (0, wcblk + j)),
                  pl.BlockSpec((1, LANE), lambda i, j: (0, 0)),
                  pl.BlockSpec((1, LANE), lambda i, j: (0, 0)),
                  pl.BlockSpec((1, SSD_GW), lambda i, j: (0, j)),
                  pl.BlockSpec((1, 1, 2, SSD_STATE, SSD_GW), lambda i, j: (i, j, 0, 0, 0)),
                  cspec(tril), cspec(triu), cspec(mtril), cspec(mtriu), cspec(eye), cspec(sel16),
                  pl.BlockSpec((1, LANE, LANE), lambda i, j: (j, 0, 0)),
                  cspec(e4), cspec(lm)],
        out_specs=[pl.BlockSpec((1, t, SSD_GW), lambda i, j: (i, 0, j)),
                   pl.BlockSpec((1, 1, 2, SSD_STATE, SSD_GW), lambda i, j: (i, j, 0, 0, 0))],
        out_shape=[jax.ShapeDtypeStruct((b, t, SSD_DI), F32),
                   jax.ShapeDtypeStruct((b, g, 2, SSD_STATE, SSD_GW), F32)],
        scratch_shapes=[pltpu.VMEM((t, SSD_GW), F32), pltpu.VMEM((t, SSD_STATE), F32),
                        pltpu.VMEM((t, SSD_STATE), F32), pltpu.VMEM((t, LANE), F32),
                        pltpu.VMEM((t, LANE), F32),
                        pltpu.VMEM((SSD_STATE, SSD_GW), F32), pltpu.VMEM((SSD_STATE, SSD_GW), F32)],
        compiler_params=_cparams(("arbitrary", "arbitrary")),
        name="ssd_scan",
    )(pmain, pmain, pmain, pdt, conv_w, conv_w, conv_w, conv_b, conv_b, conv_b,
      dtb, alog, dskip, s0, tril, triu, mtril, mtriu, eye, sel16, pg, e4, lm)


def _conv3_kernel(p_ref, w_ref, b_ref, *out_refs, t):
    pb = min(t, 512)
    nblk = t // pb

    def blk(i, carry):
        r, acc = _conv_block(p_ref, w_ref[...], b_ref[...], i, nblk, pb, t, HY_SHORT)
        out_refs[0][0, pl.ds(r, pb), :] = acc
        if len(out_refs) > 1:
            out_refs[1][0, pl.ds(r, pb), :] = acc.astype(BF16)
        return carry

    lax.fori_loop(0, nblk, blk, 0)


def _conv3(p, w, bias, col0, ncols, with_bf16):
    b, t, _ = p.shape
    tn = 256
    off = col0 // tn
    out_shape = [jax.ShapeDtypeStruct((b, t, ncols), F32)]
    out_specs = [pl.BlockSpec((1, t, tn), lambda i, j: (i, 0, j))]
    if with_bf16:
        out_shape.append(jax.ShapeDtypeStruct((b, t, ncols), BF16))
        out_specs.append(pl.BlockSpec((1, t, tn), lambda i, j: (i, 0, j)))
    return pl.pallas_call(
        functools.partial(_conv3_kernel, t=t),
        grid=(b, ncols // tn),
        in_specs=[pl.BlockSpec((1, t, tn), lambda i, j: (i, 0, off + j)),
                  pl.BlockSpec((HY_SHORT, tn), lambda i, j: (0, off + j)),
                  pl.BlockSpec((1, tn), lambda i, j: (0, off + j))],
        out_specs=out_specs,
        out_shape=out_shape,
        compiler_params=_cparams(("arbitrary", "arbitrary")),
        name="hy_conv3",
    )(p, w, bias)


def _hy_pos_emb(l):
    bands = (HY_EMB - 1) // 2
    t = np.linspace(0.0, 1.0, l)[:, None]
    w = 2 * math.pi * np.arange(l, dtype=np.float64)[:, None] / l
    ang = np.linspace(1e-4, bands - 1, bands)[None, :] * w
    z = np.concatenate([t, np.cos(ang), -np.sin(ang)], axis=-1)
    zp = np.zeros((l, LANE), np.float32)
    zp[:, :HY_EMB] = z
    return zp


def _filt_kernel(z_ref, w1_ref, b1_ref, w2_ref, b2_ref, w3_ref, b3_ref, w4_ref, fr_ref, dl_ref, o_ref, *, l, tl):
    fr = fr_ref[...]
    h = jnp.sin(fr * (_dot_f32(z_ref[...], w1_ref[...]) + b1_ref[...]))
    h = jnp.sin(fr * (_dot_f32(h, w2_ref[...]) + b2_ref[...]))
    h = jnp.sin(fr * (_dot_f32(h, w3_ref[...]) + b3_ref[...]))
    hh = _dot_f32(h, w4_ref[...])
    row = lax.broadcasted_iota(jnp.int32, (tl, 1), 0) + pl.program_id(0) * tl
    tt = row.astype(F32) * (1.0 / (l - 1))
    hh = hh * jnp.exp(-tt * dl_ref[...])
    drop = jnp.logical_and(row == 0, (pl.program_id(1) % 2) == 1)
    o_ref[...] = jnp.where(drop, 0.0, hh)


def _hy_filters(l, w1, b1, w2, b2, w3, b3, w4, freq):
    d = D_MODEL
    tl = min(l, 512)
    pad2 = lambda a: jnp.zeros((LANE, LANE), F32).at[:a.shape[0], :a.shape[1]].set(a)
    pad1 = lambda a: jnp.zeros((1, LANE), F32).at[0, :a.shape[0]].set(a)
    w4p = jnp.zeros((LANE, HY_ORDER * 2 * d), F32).at[:HY_FW].set(w4)
    deltas = np.abs(np.linspace(math.log(HY_DECAY_TARGET) / HY_FAST_DECAY,
                                math.log(HY_DECAY_TARGET) / HY_SLOW_DECAY, d))
    dl = jnp.asarray(np.tile(deltas, HY_ORDER * 2)[None, :], F32)
    full = lambda a: pl.BlockSpec(a.shape, lambda i, j: (0, 0))
    small = [pad2(w1), pad1(b1), pad2(w2), pad1(b2), pad2(w3), pad1(b3)]
    return pl.pallas_call(
        functools.partial(_filt_kernel, l=l, tl=tl),
        grid=(l // tl, HY_ORDER * 2),
        in_specs=[pl.BlockSpec((tl, LANE), lambda i, j: (i, 0))] + [full(a) for a in small]
                 + [pl.BlockSpec((LANE, d), lambda i, j: (0, j)), pl.BlockSpec((1, LANE), lambda i, j: (0, 0)),
                    pl.BlockSpec((1, d), lambda i, j: (0, j))],
        out_specs=pl.BlockSpec((tl, d), lambda i, j: (i, j)),
        out_shape=jax.ShapeDtypeStruct((l, HY_ORDER * 2 * d), F32),
        compiler_params=_cparams(("arbitrary", "arbitrary")),
        name="hy_filt",
    )(jnp.asarray(_hy_pos_emb(l)), *small, w4p, pad1(freq), dl)


def _dft_tables_np(l):
    n = 2 * l
    k = np.arange(l)
    ang = 2 * math.pi * ((k[:, None] * k[None, :]) % n) / n
    cm = np.cos(ang)
    sm = -np.sin(ang)
    alt = np.where(k % 2 == 0, 1.0, -1.0)
    s_fwd = sm.copy()
    s_fwd[0, :] = alt
    s_inv = sm.copy()
    s_inv[:, 0] = alt
    return cm, s_fwd, s_inv


def _dft_tab_kernel(ca_ref, sa_ref, cb_ref, sb_ref, c_ref, sf_ref, si_ref, *, rb):
    a = pl.program_id(0)
    ca, sa = ca_ref[0], sa_ref[0]
    cb, sb = cb_ref[...], sb_ref[...]
    cm = ca * cb - sa * sb
    sm = -(sa * cb + ca * sb)
    l = cm.shape[1]
    row = lax.broadcasted_iota(jnp.int32, (rb, l), 0)
    col = lax.broadcasted_iota(jnp.int32, (rb, l), 1)
    alt_col = jnp.where(col % 2 == 0, 1.0, -1.0)
    alt_row = jnp.where(row % 2 == 0, 1.0, -1.0)
    c_ref[...] = cm.astype(BF16)
    sf_ref[...] = jnp.where(jnp.logical_and(row == 0, a == 0), alt_col, sm).astype(BF16)
    si_ref[...] = jnp.where(col == 0, alt_row, sm).astype(BF16)


def _dft_tables(l):
    if l <= 512:
        return tuple(jnp.asarray(m, BF16) for m in _dft_tables_np(l))
    rb = 64
    na = l // rb
    n = 2 * l
    nn = np.arange(l)
    aa = np.arange(na)
    bb = np.arange(rb)
    ang_a = 2 * math.pi * ((aa[:, None] * rb * nn[None, :]) % n) / n
    ang_b = 2 * math.pi * ((bb[:, None] * nn[None, :]) % n) / n
    ca = jnp.asarray(np.cos(ang_a)[:, None, :], F32)
    sa = jnp.asarray(np.sin(ang_a)[:, None, :], F32)
    cb = jnp.asarray(np.cos(ang_b), F32)
    sb = jnp.asarray(np.sin(ang_b), F32)
    rowspec = pl.BlockSpec((1, 1, l), lambda a: (a, 0, 0))
    tabspec = pl.BlockSpec((rb, l), lambda a: (0, 0))
    outspec = pl.BlockSpec((rb, l), lambda a: (a, 0))
    return tuple(pl.pallas_call(
        functools.partial(_dft_tab_kernel, rb=rb),
        grid=(na,),
        in_specs=[rowspec, rowspec, tabspec, tabspec],
        out_specs=[outspec] * 3,
        out_shape=[jax.ShapeDtypeStruct((l, l), BF16)] * 3,
        compiler_params=_cparams(("arbitrary",)),
        name="dft_tables",
    )(ca, sa, cb, sb))


def _spec_filt_kernel(c_ref, s_ref, a_ref, b_ref, h_ref, accr, acci, accn, *, l):
    kk = pl.program_id(2)

    @pl.when(kk == 0)
    def _():
        accr[...] = jnp.zeros_like(accr)
        acci[...] = jnp.zeros_like(acci)
        accn[...] = jnp.zeros_like(accn)

    a, bw = a_ref[...], b_ref[...]
    sm = (a + bw).astype(BF16)
    df = (a - bw).astype(BF16)
    accr[...] += jnp.dot(c_ref[...], sm, preferred_element_type=F32)
    acci[...] += jnp.dot(s_ref[...], df, preferred_element_type=F32)
    accn[...] += jnp.dot(s_ref[...], sm, preferred_element_type=F32)

    @pl.when(kk == pl.num_programs(2) - 1)
    def _():
        tm = accr.shape[0]
        row0 = (lax.broadcasted_iota(jnp.int32, (tm, 1), 0) + pl.program_id(0) * tm) == 0
        scale = jnp.where(row0, 0.5 / l, 1.0 / l)
        h_ref[0] = accr[...] * scale
        h_ref[1] = jnp.where(row0, accn[...], acci[...]) * scale


def _spec_filt(cm, s_fwd, hh, l):
    d = D_MODEL
    tm, tn, tk = min(l, 512), 512, min(l, 1024)
    nd = d // tn
    return pl.pallas_call(
        functools.partial(_spec_filt_kernel, l=l),
        grid=(l // tm, HY_ORDER * nd, l // tk),
        in_specs=[pl.BlockSpec((tm, tk), lambda m, j, k: (m, k)),
                  pl.BlockSpec((tm, tk), lambda m, j, k: (m, k)),
                  pl.BlockSpec((tk, tn), lambda m, j, k: (k, (j // nd) * 2 * nd + j % nd)),
                  pl.BlockSpec((tk, tn), lambda m, j, k: (k, (j // nd) * 2 * nd + nd + j % nd))],
        out_specs=pl.BlockSpec((2, tm, tn), lambda m, j, k: (0, m, j)),
        out_shape=jax.ShapeDtypeStruct((2, l, HY_ORDER * d), F32),
        scratch_shapes=[pltpu.VMEM((tm, tn), F32)] * 3,
        compiler_params=_cparams(("arbitrary", "arbitrary", "arbitrary")),
        name="hy_spec_filt",
    )(cm, s_fwd, hh, hh)


def _spec_sig_kernel(c_ref, s_ref, u_ref, h_ref, y_ref, accr, acci):
    kk = pl.program_id(2)

    @pl.when(kk == 0)
    def _():
        accr[...] = jnp.zeros_like(accr)
        acci[...] = jnp.zeros_like(acci)

    u = u_ref[0]
    accr[...] += jnp.dot(c_ref[...], u, preferred_element_type=F32)
    acci[...] += jnp.dot(s_ref[...], u, preferred_element_type=F32)

    @pl.when(kk == pl.num_programs(2) - 1)
    def _():
        tm = accr.shape[0]
        row0 = (lax.broadcasted_iota(jnp.int32, (tm, 1), 0) + pl.program_id(0) * tm) == 0
        xr, xi = accr[...], acci[...]
        hr, hi = h_ref[0], h_ref[1]
        y_ref[0] = (xr * hr - jnp.where(row0, 0.0, xi * hi)).astype(BF16)
        y_ref[1] = jnp.where(row0, xi * hi, xr * hi + xi * hr).astype(BF16)


def _spec_sig(cm, s_fwd, ub, hspec, order, l):
    b, _, d = ub.shape
    tm, tn, tk = min(l, 1024), 512, min(l, 1024)
    nd = d // tn
    return pl.pallas_call(
        _spec_sig_kernel,
        grid=(l // tm, b * nd, l // tk),
        in_specs=[pl.BlockSpec((tm, tk), lambda m, j, k: (m, k)),
                  pl.BlockSpec((tm, tk), lambda m, j, k: (m, k)),
                  pl.BlockSpec((1, tk, tn), lambda m, j, k: (j // nd, k, j % nd)),
                  pl.BlockSpec((2, tm, tn), lambda m, j, k: (0, m, order * nd + j % nd))],
        out_specs=pl.BlockSpec((2, tm, tn), lambda m, j, k: (0, m, j)),
        out_shape=jax.ShapeDtypeStruct((2, l, b * d), BF16),
        scratch_shapes=[pltpu.VMEM((tm, tn), F32)] * 2,
        compiler_params=_cparams(("arbitrary", "arbitrary", "arbitrary")),
        name="hy_spec_sig",
    )(cm, s_fwd, ub, hspec)


def _inv_kernel(c_ref, s_ref, y_ref, u_ref, g_ref, bias_ref, *refs):
    out_refs, acc = refs[:-1], refs[-1]
    kk = pl.program_id(2)

    @pl.when(kk == 0)
    def _():
        acc[...] = jnp.zeros_like(acc)

    acc[...] += (jnp.dot(c_ref[...], y_ref[0], preferred_element_type=F32)
                 + jnp.dot(s_ref[...], y_ref[1], preferred_element_type=F32))

    @pl.when(kk == pl.num_programs(2) - 1)
    def _():
        res = g_ref[0] * (acc[...] + u_ref[0] * bias_ref[...])
        out_refs[0][0] = res
        if len(out_refs) > 1:
            out_refs[1][0] = res.astype(BF16)


def _spec_inv(cm, s_inv, yspec, u, ucol, gate, gcol, bias, with_bf16, l):
    b = u.shape[0]
    d = D_MODEL
    tt, tn, tk = min(l, 1024), 512, min(l, 1024)
    nd = d // tn
    uo, go = ucol // tn, gcol // tn
    out_shape = [jax.ShapeDtypeStruct((b, l, d), F32)]
    out_specs = [pl.BlockSpec((1, tt, tn), lambda t, j, k: (j // nd, t, j % nd))]
    if with_bf16:
        out_shape.append(jax.ShapeDtypeStruct((b, l, d), BF16))
        out_specs.append(pl.BlockSpec((1, tt, tn), lambda t, j, k: (j // nd, t, j % nd)))
    return pl.pallas_call(
        _inv_kernel,
        grid=(l // tt, b * nd, l // tk),
        in_specs=[pl.BlockSpec((tt, tk), lambda t, j, k: (t, k)),
                  pl.BlockSpec((tt, tk), lambda t, j, k: (t, k)),
                  pl.BlockSpec((2, tk, tn), lambda t, j, k: (0, k, j)),
                  pl.BlockSpec((1, tt, tn), lambda t, j, k: (j // nd, t, uo + j % nd)),
                  pl.BlockSpec((1, tt, tn), lambda t, j, k: (j // nd, t, go + j % nd)),
                  pl.BlockSpec((1, tn), lambda t, j, k: (0, j % nd))],
        out_specs=out_specs,
        out_shape=out_shape,
        scratch_shapes=[pltpu.VMEM((tt, tn), F32)],
        compiler_params=_cparams(("arbitrary", "arbitrary", "arbitrary")),
        name="hy_spec_inv",
    )(cm, s_inv, yspec, u, gate, bias)


def _hyena_run(x, mods, w_in, conv_w, conv_b, fw, h_bias):
    b, l, d = x.shape
    p = _proj(x, mods, w_in, 512)
    v, vb = _conv3(p, conv_w, conv_b, 0, d, True)
    x12 = _conv3(p, conv_w, conv_b, d, 2 * d, False)[0]
    cm, s_fwd, s_inv = _dft_tables(l)
    hh = _hy_filters(l, *fw)
    hspec = _spec_filt(cm, s_fwd, hh, l)
    y1 = _spec_sig(cm, s_fwd, vb, hspec, 0, l)
    z, zb = _spec_inv(cm, s_inv, y1, v, 0, x12, 0, h_bias[0:1], True, l)
    y2 = _spec_sig(cm, s_fwd, zb, hspec, 1, l)
    return _spec_inv(cm, s_inv, y2, z, 0, x12, d, h_bias[1:2], False, l)[0]


def _snake_kernel(x_ref, j_ref, o_ref):
    jm = j_ref[...]
    for g in range(x_ref.shape[1] // (2 * GRID_W)):
        r0 = g * 2 * GRID_W
        o_ref[0, r0:r0 + GRID_W, :] = x_ref[0, r0:r0 + GRID_W, :]
        o_ref[0, r0 + GRID_W:r0 + 2 * GRID_W, :] = _dot_exact_l(jm, x_ref[0, r0 + GRID_W:r0 + 2 * GRID_W, :])


def _snake(h):
    b, l, ch = h.shape
    tm = 512
    jm = jnp.asarray(np.eye(GRID_W)[::-1].copy(), BF16)
    return pl.pallas_call(
        _snake_kernel,
        grid=(b, l // tm),
        in_specs=[pl.BlockSpec((1, tm, ch), lambda i, j: (i, j, 0)),
                  pl.BlockSpec((GRID_W, GRID_W), lambda i, j: (0, 0))],
        out_specs=pl.BlockSpec((1, tm, ch), lambda i, j: (i, j, 0)),
        out_shape=jax.ShapeDtypeStruct((b, l, ch), F32),
        compiler_params=_cparams(("arbitrary", "arbitrary")),
        name="snake",
    )(h, jm)


def _pad_cols(w, n):
    return jnp.zeros((w.shape[0], n), w.dtype).at[:, :w.shape[1]].set(w)


def kernel(x, c, ctx, c_ctx, ada_w, ada_b, ln_g, ln_b, ffn_w1, ffn_w2, gla_w_in, gla_w_a2, gla_b_a2, gla_norm, gla_w_out, ssd_w_in, ssd_conv_w, ssd_conv_b, ssd_dt_bias, ssd_a_log, ssd_d, ssd_norm, ssd_w_out, hy_w_in, hy_conv_w, hy_conv_b, hy_f_w1, hy_f_b1, hy_f_w2, hy_f_b2, hy_f_w3, hy_f_b3, hy_f_w4, hy_f_freq, hy_bias, hy_w_out):
    bsz, _, d = x.shape
    hl = _snake(x)
    hc = ctx
    cvec = jnp.zeros((16, d), F32).at[:bsz].set(c).at[bsz].set(c_ctx)
    mods = _ada(cvec, ada_w, ada_b).reshape(DEPTH, 16, 6, d)
    mods = jnp.concatenate([mods, jnp.zeros((DEPTH, 16, 2, d), F32)], axis=2)

    for i in range(DEPTH):
        kind, j = i % N_MIXERS, i // N_MIXERS
        need_ctx = i < DEPTH - 1
        ml = mods[i, :bsz]
        mc = jnp.broadcast_to(mods[i, bsz][None], (bsz, 8, d))
        g0, b0 = ln_g[i, 0][None], ln_b[i, 0][None]
        g1, b1 = ln_g[i, 1][None], ln_b[i, 1][None]
        w1 = ffn_w1[i].astype(BF16)
        w2 = ffn_w2[i].astype(BF16)
        streams = [(hc, mc, True), (hl, ml, False)]
        if kind == 0:
            w_main = gla_w_in[j][:, :GLA_MAIN].astype(BF16)
            w_a = _pad_cols(gla_w_in[j][:, GLA_MAIN:], LANE).astype(BF16)
            w2p = jnp.zeros((2, LANE, GLA_DK), F32)
            for z in range(2):
                w2p = w2p.at[z, z * GLA_RANK:(z + 1) * GLA_RANK].set(gla_w_a2[j, z])
            b2p = gla_b_a2[j][:, None, :]
            w_out = gla_w_out[j].astype(BF16)
            ng = gla_norm[j][None]
            state = jnp.zeros((bsz, GLA_HEADS, 2, GLA_HV, GLA_HK), F32)
            new = []
            for h, m, is_ctx in streams:
                pmain = _proj(h, m, w_main, 512)
                pa = _proj(h, m, w_a, LANE)
                o, st = _gla_scan(pmain, pa, w2p, b2p, state)
                if is_ctx:
                    state = st
                if is_ctx and not need_ctx:
                    new.append(h)
                    continue
                new.append(_gla_out(o, pmain, h, m, ng, w_out, g0, b0))
            hc, hl = new
        elif kind == 1:
            w_main = ssd_w_in[j][:, :SSD_MAIN].astype(BF16)
            w_dt = _pad_cols(ssd_w_in[j][:, SSD_MAIN:], LANE).astype(BF16)
            dtb = _pad_cols(ssd_dt_bias[j].reshape(1, -1), LANE)
            alog = _pad_cols(ssd_a_log[j].reshape(1, -1), LANE)
            dskip = jnp.repeat(ssd_d[j], SSD_HEADDIM)[None]
            cw = ssd_conv_w[j]
            cbias = ssd_conv_b[j][None]
            w_out = ssd_w_out[j].astype(BF16)
            ng = ssd_norm[j][None]
            state = jnp.zeros((bsz, SSD_GROUPS, 2, SSD_STATE, SSD_GW), F32)
            new = []
            for h, m, is_ctx in streams:
                pmain = _proj(h, m, w_main, 512)
                pdt = _proj(h, m, w_dt, LANE)
                y, st = _ssd_scan(pmain, pdt, cw, cbias, dtb, alog, dskip, state)
                if is_ctx:
                    state = st
                if is_ctx and not need_ctx:
                    new.append(h)
                    continue
                new.append(_ssd_out(y, pmain, h, m, ng, w_out, g0, b0))
            hc, hl = new
        else:
            w_in = hy_w_in[j].astype(BF16)
            w_out = hy_w_out[j].astype(BF16)
            fw = (hy_f_w1[j], hy_f_b1[j], hy_f_w2[j], hy_f_b2[j], hy_f_w3[j], hy_f_b3[j], hy_f_w4[j], hy_f_freq[j])
            new = []
            for h, m, is_ctx in streams:
                if is_ctx and not need_ctx:
                    new.append(h)
                    continue
                zz = _hyena_run(h, m, w_in, hy_conv_w[j], hy_conv_b[j][None], fw, hy_bias[j])
                new.append(_hy_out(zz, h, m, w_out, g0, b0))
            hc, hl = new
        hl = _ffn(hl, ml, w1, w2, g1, b1)
        if need_ctx:
            hc = _ffn(hc, mc, w1, w2, g1, b1)
    return _snake(hl)
```

```python
import functools
import math

import numpy as np
import jax
import jax.numpy as jnp
from jax import lax
from jax.experimental import pallas as pl
from jax.experimental.pallas import tpu as pltpu

F32 = jnp.float32
BF16 = jnp.bfloat16

D_MODEL = 1024
DEPTH = 4
GRID_W = 64
N_MIXERS = 3
D_FF = 4 * D_MODEL
DEEPNORM_ALPHA = (2 * DEPTH) ** 0.25
LN_EPS = 1e-5
RMS_EPS = 1e-6

GLA_HEADS = 4
GLA_DK = D_MODEL // 2
GLA_DV = D_MODEL
GLA_HK = GLA_DK // GLA_HEADS
GLA_HV = GLA_DV // GLA_HEADS
GLA_RANK = 16
GLA_GATE_NORM = 16.0
GLA_CHUNK = 64
GLA_UNROLL = 4
GLA_NEG = -1e30
LOG2E = math.log2(math.e)
GLA_MAIN = 2 * GLA_DK + 2 * GLA_DV

SSD_DI = 2 * D_MODEL
SSD_HEADDIM = 64
SSD_HEADS = SSD_DI // SSD_HEADDIM
SSD_GROUPS = 8
SSD_REP = SSD_HEADS // SSD_GROUPS
SSD_STATE = 128
SSD_CONV = 5
SSD_CHUNK = 128
SSD_UNROLL = 2
SSD_GN = SSD_GROUPS * SSD_STATE
SSD_CONV_DIM = SSD_DI + 2 * SSD_GN
SSD_MAIN = SSD_DI + SSD_CONV_DIM
SSD_GW = SSD_REP * SSD_HEADDIM

HY_ORDER = 2
HY_SHORT = 3
HY_EMB = 33
HY_FW = 64
HY_DECAY_TARGET = 1e-2
HY_FAST_DECAY = 0.3
HY_SLOW_DECAY = 1.5

LANE = 128
SUBLANE = 8
VMEM_LIMIT = 56 * 1024 * 1024


def _cparams(sem):
    return pltpu.CompilerParams(dimension_semantics=sem, vmem_limit_bytes=VMEM_LIMIT)


def _dot(a, b):
    return jnp.dot(a.astype(BF16), b.astype(BF16), preferred_element_type=F32)


def _dot_nt(a, b):
    return lax.dot_general(a.astype(BF16), b.astype(BF16), (((1,), (1,)), ((), ())),
                           preferred_element_type=F32)


def _split3(x):
    hi = x.astype(BF16)
    r1 = x - hi.astype(F32)
    mid = r1.astype(BF16)
    lo = (r1 - mid.astype(F32)).astype(BF16)
    return hi, mid, lo


def _dot_exact_l(m01, x):
    hi, mid, lo = _split3(x)
    d = lambda p: jnp.dot(m01, p, preferred_element_type=F32)
    return d(hi) + d(mid) + d(lo)


def _dot_exact_r(x, m01):
    hi, mid, lo = _split3(x)
    d = lambda p: jnp.dot(p, m01, preferred_element_type=F32)
    return d(hi) + d(mid) + d(lo)


def _dot_split2_r(x, m01):
    hi = x.astype(BF16)
    mid = (x - hi.astype(F32)).astype(BF16)
    d = lambda p: jnp.dot(p, m01, preferred_element_type=F32)
    return d(hi) + d(mid)


def _dot_exact_nt(m01, x):
    hi, mid, lo = _split3(x)
    d = lambda p: lax.dot_general(m01, p, (((1,), (1,)), ((), ())), preferred_element_type=F32)
    return d(hi) + d(mid) + d(lo)


def _dot_f32(a, b):
    ah, am, al = _split3(a)
    bh, bm, bl = _split3(b)
    d = lambda p, q: jnp.dot(p, q, preferred_element_type=F32)
    return (d(ah, bh) + (d(ah, bm) + d(am, bh)) + (d(ah, bl) + d(al, bh) + d(am, bm)))


def _dot_f32x3(a, b):
    ah, am, _ = _split3(a)
    bh, bm, _ = _split3(b)
    d = lambda p, q: jnp.dot(p, q, preferred_element_type=F32)
    return d(ah, bh) + (d(ah, bm) + d(am, bh))


def _silu(x):
    return x * jax.nn.sigmoid(x)


def _softplus(x):
    return jnp.maximum(x, 0.0) + jnp.log1p(jnp.exp(-jnp.abs(x)))


def _log_sigmoid(x):
    return -_softplus(-x)


def _layer_norm(h, g, b):
    mu = jnp.mean(h, -1, keepdims=True)
    d = h - mu
    var = jnp.mean(d * d, -1, keepdims=True)
    return d * lax.rsqrt(var + LN_EPS) * g + b


def _res_ln(x, gate, y, g, b):
    return _layer_norm(DEEPNORM_ALPHA * x + gate * y, g, b)


def _ada_kernel(c_ref, w_ref, b_ref, o_ref):
    s = _silu(c_ref[...])
    o_ref[0] = _dot_f32(s, w_ref[0]) + b_ref[0]


def _ada(cvec, ada_w, ada_b):
    tn = 1536
    n = 6 * D_MODEL
    return pl.pallas_call(
        _ada_kernel,
        grid=(DEPTH, n // tn),
        in_specs=[pl.BlockSpec((16, D_MODEL), lambda i, j: (0, 0)),
                  pl.BlockSpec((1, D_MODEL, tn), lambda i, j: (i, 0, j)),
                  pl.BlockSpec((1, 1, tn), lambda i, j: (i, 0, j))],
        out_specs=pl.BlockSpec((1, 16, tn), lambda i, j: (i, 0, j)),
        out_shape=jax.ShapeDtypeStruct((DEPTH, 16, n), F32),
        compiler_params=_cparams(("arbitrary", "arbitrary")),
        name="ada",
    )(cvec, ada_w, ada_b.reshape(DEPTH, 1, n))


def _proj_kernel(x_ref, m_ref, w_ref, o_ref, u_scr):
    @pl.when(pl.program_id(2) == 0)
    def _():
        m = m_ref[0]
        u_scr[...] = (x_ref[0] * (1.0 + m[1:2]) + m[0:1]).astype(BF16)

    o_ref[0] = jnp.dot(u_scr[...], w_ref[...], preferred_element_type=F32).astype(o_ref.dtype)


def _proj(x, mods, w, tn, out_dtype):
    b, t, d = x.shape
    n = w.shape[1]
    tm = min(t, 1024)
    return pl.pallas_call(
        _proj_kernel,
        grid=(b, t // tm, n // tn),
        in_specs=[pl.BlockSpec((1, tm, d), lambda i, j, k: (i, j, 0)),
                  pl.BlockSpec((1, 8, d), lambda i, j, k: (i, 0, 0)),
                  pl.BlockSpec((d, tn), lambda i, j, k: (0, k))],
        out_specs=pl.BlockSpec((1, tm, tn), lambda i, j, k: (i, j, k)),
        out_shape=jax.ShapeDtypeStruct((b, t, n), out_dtype),
        scratch_shapes=[pltpu.VMEM((tm, d), BF16)],
        compiler_params=_cparams(("arbitrary", "arbitrary", "arbitrary")),
        name="proj",
    )(x, mods, w)


def _ffn_kernel(x_ref, m_ref, w1_ref, w2_ref, g_ref, b_ref, o_ref, acc_ref, u_scr):
    f = pl.program_id(2)

    @pl.when(f == 0)
    def _():
        m = m_ref[0]
        acc_ref[...] = jnp.zeros_like(acc_ref)
        u_scr[...] = (x_ref[0] * (1.0 + m[4:5]) + m[3:4]).astype(BF16)

    a = jnp.square(jnp.maximum(jnp.dot(u_scr[...], w1_ref[...], preferred_element_type=F32), 0.0))
    acc_ref[...] += _dot(a, w2_ref[...])

    @pl.when(f == pl.num_programs(2) - 1)
    def _():
        o_ref[0] = _res_ln(x_ref[0], m_ref[0][5:6], acc_ref[...], g_ref[...], b_ref[...])


def _ffn(x, mods, w1, w2, g, bb):
    b, t, d = x.shape
    tm = min(t, 1024)
    tf = 512
    return pl.pallas_call(
        _ffn_kernel,
        grid=(b, t // tm, D_FF // tf),
        in_specs=[pl.BlockSpec((1, tm, d), lambda i, j, k: (i, j, 0)),
                  pl.BlockSpec((1, 8, d), lambda i, j, k: (i, 0, 0)),
                  pl.BlockSpec((d, tf), lambda i, j, k: (0, k)),
                  pl.BlockSpec((tf, d), lambda i, j, k: (k, 0)),
                  pl.BlockSpec((1, d), lambda i, j, k: (0, 0)),
                  pl.BlockSpec((1, d), lambda i, j, k: (0, 0))],
        out_specs=pl.BlockSpec((1, tm, d), lambda i, j, k: (i, j, 0)),
        out_shape=jax.ShapeDtypeStruct((b, t, d), F32),
        scratch_shapes=[pltpu.VMEM((tm, d), F32), pltpu.VMEM((tm, d), BF16)],
        compiler_params=_cparams(("arbitrary", "arbitrary", "arbitrary")),
        name="ffn",
    )(x, mods, w1, w2, g, bb)


def _gla_out_kernel(o_ref, gate_ref, x_ref, m_ref, ng_ref, w_ref, g_ref, b_ref, out_ref):
    o = o_ref[0]
    ng = ng_ref[...]
    parts = []
    for h in range(GLA_HEADS):
        oh = o[:, h * GLA_HV:(h + 1) * GLA_HV]
        r = lax.rsqrt(jnp.mean(oh * oh, -1, keepdims=True) + RMS_EPS)
        parts.append(oh * r * ng)
    z = jnp.concatenate(parts, axis=-1) * _silu(gate_ref[0].astype(F32))
    y = _dot(z, w_ref[...])
    out_ref[0] = _res_ln(x_ref[0], m_ref[0][2:3], y, g_ref[...], b_ref[...])


def _gla_out(o, pmain, x, mods, ng, w, g, bb):
    b, t, d = x.shape
    tm = min(t, 512)
    gate_blk = (2 * GLA_DK) // GLA_DV + 1
    return pl.pallas_call(
        _gla_out_kernel,
        grid=(b, t // tm),
        in_specs=[pl.BlockSpec((1, tm, GLA_DV), lambda i, j: (i, j, 0)),
                  pl.BlockSpec((1, tm, GLA_DV), lambda i, j: (i, j, gate_blk)),
                  pl.BlockSpec((1, tm, d), lambda i, j: (i, j, 0)),
                  pl.BlockSpec((1, 8, d), lambda i, j: (i, 0, 0)),
                  pl.BlockSpec((1, GLA_HV), lambda i, j: (0, 0)),
                  pl.BlockSpec((GLA_DV, d), lambda i, j: (0, 0)),
                  pl.BlockSpec((1, d), lambda i, j: (0, 0)),
                  pl.BlockSpec((1, d), lambda i, j: (0, 0))],
        out_specs=pl.BlockSpec((1, tm, d), lambda i, j: (i, j, 0)),
        out_shape=jax.ShapeDtypeStruct((b, t, d), F32),
        compiler_params=_cparams(("arbitrary", "arbitrary")),
        name="gla_out",
    )(o, pmain, x, mods, ng, w, g, bb)


def _ssd_out_kernel(y_ref, z_ref, x_ref, m_ref, ng_ref, w_ref, g_ref, b_ref, out_ref):
    yz = y_ref[0] * _silu(z_ref[0].astype(F32))
    r = lax.rsqrt(jnp.mean(yz * yz, -1, keepdims=True) + RMS_EPS)
    y = _dot(yz * r * ng_ref[...], w_ref[...])
    out_ref[0] = _res_ln(x_ref[0], m_ref[0][2:3], y, g_ref[...], b_ref[...])


def _ssd_out(y, pmain, x, mods, ng, w, g, bb):
    b, t, d = x.shape
    tm = min(t, 512)
    return pl.pallas_call(
        _ssd_out_kernel,
        grid=(b, t // tm),
        in_specs=[pl.BlockSpec((1, tm, SSD_DI), lambda i, j: (i, j, 0)),
                  pl.BlockSpec((1, tm, SSD_DI), lambda i, j: (i, j, 0)),
                  pl.BlockSpec((1, tm, d), lambda i, j: (i, j, 0)),
                  pl.BlockSpec((1, 8, d), lambda i, j: (i, 0, 0)),
                  pl.BlockSpec((1, SSD_DI), lambda i, j: (0, 0)),
                  pl.BlockSpec((SSD_DI, d), lambda i, j: (0, 0)),
                  pl.BlockSpec((1, d), lambda i, j: (0, 0)),
                  pl.BlockSpec((1, d), lambda i, j: (0, 0))],
        out_specs=pl.BlockSpec((1, tm, d), lambda i, j: (i, j, 0)),
        out_shape=jax.ShapeDtypeStruct((b, t, d), F32),
        compiler_params=_cparams(("arbitrary", "arbitrary")),
        name="ssd_out",
    )(y, pmain, x, mods, ng, w, g, bb)


def _hy_out_kernel(z_ref, x_ref, m_ref, w_ref, g_ref, b_ref, out_ref):
    y = _dot(z_ref[0], w_ref[...])
    out_ref[0] = _res_ln(x_ref[0], m_ref[0][2:3], y, g_ref[...], b_ref[...])


def _hy_out(z, x, mods, w, g, bb):
    b, t, d = x.shape
    tm = min(t, 512)
    return pl.pallas_call(
        _hy_out_kernel,
        grid=(b, t // tm),
        in_specs=[pl.BlockSpec((1, tm, d), lambda i, j: (i, j, 0)),
                  pl.BlockSpec((1, tm, d), lambda i, j: (i, j, 0)),
                  pl.BlockSpec((1, 8, d), lambda i, j: (i, 0, 0)),
                  pl.BlockSpec((d, d), lambda i, j: (0, 0)),
                  pl.BlockSpec((1, d), lambda i, j: (0, 0)),
                  pl.BlockSpec((1, d), lambda i, j: (0, 0))],
        out_specs=pl.BlockSpec((1, tm, d), lambda i, j: (i, j, 0)),
        out_shape=jax.ShapeDtypeStruct((b, t, d), F32),
        compiler_params=_cparams(("arbitrary", "arbitrary")),
        name="hy_out",
    )(z, x, mods, w, g, bb)


def _gla_consts(fwd):
    c = GLA_CHUNK
    i = np.arange(c)
    tri = ((i[:, None] >= i[None, :]) if fwd else (i[:, None] <= i[None, :])).astype(np.float32)
    halves = [c >> (s + 1) for s in range(int(math.log2(c)))]
    nl = len(halves)
    wst = np.zeros(((nl + 1) * c, c), np.float32)
    wst[:c] = tri
    negq = np.zeros((nl * c, GLA_HK), np.float32)
    negk = np.zeros((nl * c, GLA_HK), np.float32)
    msk = np.zeros(((nl + 1) * c, c), np.float32)
    msk[:c] = np.eye(c)
    for lv, half in enumerate(halves):
        blk = i // (2 * half)
        upper = (i % (2 * half)) >= half
        ref = blk * 2 * half + (half - 1 if fwd else half)
        wst[(lv + 1) * c:(lv + 2) * c] = tri - tri[ref]
        qside = upper if fwd else ~upper
        negq[lv * c:(lv + 1) * c] = np.where(qside, 0.0, GLA_NEG)[:, None]
        negk[lv * c:(lv + 1) * c] = np.where(~qside, 0.0, GLA_NEG)[:, None]
        msk[(lv + 1) * c:(lv + 2) * c] = ((blk[:, None] == blk[None, :]) & qside[:, None] & (~qside)[None, :])
    return (jnp.asarray(wst, BF16), jnp.asarray(negq), jnp.asarray(negk), jnp.asarray(msk))


def _gla_group(chains, eye):
    c = GLA_CHUNK
    for ch in chains:
        wst = ch["consts"][0]
        ch["est"] = _dot_split2_l(wst, ch["ga"])
    for ch in chains:
        _, negq, negk, _ = ch["consts"]
        q, k, est = ch["q"], ch["k"], ch["est"]
        nl = negq.shape[0] // c
        cum = est[0:c]
        tot = cum[c - 1:c] if ch["fwd"] else cum[0:1]
        ch["qt"] = (q * jnp.exp2(cum)).astype(BF16)
        ch["kt"] = (k * jnp.exp2(tot - cum)).astype(BF16)
        ch["dec"] = jnp.exp2(tot)
        qs, ks = [q.astype(BF16)], [k.astype(BF16)]
        for lv in range(nl):
            e = est[(lv + 1) * c:(lv + 2) * c]
            sl = slice(lv * c, (lv + 1) * c)
            qs.append((q * jnp.exp2(e + negq[sl])).astype(BF16))
            ks.append((k * jnp.exp2(negk[sl] - e)).astype(BF16))
        ch["qs"], ch["ks"] = qs, ks
    for ch in chains:
        ch["ps"] = [_dot_nt(a, b) for a, b in zip(ch["qs"], ch["ks"])]
    for ch in chains:
        msk = ch["consts"][3]
        attn = None
        for lv, p in enumerate(ch["ps"]):
            term = p * msk[lv * c:(lv + 1) * c]
            attn = term if attn is None else attn + term
        ch["attn"] = attn.astype(BF16)
    row = lax.broadcasted_iota(jnp.int32, (2 * SUBLANE, 1), 0)
    for ch in chains:
        vb = ch["v"].astype(BF16)
        ch["o"] = jnp.dot(ch["attn"], vb, preferred_element_type=F32)
        dec = ch["dec"]
        hi = dec.astype(BF16).astype(F32)
        mid = (dec - hi).astype(BF16).astype(F32)
        extra = jnp.where(row == 0, hi, jnp.where(row == 1, mid, 0.0)).astype(BF16)
        kt_t = _dot_nt(eye, jnp.concatenate([ch["kt"], extra], axis=0))
        ch["dec_col"] = kt_t[:, c:c + 1] + kt_t[:, c + 1:c + 2]
        ch["upd"] = jnp.dot(kt_t[:, 0:c].astype(BF16), vb, preferred_element_type=F32)


def _dot_split2_l(m, x):
    hi = x.astype(BF16)
    mid = (x - hi.astype(F32)).astype(BF16)
    d = lambda p: jnp.dot(m, p, preferred_element_type=F32)
    return d(hi) + d(mid)


def _gla_kernel(q_ref, k_ref, v_ref, a_ref, w2_ref, b2_ref, s0_ref,
                wf_ref, nqf_ref, nkf_ref, mf_ref, wb_ref, nqb_ref, nkb_ref, mb_ref, eye_ref,
                o_ref, s_ref, gaf, gab, sf, sb, *, t):
    c = GLA_CHUNK
    un = GLA_UNROLL
    nc = t // c
    pb = min(t, 512)

    def prep(i, carry):
        r = pl.multiple_of(i * pb, pb)
        a = a_ref[0, pl.ds(r, pb), :]
        for z, ga in ((0, gaf), (1, gab)):
            logit = _dot_f32x3(a, w2_ref[z]) + b2_ref[z]
            ga[pl.ds(r, pb), :] = _log_sigmoid(logit) * (LOG2E / GLA_GATE_NORM)
        o_ref[0, pl.ds(r, pb), :] = jnp.zeros((pb, GLA_HV), F32)
        return carry

    lax.fori_loop(0, t // pb, prep, 0)
    sf[...] = s0_ref[0, 0, 0]
    sb[...] = s0_ref[0, 0, 1]
    scale = GLA_HK ** -0.5
    fconst = (wf_ref, nqf_ref, nkf_ref, mf_ref)
    bconst = (wb_ref, nqb_ref, nkb_ref, mb_ref)

    def body(ci, carry):
        chains = []
        for fwd, ga, consts in ((True, gaf, fconst), (False, gab, bconst)):
            cvals = tuple(x[...] for x in consts)
            for u in range(un):
                idx = ci * un + u
                r = pl.multiple_of((idx if fwd else nc - 1 - idx) * c, c)
                chains.append(dict(r=r, fwd=fwd, consts=cvals, ga=ga[pl.ds(r, c), :],
                                   q=q_ref[0, pl.ds(r, c), :].astype(F32) * scale,
                                   k=k_ref[0, pl.ds(r, c), :].astype(F32), v=v_ref[0, pl.ds(r, c), :]))
        _gla_group(chains, eye_ref[...])
        for fwd, st_ref in ((True, sf), (False, sb)):
            st = st_ref[...]
            for ch in chains:
                if ch["fwd"] != fwd:
                    continue
                o = ch["o"] + jnp.dot(ch["qt"], st.astype(BF16), preferred_element_type=F32)
                o_ref[0, pl.ds(ch["r"], c), :] += o
                st = st * ch["dec_col"] + ch["upd"]
            st_ref[...] = st
        return carry

    lax.fori_loop(0, nc // un, body, 0)
    s_ref[0, 0, 0] = sf[...]
    s_ref[0, 0, 1] = sb[...]


def _gla_scan(pmain, pa, w2p, b2p, s0):
    b, t, _ = pmain.shape
    h = GLA_HEADS
    assert t % (GLA_CHUNK * GLA_UNROLL) == 0
    consts = _gla_consts(True) + _gla_consts(False) + (jnp.asarray(np.eye(GLA_HK), BF16),)
    kblk = GLA_DK // GLA_HK
    vblk = (2 * GLA_DK) // GLA_HV
    cspec = lambda a: pl.BlockSpec(a.shape, lambda i, j: (0,) * a.ndim)
    return pl.pallas_call(
        functools.partial(_gla_kernel, t=t),
        grid=(b, h),
        in_specs=[pl.BlockSpec((1, t, GLA_HK), lambda i, j: (i, 0, j)),
                  pl.BlockSpec((1, t, GLA_HK), lambda i, j: (i, 0, kblk + j)),
                  pl.BlockSpec((1, t, GLA_HV), lambda i, j: (i, 0, vblk + j)),
                  pl.BlockSpec((1, t, LANE), lambda i, j: (i, 0, 0)),
                  pl.BlockSpec((2, LANE, GLA_HK), lambda i, j: (0, 0, j)),
                  pl.BlockSpec((2, 1, GLA_HK), lambda i, j: (0, 0, j)),
                  pl.BlockSpec((1, 1, 2, GLA_HK, GLA_HV), lambda i, j: (i, j, 0, 0, 0))]
                 + [cspec(a) for a in consts],
        out_specs=[pl.BlockSpec((1, t, GLA_HV), lambda i, j: (i, 0, j)),
                   pl.BlockSpec((1, 1, 2, GLA_HK, GLA_HV), lambda i, j: (i, j, 0, 0, 0))],
        out_shape=[jax.ShapeDtypeStruct((b, t, GLA_DV), F32),
                   jax.ShapeDtypeStruct((b, h, 2, GLA_HK, GLA_HV), F32)],
        scratch_shapes=[pltpu.VMEM((t, GLA_HK), F32), pltpu.VMEM((t, GLA_HK), F32),
                        pltpu.VMEM((GLA_HK, GLA_HV), F32), pltpu.VMEM((GLA_HK, GLA_HV), F32)],
        compiler_params=_cparams(("arbitrary", "arbitrary")),
        name="gla_scan",
    )(pmain, pmain, pmain, pa, w2p, b2p, s0, *consts)


def _ssd_consts():
    c = SSD_CHUNK
    i = np.arange(c)
    tril = (i[:, None] >= i[None, :]).astype(np.float32)
    triu = (i[:, None] <= i[None, :]).astype(np.float32)
    eye = np.eye(SSD_STATE, dtype=np.float32)
    sel16 = np.eye(16, LANE, dtype=np.float32)
    e4 = np.zeros((2, LANE, SSD_GW), np.float32)
    for z in range(2):
        for r in range(SSD_REP):
            e4[z, z * SSD_REP + r, r * SSD_HEADDIM:(r + 1) * SSD_HEADDIM] = 1.0
    lm = np.zeros((SSD_REP * c, SSD_GW), np.float32)
    for r in range(SSD_REP):
        lm[r * c:(r + 1) * c, r * SSD_HEADDIM:(r + 1) * SSD_HEADDIM] = 1.0
    return (jnp.asarray(tril, BF16), jnp.asarray(triu, BF16), jnp.asarray(tril), jnp.asarray(triu),
            jnp.asarray(eye, BF16), jnp.asarray(sel16, BF16), jnp.asarray(e4, BF16), jnp.asarray(lm))


def _conv_block(in_ref, w, bias, i, nblk, rows, t, taps):
    halo = 2 * SUBLANE
    r = pl.multiple_of(i * rows, rows)
    cur = in_ref[0, pl.ds(r, rows), :].astype(F32)
    rp = pl.multiple_of(jnp.maximum(r - halo, 0), halo)
    rn = pl.multiple_of(jnp.minimum(r + rows, t - halo), halo)
    prev = jnp.where(i > 0, in_ref[0, pl.ds(rp, halo), :].astype(F32), 0.0)
    nxt = jnp.where(i < nblk - 1, in_ref[0, pl.ds(rn, halo), :].astype(F32), 0.0)
    ext = jnp.concatenate([prev, cur, nxt], axis=0)
    half = taps // 2
    acc = bias
    for j in range(taps):
        off = halo - half + j
        acc = acc + w[j:j + 1] * ext[off:off + rows]
    return r, acc


def _ssd_group(chains, eye, sel16, lm):
    c = SSD_CHUNK
    for ch in chains:
        ch["cum"] = _dot_exact_l(ch["tri"], ch["das"])
        ch["dt_e"] = _dot_split2_r(ch["ds"], ch["e4z"])
        ch["ccb"], bcb = ch["cc"].astype(BF16), ch["bc"].astype(BF16)
        ch["cb"] = _dot_nt(ch["ccb"], bcb)
        ch["bc_t"] = _dot_nt(eye, bcb).astype(BF16)
    for ch in chains:
        ch["cum_t"] = _dot_exact_nt(sel16, ch["cum"])
        ch["cum_e"] = _dot_split2_r(ch["cum"], ch["e4z"])
    for ch in chains:
        cum, cum_t, cum_e = ch["cum"], ch["cum_t"], ch["cum_e"]
        tot_e = cum_e[c - 1:c] if ch["fwd"] else cum_e[0:1]
        xdt = ch["xg"] * ch["dt_e"]
        cb = ch["cb"] * ch["mask"]
        ms = []
        for r in range(SSD_REP):
            q = ch["z"] * SSD_REP + r
            seg = cum[:, q:q + 1] - cum_t[q:q + 1, :]
            ms.append((cb * jnp.exp2(jnp.minimum(seg, 0.0))).astype(BF16))
        ch["mcat"] = jnp.concatenate(ms, axis=1)
        ch["xbd"] = (jnp.concatenate([xdt] * SSD_REP, axis=0) * lm).astype(BF16)
        ch["w"] = (xdt * jnp.exp2(tot_e - cum_e)).astype(BF16)
        ch["dece"] = jnp.exp2(cum_e)
        ch["dec"] = jnp.exp2(tot_e)
    for ch in chains:
        ch["y"] = jnp.dot(ch["mcat"], ch["xbd"], preferred_element_type=F32)
        ch["upd"] = jnp.dot(ch["bc_t"], ch["w"], preferred_element_type=F32)


def _ssd_kernel(x_ref, bm_ref, cm_ref, dt_ref, wx_ref, wb_ref, wc_ref, bx_ref, bb_ref, bcb_ref,
                dtb_ref, alog_ref, dsk_ref, s0_ref,
                tl_ref, tu_ref, ml_ref, mu_ref, eye_ref, sel16_ref, e4_ref, lm_ref,
                y_ref, s_ref, xc, bcs, ccs, dsel, dasel, sf, sb, *, t):
    c = SSD_CHUNK
    un = SSD_UNROLL
    nc = t // c
    pb = c
    nblk = t // pb

    shift = (LANE - 2 * SSD_REP * pl.program_id(1)) % LANE

    def prep(i):
        for in_ref, w_ref, b_ref, out in ((x_ref, wx_ref, bx_ref, xc), (bm_ref, wb_ref, bb_ref, bcs),
                                          (cm_ref, wc_ref, bcb_ref, ccs)):
            r, acc = _conv_block(in_ref, w_ref[...], b_ref[...], i, nblk, pb, t, SSD_CONV)
            val = _silu(acc)
            out[pl.ds(r, pb), :] = val
            if out is xc:
                y_ref[0, pl.ds(r, pb), :] = val * dsk_ref[...]
        r = pl.multiple_of(i * pb, pb)
        dtp = _softplus(dt_ref[0, pl.ds(r, pb), :] + dtb_ref[...])
        a2 = jnp.exp(alog_ref[...]) * (-LOG2E)
        dsel[pl.ds(r, pb), :] = pltpu.roll(dtp, shift, 1)
        dasel[pl.ds(r, pb), :] = pltpu.roll(dtp * a2, shift, 1)

    for blk in sorted(set(range(un)) | set(nc - 1 - u for u in range(un))):
        prep(jnp.int32(blk))
    sf[...] = s0_ref[0, 0, 0]
    sb[...] = s0_ref[0, 0, 1]
    n_ahead = max(nc // (2 * un) - 1, 0)

    def body(ci, carry, ahead):
        chains = []
        for fwd, z, tri_ref, mk_ref in ((True, 0, tl_ref, ml_ref), (False, 1, tu_ref, mu_ref)):
            tri, mask, e4z = tri_ref[...], mk_ref[...], e4_ref[z]
            for u in range(un):
                idx = ci * un + u
                r = pl.multiple_of((idx if fwd else nc - 1 - idx) * c, c)
                chains.append(dict(r=r, fwd=fwd, z=z, tri=tri, mask=mask, e4z=e4z,
                                   xg=xc[pl.ds(r, c), :], bc=bcs[pl.ds(r, c), :], cc=ccs[pl.ds(r, c), :],
                                   ds=dsel[pl.ds(r, c), :], das=dasel[pl.ds(r, c), :]))
        if ahead:
            for u in range(un):
                nxt = (ci + 1) * un + u
                prep(nxt)
                prep(nc - 1 - nxt)
        _ssd_group(chains, eye_ref[...], sel16_ref[...], lm_ref[...])
        for fwd, st_ref in ((True, sf), (False, sb)):
            st = st_ref[...]
            for ch in chains:
                if ch["fwd"] != fwd:
                    continue
                y = ch["y"] + jnp.dot(ch["ccb"], st.astype(BF16), preferred_element_type=F32) * ch["dece"]
                y_ref[0, pl.ds(ch["r"], c), :] += y
                st = st * ch["dec"] + ch["upd"]
            st_ref[...] = st
        return carry

    lax.fori_loop(0, n_ahead, functools.partial(body, ahead=True), 0)
    lax.fori_loop(n_ahead, nc // un, functools.partial(body, ahead=False), 0)
    s_ref[0, 0, 0] = sf[...]
    s_ref[0, 0, 1] = sb[...]


def _ssd_scan(pmain, pdt, conv_w, conv_b, dtb, alog, dskip, s0):
    b, t, _ = pmain.shape
    g = SSD_GROUPS
    assert t % (SSD_CHUNK * SSD_UNROLL) == 0
    tril, triu, mtril, mtriu, eye, sel16, e4, lm = _ssd_consts()
    xblk = SSD_DI // SSD_GW
    bblk = (2 * SSD_DI) // SSD_STATE
    cblk = bblk + SSD_GN // SSD_STATE
    wbblk = SSD_DI // SSD_STATE
    wcblk = wbblk + SSD_GN // SSD_STATE
    cspec = lambda a: pl.BlockSpec(a.shape, lambda i, j: (0,) * a.ndim)
    return pl.pallas_call(
        functools.partial(_ssd_kernel, t=t),
        grid=(b, g),
        in_specs=[pl.BlockSpec((1, t, SSD_GW), lambda i, j: (i, 0, xblk + j)),
                  pl.BlockSpec((1, t, SSD_STATE), lambda i, j: (i, 0, bblk + j)),
                  pl.BlockSpec((1, t, SSD_STATE), lambda i, j: (i, 0, cblk + j)),
                  pl.BlockSpec((1, t, LANE), lambda i, j: (i, 0, 0)),
                  pl.BlockSpec((SSD_CONV, SSD_GW), lambda i, j: (0, j)),
                  pl.BlockSpec((SSD_CONV, SSD_STATE), lambda i, j: (0, wbblk + j)),
                  pl.BlockSpec((SSD_CONV, SSD_STATE), lambda i, j: (0, wcblk + j)),
                  pl.BlockSpec((1, SSD_GW), lambda i, j: (0, j)),
                  pl.BlockSpec((1, SSD_STATE), lambda i, j: (0, wbblk + j)),
                  pl.BlockSpec((1, SSD_STATE), lambda i, j: (0, wcblk + j)),
                  pl.BlockSpec((1, LANE), lambda i, j: (0, 0)),
                  pl.BlockSpec((1, LANE), lambda i, j: (0, 0)),
                  pl.BlockSpec((1, SSD_GW), lambda i, j: (0, j)),
                  pl.BlockSpec((1, 1, 2, SSD_STATE, SSD_GW), lambda i, j: (i, j, 0, 0, 0)),
                  cspec(tril), cspec(triu), cspec(mtril), cspec(mtriu), cspec(eye), cspec(sel16),
                  cspec(e4), cspec(lm)],
        out_specs=[pl.BlockSpec((1, t, SSD_GW), lambda i, j: (i, 0, j)),
                   pl.BlockSpec((1, 1, 2, SSD_STATE, SSD_GW), lambda i, j: (i, j, 0, 0, 0))],
        out_shape=[jax.ShapeDtypeStruct((b, t, SSD_DI), F32),
                   jax.ShapeDtypeStruct((b, g, 2, SSD_STATE, SSD_GW), F32)],
        scratch_shapes=[pltpu.VMEM((t, SSD_GW), F32), pltpu.VMEM((t, SSD_STATE), F32),
                        pltpu.VMEM((t, SSD_STATE), F32), pltpu.VMEM((t, LANE), F32),
                        pltpu.VMEM((t, LANE), F32),
                        pltpu.VMEM((SSD_STATE, SSD_GW), F32), pltpu.VMEM((SSD_STATE, SSD_GW), F32)],
        compiler_params=_cparams(("arbitrary", "arbitrary")),
        name="ssd_scan",
    )(pmain, pmain, pmain, pdt, conv_w, conv_w, conv_w, conv_b, conv_b, conv_b,
      dtb, alog, dskip, s0, tril, triu, mtril, mtriu, eye, sel16, e4, lm)


def _conv3_kernel(p_ref, w_ref, b_ref, *out_refs, t):
    pb = min(t, 256)
    nblk = t // pb

    def blk(i, carry):
        r, acc = _conv_block(p_ref, w_ref[...], b_ref[...], i, nblk, pb, t, HY_SHORT)
        out_refs[0][0, pl.ds(r, pb), :] = acc
        if len(out_refs) > 1:
            out_refs[1][0, pl.ds(r, pb), :] = acc.astype(BF16)
        return carry

    lax.fori_loop(0, nblk, blk, 0)


def _conv3(p, w, bias, col0, ncols, with_bf16):
    b, t, _ = p.shape
    tn = 256
    off = col0 // tn
    out_shape = [jax.ShapeDtypeStruct((b, t, ncols), F32)]
    out_specs = [pl.BlockSpec((1, t, tn), lambda i, j: (i, 0, j))]
    if with_bf16:
        out_shape.append(jax.ShapeDtypeStruct((b, t, ncols), BF16))
        out_specs.append(pl.BlockSpec((1, t, tn), lambda i, j: (i, 0, j)))
    return pl.pallas_call(
        functools.partial(_conv3_kernel, t=t),
        grid=(b, ncols // tn),
        in_specs=[pl.BlockSpec((1, t, tn), lambda i, j: (i, 0, off + j)),
                  pl.BlockSpec((HY_SHORT, tn), lambda i, j: (0, off + j)),
                  pl.BlockSpec((1, tn), lambda i, j: (0, off + j))],
        out_specs=out_specs,
        out_shape=out_shape,
        compiler_params=_cparams(("arbitrary", "arbitrary")),
        name="hy_conv3",
    )(p, w, bias)


def _hy_pos_emb(l):
    bands = (HY_EMB - 1) // 2
    t = np.linspace(0.0, 1.0, l)[:, None]
    w = 2 * math.pi * np.arange(l, dtype=np.float64)[:, None] / l
    ang = np.linspace(1e-4, bands - 1, bands)[None, :] * w
    z = np.concatenate([t, np.cos(ang), -np.sin(ang)], axis=-1)
    zp = np.zeros((l, LANE), np.float32)
    zp[:, :HY_EMB] = z
    return zp


def _filt_kernel(z_ref, w1_ref, b1_ref, w2_ref, b2_ref, w3_ref, b3_ref, w4_ref, fr_ref, dl_ref, o_ref, *, l, tl):
    fr = fr_ref[...]
    h = jnp.sin(fr * (_dot_f32(z_ref[...], w1_ref[...]) + b1_ref[...]))
    h = jnp.sin(fr * (_dot_f32(h, w2_ref[...]) + b2_ref[...]))
    h = jnp.sin(fr * (_dot_f32(h, w3_ref[...]) + b3_ref[...]))
    hh = _dot_f32(h, w4_ref[...])
    row = lax.broadcasted_iota(jnp.int32, (tl, 1), 0) + pl.program_id(0) * tl
    tt = row.astype(F32) * (1.0 / (l - 1))
    hh = hh * jnp.exp(-tt * dl_ref[...])
    drop = jnp.logical_and(row == 0, (pl.program_id(1) % 2) == 1)
    o_ref[...] = jnp.where(drop, 0.0, hh)


def _hy_filters(l, w1, b1, w2, b2, w3, b3, w4, freq):
    d = D_MODEL
    tl = min(l, 512)
    pad2 = lambda a: jnp.zeros((LANE, LANE), F32).at[:a.shape[0], :a.shape[1]].set(a)
    pad1 = lambda a: jnp.zeros((1, LANE), F32).at[0, :a.shape[0]].set(a)
    w4p = jnp.zeros((LANE, HY_ORDER * 2 * d), F32).at[:HY_FW].set(w4)
    deltas = np.abs(np.linspace(math.log(HY_DECAY_TARGET) / HY_FAST_DECAY,
                                math.log(HY_DECAY_TARGET) / HY_SLOW_DECAY, d))
    dl = jnp.asarray(np.tile(deltas, HY_ORDER * 2)[None, :], F32)
    full = lambda a: pl.BlockSpec(a.shape, lambda i, j: (0, 0))
    small = [pad2(w1), pad1(b1), pad2(w2), pad1(b2), pad2(w3), pad1(b3)]
    return pl.pallas_call(
        functools.partial(_filt_kernel, l=l, tl=tl),
        grid=(l // tl, HY_ORDER * 2),
        in_specs=[pl.BlockSpec((tl, LANE), lambda i, j: (i, 0))] + [full(a) for a in small]
                 + [pl.BlockSpec((LANE, d), lambda i, j: (0, j)), pl.BlockSpec((1, LANE), lambda i, j: (0, 0)),
                    pl.BlockSpec((1, d), lambda i, j: (0, j))],
        out_specs=pl.BlockSpec((tl, d), lambda i, j: (i, j)),
        out_shape=jax.ShapeDtypeStruct((l, HY_ORDER * 2 * d), F32),
        compiler_params=_cparams(("arbitrary", "arbitrary")),
        name="hy_filt",
    )(jnp.asarray(_hy_pos_emb(l)), *small, w4p, pad1(freq), dl)


def _dft_tables_np(l):
    n = 2 * l
    k = np.arange(l)
    ang = 2 * math.pi * ((k[:, None] * k[None, :]) % n) / n
    cm = np.cos(ang)
    sm = -np.sin(ang)
    alt = np.where(k % 2 == 0, 1.0, -1.0)
    s_fwd = sm.copy()
    s_fwd[0, :] = alt
    s_inv = sm.copy()
    s_inv[:, 0] = alt
    return cm, s_fwd, s_inv


def _dft_tab_kernel(ca_ref, sa_ref, cb_ref, sb_ref, c_ref, sf_ref, si_ref, *, rb):
    a = pl.program_id(0)
    ca, sa = ca_ref[0], sa_ref[0]
    cb, sb = cb_ref[...], sb_ref[...]
    cm = ca * cb - sa * sb
    sm = -(sa * cb + ca * sb)
    l = cm.shape[1]
    row = lax.broadcasted_iota(jnp.int32, (rb, l), 0)
    col = lax.broadcasted_iota(jnp.int32, (rb, l), 1)
    alt_col = jnp.where(col % 2 == 0, 1.0, -1.0)
    alt_row = jnp.where(row % 2 == 0, 1.0, -1.0)
    c_ref[...] = cm.astype(BF16)
    sf_ref[...] = jnp.where(jnp.logical_and(row == 0, a == 0), alt_col, sm).astype(BF16)
    si_ref[...] = jnp.where(col == 0, alt_row, sm).astype(BF16)


def _dft_tables(l):
    if l <= 512:
        return tuple(jnp.asarray(m, BF16) for m in _dft_tables_np(l))
    rb = 64
    na = l // rb
    n = 2 * l
    nn = np.arange(l)
    aa = np.arange(na)
    bb = np.arange(rb)
    ang_a = 2 * math.pi * ((aa[:, None] * rb * nn[None, :]) % n) / n
    ang_b = 2 * math.pi * ((bb[:, None] * nn[None, :]) % n) / n
    ca = jnp.asarray(np.cos(ang_a)[:, None, :], F32)
    sa = jnp.asarray(np.sin(ang_a)[:, None, :], F32)
    cb = jnp.asarray(np.cos(ang_b), F32)
    sb = jnp.asarray(np.sin(ang_b), F32)
    rowspec = pl.BlockSpec((1, 1, l), lambda a: (a, 0, 0))
    tabspec = pl.BlockSpec((rb, l), lambda a: (0, 0))
    outspec = pl.BlockSpec((rb, l), lambda a: (a, 0))
    return tuple(pl.pallas_call(
        functools.partial(_dft_tab_kernel, rb=rb),
        grid=(na,),
        in_specs=[rowspec, rowspec, tabspec, tabspec],
        out_specs=[outspec] * 3,
        out_shape=[jax.ShapeDtypeStruct((l, l), BF16)] * 3,
        compiler_params=_cparams(("arbitrary",)),
        name="dft_tables",
    )(ca, sa, cb, sb))


def _spec_filt_kernel(c_ref, s_ref, a_ref, b_ref, h_ref, accr, acci, accn, *, l):
    kk = pl.program_id(2)

    @pl.when(kk == 0)
    def _():
        accr[...] = jnp.zeros_like(accr)
        acci[...] = jnp.zeros_like(acci)
        accn[...] = jnp.zeros_like(accn)

    a, bw = a_ref[...], b_ref[...]
    sm = (a + bw).astype(BF16)
    df = (a - bw).astype(BF16)
    accr[...] += jnp.dot(c_ref[...], sm, preferred_element_type=F32)
    acci[...] += jnp.dot(s_ref[...], df, preferred_element_type=F32)
    accn[...] += jnp.dot(s_ref[...], sm, preferred_element_type=F32)

    @pl.when(kk == pl.num_programs(2) - 1)
    def _():
        tm = accr.shape[0]
        row0 = (lax.broadcasted_iota(jnp.int32, (tm, 1), 0) + pl.program_id(0) * tm) == 0
        scale = jnp.where(row0, 0.5 / l, 1.0 / l)
        h_ref[0] = accr[...] * scale
        h_ref[1] = jnp.where(row0, accn[...], acci[...]) * scale


def _spec_filt(cm, s_fwd, hh, l):
    d = D_MODEL
    tm, tn, tk = min(l, 512), 512, min(l, 1024)
    nd = d // tn
    return pl.pallas_call(
        functools.partial(_spec_filt_kernel, l=l),
        grid=(l // tm, HY_ORDER * nd, l // tk),
        in_specs=[pl.BlockSpec((tm, tk), lambda m, j, k: (m, k)),
                  pl.BlockSpec((tm, tk), lambda m, j, k: (m, k)),
                  pl.BlockSpec((tk, tn), lambda m, j, k: (k, (j // nd) * 2 * nd + j % nd)),
                  pl.BlockSpec((tk, tn), lambda m, j, k: (k, (j // nd) * 2 * nd + nd + j % nd))],
        out_specs=pl.BlockSpec((2, tm, tn), lambda m, j, k: (0, m, j)),
        out_shape=jax.ShapeDtypeStruct((2, l, HY_ORDER * d), F32),
        scratch_shapes=[pltpu.VMEM((tm, tn), F32)] * 3,
        compiler_params=_cparams(("arbitrary", "arbitrary", "arbitrary")),
        name="hy_spec_filt",
    )(cm, s_fwd, hh, hh)


def _spec_sig_kernel(c_ref, s_ref, u_ref, h_ref, y_ref, accr, acci):
    kk = pl.program_id(2)

    @pl.when(kk == 0)
    def _():
        accr[...] = jnp.zeros_like(accr)
        acci[...] = jnp.zeros_like(acci)

    u = u_ref[0]
    accr[...] += jnp.dot(c_ref[...], u, preferred_element_type=F32)
    acci[...] += jnp.dot(s_ref[...], u, preferred_element_type=F32)

    @pl.when(kk == pl.num_programs(2) - 1)
    def _():
        tm = accr.shape[0]
        row0 = (lax.broadcasted_iota(jnp.int32, (tm, 1), 0) + pl.program_id(0) * tm) == 0
        xr, xi = accr[...], acci[...]
        hr, hi = h_ref[0], h_ref[1]
        y_ref[0] = (xr * hr - jnp.where(row0, 0.0, xi * hi)).astype(BF16)
        y_ref[1] = jnp.where(row0, xi * hi, xr * hi + xi * hr).astype(BF16)


def _spec_sig(cm, s_fwd, ub, hspec, order, l):
    b, _, d = ub.shape
    tm, tn, tk = min(l, 1024), 512, min(l, 1024)
    nd = d // tn
    return pl.pallas_call(
        _spec_sig_kernel,
        grid=(l // tm, b * nd, l // tk),
        in_specs=[pl.BlockSpec((tm, tk), lambda m, j, k: (m, k)),
                  pl.BlockSpec((tm, tk), lambda m, j, k: (m, k)),
                  pl.BlockSpec((1, tk, tn), lambda m, j, k: (j // nd, k, j % nd)),
                  pl.BlockSpec((2, tm, tn), lambda m, j, k: (0, m, order * nd + j % nd))],
        out_specs=pl.BlockSpec((2, tm, tn), lambda m, j, k: (0, m, j)),
        out_shape=jax.ShapeDtypeStruct((2, l, b * d), BF16),
        scratch_shapes=[pltpu.VMEM((tm, tn), F32)] * 2,
        compiler_params=_cparams(("arbitrary", "arbitrary", "arbitrary")),
        name="hy_spec_sig",
    )(cm, s_fwd, ub, hspec)


def _inv_kernel(c_ref, s_ref, y_ref, u_ref, g_ref, bias_ref, *refs):
    out_refs, acc = refs[:-1], refs[-1]
    kk = pl.program_id(2)

    @pl.when(kk == 0)
    def _():
        acc[...] = jnp.zeros_like(acc)

    acc[...] += (jnp.dot(c_ref[...], y_ref[0], preferred_element_type=F32)
                 + jnp.dot(s_ref[...], y_ref[1], preferred_element_type=F32))

    @pl.when(kk == pl.num_programs(2) - 1)
    def _():
        res = g_ref[0] * (acc[...] + u_ref[0] * bias_ref[...])
        out_refs[0][0] = res
        if len(out_refs) > 1:
            out_refs[1][0] = res.astype(BF16)


def _spec_inv(cm, s_inv, yspec, u, ucol, gate, gcol, bias, with_bf16, l):
    b = u.shape[0]
    d = D_MODEL
    tt, tn, tk = min(l, 1024), 512, min(l, 1024)
    nd = d // tn
    uo, go = ucol // tn, gcol // tn
    out_shape = [jax.ShapeDtypeStruct((b, l, d), F32)]
    out_specs = [pl.BlockSpec((1, tt, tn), lambda t, j, k: (j // nd, t, j % nd))]
    if with_bf16:
        out_shape.append(jax.ShapeDtypeStruct((b, l, d), BF16))
        out_specs.append(pl.BlockSpec((1, tt, tn), lambda t, j, k: (j // nd, t, j % nd)))
    return pl.pallas_call(
        _inv_kernel,
        grid=(l // tt, b * nd, l // tk),
        in_specs=[pl.BlockSpec((tt, tk), lambda t, j, k: (t, k)),
                  pl.BlockSpec((tt, tk), lambda t, j, k: (t, k)),
                  pl.BlockSpec((2, tk, tn), lambda t, j, k: (0, k, j)),
                  pl.BlockSpec((1, tt, tn), lambda t, j, k: (j // nd, t, uo + j % nd)),
                  pl.BlockSpec((1, tt, tn), lambda t, j, k: (j // nd, t, go + j % nd)),
                  pl.BlockSpec((1, tn), lambda t, j, k: (0, j % nd))],
        out_specs=out_specs,
        out_shape=out_shape,
        scratch_shapes=[pltpu.VMEM((tt, tn), F32)],
        compiler_params=_cparams(("arbitrary", "arbitrary", "arbitrary")),
        name="hy_spec_inv",
    )(cm, s_inv, yspec, u, gate, bias)


def _hyena_run(p, conv_w, conv_b, fw, h_bias):
    b, l, _ = p.shape
    d = D_MODEL
    v, vb = _conv3(p, conv_w, conv_b, 0, d, True)
    x12 = _conv3(p, conv_w, conv_b, d, 2 * d, False)[0]
    cm, s_fwd, s_inv = _dft_tables(l)
    hh = _hy_filters(l, *fw)
    hspec = _spec_filt(cm, s_fwd, hh, l)
    y1 = _spec_sig(cm, s_fwd, vb, hspec, 0, l)
    z, zb = _spec_inv(cm, s_inv, y1, v, 0, x12, 0, h_bias[0:1], True, l)
    y2 = _spec_sig(cm, s_fwd, zb, hspec, 1, l)
    return _spec_inv(cm, s_inv, y2, z, 0, x12, d, h_bias[1:2], False, l)[0]


def _snake_kernel(x_ref, j_ref, o_ref):
    jm = j_ref[...]
    for g in range(x_ref.shape[1] // (2 * GRID_W)):
        r0 = g * 2 * GRID_W
        o_ref[0, r0:r0 + GRID_W, :] = x_ref[0, r0:r0 + GRID_W, :]
        o_ref[0, r0 + GRID_W:r0 + 2 * GRID_W, :] = _dot_exact_l(jm, x_ref[0, r0 + GRID_W:r0 + 2 * GRID_W, :])


def _snake(h):
    b, l, ch = h.shape
    tm = 512
    jm = jnp.asarray(np.eye(GRID_W)[::-1].copy(), BF16)
    return pl.pallas_call(
        _snake_kernel,
        grid=(b, l // tm),
        in_specs=[pl.BlockSpec((1, tm, ch), lambda i, j: (i, j, 0)),
                  pl.BlockSpec((GRID_W, GRID_W), lambda i, j: (0, 0))],
        out_specs=pl.BlockSpec((1, tm, ch), lambda i, j: (i, j, 0)),
        out_shape=jax.ShapeDtypeStruct((b, l, ch), F32),
        compiler_params=_cparams(("arbitrary", "arbitrary")),
        name="snake",
    )(h, jm)


def _pad_cols(w, n):
    return jnp.zeros((w.shape[0], n), w.dtype).at[:, :w.shape[1]].set(w)


def kernel(x, c, ctx, c_ctx, ada_w, ada_b, ln_g, ln_b, ffn_w1, ffn_w2, gla_w_in, gla_w_a2, gla_b_a2, gla_norm, gla_w_out, ssd_w_in, ssd_conv_w, ssd_conv_b, ssd_dt_bias, ssd_a_log, ssd_d, ssd_norm, ssd_w_out, hy_w_in, hy_conv_w, hy_conv_b, hy_f_w1, hy_f_b1, hy_f_w2, hy_f_b2, hy_f_w3, hy_f_b3, hy_f_w4, hy_f_freq, hy_bias, hy_w_out):
    bsz, _, d = x.shape
    hl = _snake(x)
    hc = ctx.reshape(1, -1, d)
    cvec = jnp.zeros((16, d), F32).at[:bsz].set(c).at[bsz].set(c_ctx)
    mods = _ada(cvec, ada_w, ada_b).reshape(DEPTH, 16, 6, d)
    mods = jnp.concatenate([mods, jnp.zeros((DEPTH, 16, 2, d), F32)], axis=2)
    per_batch = lambda a: a.reshape(bsz, -1, a.shape[-1])
    like = lambda a, h: a.reshape(h.shape[0], -1, a.shape[-1])

    for i in range(DEPTH):
        kind, j = i % N_MIXERS, i // N_MIXERS
        need_ctx = i < DEPTH - 1
        ml = mods[i, :bsz]
        mc = mods[i, bsz][None]
        g0, b0 = ln_g[i, 0][None], ln_b[i, 0][None]
        g1, b1 = ln_g[i, 1][None], ln_b[i, 1][None]
        w1 = ffn_w1[i].astype(BF16)
        w2 = ffn_w2[i].astype(BF16)
        streams = [(hc, mc, True), (hl, ml, False)]
        if kind == 0:
            w_main = gla_w_in[j][:, :GLA_MAIN].astype(BF16)
            w_a = _pad_cols(gla_w_in[j][:, GLA_MAIN:], LANE).astype(BF16)
            w2p = jnp.zeros((2, LANE, GLA_DK), F32)
            for z in range(2):
                w2p = w2p.at[z, z * GLA_RANK:(z + 1) * GLA_RANK].set(gla_w_a2[j, z])
            b2p = gla_b_a2[j][:, None, :]
            w_out = gla_w_out[j].astype(BF16)
            ng = gla_norm[j][None]
            state = jnp.zeros((bsz, GLA_HEADS, 2, GLA_HK, GLA_HV), F32)
            new = []
            for h, m, is_ctx in streams:
                pmain = _proj(h, m, w_main, 1024, BF16)
                pa = _proj(h, m, w_a, LANE, F32)
                o, st = _gla_scan(per_batch(pmain), per_batch(pa), w2p, b2p, state)
                if is_ctx:
                    state = st
                if is_ctx and not need_ctx:
                    new.append(h)
                    continue
                new.append(_gla_out(like(o, h), pmain, h, m, ng, w_out, g0, b0))
            hc, hl = new
        elif kind == 1:
            perm = np.arange(2 * SSD_HEADS).reshape(2, SSD_GROUPS, SSD_REP).transpose(1, 0, 2).reshape(-1)
            w_main = ssd_w_in[j][:, :SSD_MAIN].astype(BF16)
            w_dt = _pad_cols(ssd_w_in[j][:, SSD_MAIN:][:, perm], LANE).astype(BF16)
            dtb = _pad_cols(ssd_dt_bias[j].reshape(1, -1)[:, perm], LANE)
            alog = _pad_cols(ssd_a_log[j].reshape(1, -1)[:, perm], LANE)
            dskip = jnp.repeat(ssd_d[j], SSD_HEADDIM)[None]
            cw = ssd_conv_w[j]
            cbias = ssd_conv_b[j][None]
            w_out = ssd_w_out[j].astype(BF16)
            ng = ssd_norm[j][None]
            state = jnp.zeros((bsz, SSD_GROUPS, 2, SSD_STATE, SSD_GW), F32)
            new = []
            for h, m, is_ctx in streams:
                pmain = _proj(h, m, w_main, 1024, BF16)
                pdt = _proj(h, m, w_dt, LANE, F32)
                y, st = _ssd_scan(per_batch(pmain), per_batch(pdt), cw, cbias, dtb, alog, dskip, state)
                if is_ctx:
                    state = st
                if is_ctx and not need_ctx:
                    new.append(h)
                    continue
                new.append(_ssd_out(like(y, h), pmain, h, m, ng, w_out, g0, b0))
            hc, hl = new
        else:
            w_in = hy_w_in[j].astype(BF16)
            w_out = hy_w_out[j].astype(BF16)
            fw = (hy_f_w1[j], hy_f_b1[j], hy_f_w2[j], hy_f_b2[j], hy_f_w3[j], hy_f_b3[j], hy_f_w4[j], hy_f_freq[j])
            new = []
            for h, m, is_ctx in streams:
                if is_ctx and not need_ctx:
                    new.append(h)
                    continue
                p = per_batch(_proj(h, m, w_in, 1024, BF16))
                zz = _hyena_run(p, hy_conv_w[j], hy_conv_b[j][None], fw, hy_bias[j])
                new.append(_hy_out(like(zz, h), h, m, w_out, g0, b0))
            hc, hl = new
        hl = _ffn(hl, ml, w1, w2, g1, b1)
        if need_ctx:
            hc = _ffn(hc, mc, w1, w2, g1, b1)
    return _snake(hl)
```

```python
import functools
import math

import numpy as np
import jax
import jax.numpy as jnp
from jax import lax
from jax.experimental import pallas as pl
from jax.experimental.pallas import tpu as pltpu

F32 = jnp.float32
BF16 = jnp.bfloat16

D_MODEL = 1024
DEPTH = 4
GRID_W = 64
N_MIXERS = 3
D_FF = 4 * D_MODEL
DEEPNORM_ALPHA = (2 * DEPTH) ** 0.25
LN_EPS = 1e-5
RMS_EPS = 1e-6

GLA_HEADS = 4
GLA_DK = D_MODEL // 2
GLA_DV = D_MODEL
GLA_HK = GLA_DK // GLA_HEADS
GLA_HV = GLA_DV // GLA_HEADS
GLA_RANK = 16
GLA_GATE_NORM = 16.0
GLA_CHUNK = 64
GLA_UNROLL = 4
GLA_NEG = -1e30
LOG2E = math.log2(math.e)
GLA_MAIN = 2 * GLA_DK + 2 * GLA_DV

SSD_DI = 2 * D_MODEL
SSD_HEADDIM = 64
SSD_HEADS = SSD_DI // SSD_HEADDIM
SSD_GROUPS = 8
SSD_REP = SSD_HEADS // SSD_GROUPS
SSD_STATE = 128
SSD_CONV = 5
SSD_CHUNK = 128
SSD_UNROLL = 2
SSD_GN = SSD_GROUPS * SSD_STATE
SSD_CONV_DIM = SSD_DI + 2 * SSD_GN
SSD_MAIN = SSD_DI + SSD_CONV_DIM
SSD_GW = SSD_REP * SSD_HEADDIM

HY_ORDER = 2
HY_SHORT = 3
HY_EMB = 33
HY_FW = 64
HY_DECAY_TARGET = 1e-2
HY_FAST_DECAY = 0.3
HY_SLOW_DECAY = 1.5
HY_RADIX = 4
HY_MIN_PHASE_LEN = 1024

LANE = 128
SUBLANE = 8
VMEM_LIMIT = 56 * 1024 * 1024


def _cparams(sem):
    return pltpu.CompilerParams(dimension_semantics=sem, vmem_limit_bytes=VMEM_LIMIT)


def _dot(a, b):
    return jnp.dot(a.astype(BF16), b.astype(BF16), preferred_element_type=F32)


def _dot_nt(a, b):
    return lax.dot_general(a.astype(BF16), b.astype(BF16), (((1,), (1,)), ((), ())),
                           preferred_element_type=F32)


def _split3(x):
    hi = x.astype(BF16)
    r1 = x - hi.astype(F32)
    mid = r1.astype(BF16)
    lo = (r1 - mid.astype(F32)).astype(BF16)
    return hi, mid, lo


def _dot_exact_l(m01, x):
    hi, mid, lo = _split3(x)
    d = lambda p: jnp.dot(m01, p, preferred_element_type=F32)
    return d(hi) + d(mid) + d(lo)


def _dot_exact_r(x, m01):
    hi, mid, lo = _split3(x)
    d = lambda p: jnp.dot(p, m01, preferred_element_type=F32)
    return d(hi) + d(mid) + d(lo)


def _dot_split2_r(x, m01):
    hi = x.astype(BF16)
    mid = (x - hi.astype(F32)).astype(BF16)
    d = lambda p: jnp.dot(p, m01, preferred_element_type=F32)
    return d(hi) + d(mid)


def _dot_exact_nt(m01, x):
    hi, mid, lo = _split3(x)
    d = lambda p: lax.dot_general(m01, p, (((1,), (1,)), ((), ())), preferred_element_type=F32)
    return d(hi) + d(mid) + d(lo)


def _dot_f32(a, b):
    ah, am, al = _split3(a)
    bh, bm, bl = _split3(b)
    d = lambda p, q: jnp.dot(p, q, preferred_element_type=F32)
    return (d(ah, bh) + (d(ah, bm) + d(am, bh)) + (d(ah, bl) + d(al, bh) + d(am, bm)))


def _dot_f32x3(a, b):
    ah, am, _ = _split3(a)
    bh, bm, _ = _split3(b)
    d = lambda p, q: jnp.dot(p, q, preferred_element_type=F32)
    return d(ah, bh) + (d(ah, bm) + d(am, bh))


def _silu(x):
    return x * jax.nn.sigmoid(x)


def _softplus(x):
    return jnp.maximum(x, 0.0) + jnp.log1p(jnp.exp(-jnp.abs(x)))


def _log_sigmoid(x):
    return -_softplus(-x)


def _layer_norm(h, g, b):
    mu = jnp.mean(h, -1, keepdims=True)
    d = h - mu
    var = jnp.mean(d * d, -1, keepdims=True)
    return d * lax.rsqrt(var + LN_EPS) * g + b


def _res_ln(x, gate, y, g, b):
    return _layer_norm(DEEPNORM_ALPHA * x + gate * y, g, b)


def _ada_kernel(c_ref, w_ref, b_ref, o_ref):
    s = _silu(c_ref[...])
    o_ref[0] = _dot_f32(s, w_ref[0]) + b_ref[0]


def _ada(cvec, ada_w, ada_b):
    tn = 1536
    n = 6 * D_MODEL
    return pl.pallas_call(
        _ada_kernel,
        grid=(DEPTH, n // tn),
        in_specs=[pl.BlockSpec((16, D_MODEL), lambda i, j: (0, 0)),
                  pl.BlockSpec((1, D_MODEL, tn), lambda i, j: (i, 0, j)),
                  pl.BlockSpec((1, 1, tn), lambda i, j: (i, 0, j))],
        out_specs=pl.BlockSpec((1, 16, tn), lambda i, j: (i, 0, j)),
        out_shape=jax.ShapeDtypeStruct((DEPTH, 16, n), F32),
        compiler_params=_cparams(("arbitrary", "arbitrary")),
        name="ada",
    )(cvec, ada_w, ada_b.reshape(DEPTH, 1, n))


def _proj_kernel(x_ref, m_ref, w_ref, o_ref, u_scr):
    @pl.when(pl.program_id(2) == 0)
    def _():
        m = m_ref[0]
        u_scr[...] = (x_ref[0] * (1.0 + m[1:2]) + m[0:1]).astype(BF16)

    o_ref[0] = jnp.dot(u_scr[...], w_ref[...], preferred_element_type=F32).astype(o_ref.dtype)


def _proj(x, mods, w, tn, out_dtype):
    b, t, d = x.shape
    n = w.shape[1]
    tm = min(t, 1024)
    return pl.pallas_call(
        _proj_kernel,
        grid=(b, t // tm, n // tn),
        in_specs=[pl.BlockSpec((1, tm, d), lambda i, j, k: (i, j, 0)),
                  pl.BlockSpec((1, 8, d), lambda i, j, k: (i, 0, 0)),
                  pl.BlockSpec((d, tn), lambda i, j, k: (0, k))],
        out_specs=pl.BlockSpec((1, tm, tn), lambda i, j, k: (i, j, k)),
        out_shape=jax.ShapeDtypeStruct((b, t, n), out_dtype),
        scratch_shapes=[pltpu.VMEM((tm, d), BF16)],
        compiler_params=_cparams(("arbitrary", "arbitrary", "arbitrary")),
        name="proj",
    )(x, mods, w)


def _ffn_kernel(x_ref, m_ref, w1_ref, w2_ref, g_ref, b_ref, o_ref, acc_ref, u_scr):
    f = pl.program_id(2)

    @pl.when(f == 0)
    def _():
        m = m_ref[0]
        acc_ref[...] = jnp.zeros_like(acc_ref)
        u_scr[...] = (x_ref[0] * (1.0 + m[4:5]) + m[3:4]).astype(BF16)

    a = jnp.square(jnp.maximum(jnp.dot(u_scr[...], w1_ref[...], preferred_element_type=F32), 0.0))
    acc_ref[...] += _dot(a, w2_ref[...])

    @pl.when(f == pl.num_programs(2) - 1)
    def _():
        o_ref[0] = _res_ln(x_ref[0], m_ref[0][5:6], acc_ref[...], g_ref[...], b_ref[...])


def _ffn(x, mods, w1, w2, g, bb):
    b, t, d = x.shape
    tm = min(t, 1024)
    tf = 512
    return pl.pallas_call(
        _ffn_kernel,
        grid=(b, t // tm, D_FF // tf),
        in_specs=[pl.BlockSpec((1, tm, d), lambda i, j, k: (i, j, 0)),
                  pl.BlockSpec((1, 8, d), lambda i, j, k: (i, 0, 0)),
                  pl.BlockSpec((d, tf), lambda i, j, k: (0, k)),
                  pl.BlockSpec((tf, d), lambda i, j, k: (k, 0)),
                  pl.BlockSpec((1, d), lambda i, j, k: (0, 0)),
                  pl.BlockSpec((1, d), lambda i, j, k: (0, 0))],
        out_specs=pl.BlockSpec((1, tm, d), lambda i, j, k: (i, j, 0)),
        out_shape=jax.ShapeDtypeStruct((b, t, d), F32),
        scratch_shapes=[pltpu.VMEM((tm, d), F32), pltpu.VMEM((tm, d), BF16)],
        compiler_params=_cparams(("arbitrary", "arbitrary", "arbitrary")),
        name="ffn",
    )(x, mods, w1, w2, g, bb)


def _gla_out_kernel(o_ref, gate_ref, x_ref, m_ref, ng_ref, w_ref, g_ref, b_ref, out_ref):
    o = o_ref[0]
    ng = ng_ref[...]
    parts = []
    for h in range(GLA_HEADS):
        oh = o[:, h * GLA_HV:(h + 1) * GLA_HV]
        r = lax.rsqrt(jnp.mean(oh * oh, -1, keepdims=True) + RMS_EPS)
        parts.append(oh * r * ng)
    z = jnp.concatenate(parts, axis=-1) * _silu(gate_ref[0].astype(F32))
    y = _dot(z, w_ref[...])
    out_ref[0] = _res_ln(x_ref[0], m_ref[0][2:3], y, g_ref[...], b_ref[...])


def _gla_out(o, pmain, x, mods, ng, w, g, bb):
    b, t, d = x.shape
    tm = min(t, 512)
    gate_blk = (2 * GLA_DK) // GLA_DV + 1
    return pl.pallas_call(
        _gla_out_kernel,
        grid=(b, t // tm),
        in_specs=[pl.BlockSpec((1, tm, GLA_DV), lambda i, j: (i, j, 0)),
                  pl.BlockSpec((1, tm, GLA_DV), lambda i, j: (i, j, gate_blk)),
                  pl.BlockSpec((1, tm, d), lambda i, j: (i, j, 0)),
                  pl.BlockSpec((1, 8, d), lambda i, j: (i, 0, 0)),
                  pl.BlockSpec((1, GLA_HV), lambda i, j: (0, 0)),
                  pl.BlockSpec((GLA_DV, d), lambda i, j: (0, 0)),
                  pl.BlockSpec((1, d), lambda i, j: (0, 0)),
                  pl.BlockSpec((1, d), lambda i, j: (0, 0))],
        out_specs=pl.BlockSpec((1, tm, d), lambda i, j: (i, j, 0)),
        out_shape=jax.ShapeDtypeStruct((b, t, d), F32),
        compiler_params=_cparams(("arbitrary", "arbitrary")),
        name="gla_out",
    )(o, pmain, x, mods, ng, w, g, bb)


def _ssd_out_kernel(y_ref, z_ref, x_ref, m_ref, ng_ref, w_ref, g_ref, b_ref, out_ref):
    yz = y_ref[0] * _silu(z_ref[0].astype(F32))
    r = lax.rsqrt(jnp.mean(yz * yz, -1, keepdims=True) + RMS_EPS)
    y = _dot(yz * r * ng_ref[...], w_ref[...])
    out_ref[0] = _res_ln(x_ref[0], m_ref[0][2:3], y, g_ref[...], b_ref[...])


def _ssd_out(y, pmain, x, mods, ng, w, g, bb):
    b, t, d = x.shape
    tm = min(t, 512)
    return pl.pallas_call(
        _ssd_out_kernel,
        grid=(b, t // tm),
        in_specs=[pl.BlockSpec((1, tm, SSD_DI), lambda i, j: (i, j, 0)),
                  pl.BlockSpec((1, tm, SSD_DI), lambda i, j: (i, j, 0)),
                  pl.BlockSpec((1, tm, d), lambda i, j: (i, j, 0)),
                  pl.BlockSpec((1, 8, d), lambda i, j: (i, 0, 0)),
                  pl.BlockSpec((1, SSD_DI), lambda i, j: (0, 0)),
                  pl.BlockSpec((SSD_DI, d), lambda i, j: (0, 0)),
                  pl.BlockSpec((1, d), lambda i, j: (0, 0)),
                  pl.BlockSpec((1, d), lambda i, j: (0, 0))],
        out_specs=pl.BlockSpec((1, tm, d), lambda i, j: (i, j, 0)),
        out_shape=jax.ShapeDtypeStruct((b, t, d), F32),
        compiler_params=_cparams(("arbitrary", "arbitrary")),
        name="ssd_out",
    )(y, pmain, x, mods, ng, w, g, bb)


def _hy_out_kernel(z_ref, x_ref, m_ref, w_ref, g_ref, b_ref, out_ref):
    y = _dot(z_ref[0], w_ref[...])
    out_ref[0] = _res_ln(x_ref[0], m_ref[0][2:3], y, g_ref[...], b_ref[...])


def _hy_out(z, x, mods, w, g, bb):
    b, t, d = x.shape
    tm = min(t, 512)
    return pl.pallas_call(
        _hy_out_kernel,
        grid=(b, t // tm),
        in_specs=[pl.BlockSpec((1, tm, d), lambda i, j: (i, j, 0)),
                  pl.BlockSpec((1, tm, d), lambda i, j: (i, j, 0)),
                  pl.BlockSpec((1, 8, d), lambda i, j: (i, 0, 0)),
                  pl.BlockSpec((d, d), lambda i, j: (0, 0)),
                  pl.BlockSpec((1, d), lambda i, j: (0, 0)),
                  pl.BlockSpec((1, d), lambda i, j: (0, 0))],
        out_specs=pl.BlockSpec((1, tm, d), lambda i, j: (i, j, 0)),
        out_shape=jax.ShapeDtypeStruct((b, t, d), F32),
        compiler_params=_cparams(("arbitrary", "arbitrary")),
        name="hy_out",
    )(z, x, mods, w, g, bb)


def _gla_consts(fwd):
    c = GLA_CHUNK
    i = np.arange(c)
    tri = ((i[:, None] >= i[None, :]) if fwd else (i[:, None] <= i[None, :])).astype(np.float32)
    halves = [c >> (s + 1) for s in range(int(math.log2(c)))]
    nl = len(halves)
    fine = [lv for lv, half in enumerate(halves) if half < SUBLANE]
    wst = np.zeros(((len(fine) + 1) * c, c), np.float32)
    wst[:c] = tri
    negq = np.zeros((nl * c, GLA_HK), np.float32)
    negk = np.zeros((nl * c, GLA_HK), np.float32)
    msk = np.zeros(((nl + 1) * c, c), np.float32)
    msk[:c] = np.eye(c)
    for lv, half in enumerate(halves):
        blk = i // (2 * half)
        upper = (i % (2 * half)) >= half
        ref = blk * 2 * half + (half - 1 if fwd else half)
        if lv in fine:
            fi = fine.index(lv)
            wst[(fi + 1) * c:(fi + 2) * c] = tri - tri[ref]
        qside = upper if fwd else ~upper
        negq[lv * c:(lv + 1) * c] = np.where(qside, 0.0, GLA_NEG)[:, None]
        negk[lv * c:(lv + 1) * c] = np.where(~qside, 0.0, GLA_NEG)[:, None]
        msk[(lv + 1) * c:(lv + 2) * c] = ((blk[:, None] == blk[None, :]) & qside[:, None] & (~qside)[None, :])
    return (jnp.asarray(wst, BF16), jnp.asarray(negq), jnp.asarray(negk), jnp.asarray(msk))


def _gla_group(chains, eye):
    c = GLA_CHUNK
    for ch in chains:
        wst = ch["consts"][0]
        ch["est"] = _dot_split2_l(wst, ch["ga"])
    for ch in chains:
        _, negq, negk, _ = ch["consts"]
        q, k, est, fwd = ch["q"], ch["k"], ch["est"], ch["fwd"]
        nl = negq.shape[0] // c
        ncoarse = nl + 1 - est.shape[0] // c
        cum = est[0:c]
        tot = cum[c - 1:c] if fwd else cum[0:1]
        ch["qt"] = (q * jnp.exp2(cum)).astype(BF16)
        ch["kt"] = (k * jnp.exp2(tot - cum)).astype(BF16)
        ch["dec"] = jnp.exp2(tot)
        qs, ks = [q.astype(BF16)], [k.astype(BF16)]
        for lv in range(nl):
            if lv < ncoarse:
                size = c >> lv
                parts = []
                for lo in range(0, c, size):
                    ref = lo + size // 2 - (1 if fwd else 0)
                    parts.append(cum[lo:lo + size] - cum[ref:ref + 1])
                e = parts[0] if len(parts) == 1 else jnp.concatenate(parts, axis=0)
            else:
                e = est[(lv - ncoarse + 1) * c:(lv - ncoarse + 2) * c]
            sl = slice(lv * c, (lv + 1) * c)
            qs.append((q * jnp.exp2(e + negq[sl])).astype(BF16))
            ks.append((k * jnp.exp2(negk[sl] - e)).astype(BF16))
        ch["qs"], ch["ks"] = qs, ks
    for ch in chains:
        ch["ps"] = [_dot_nt(a, b) for a, b in zip(ch["qs"], ch["ks"])]
    for ch in chains:
        msk = ch["consts"][3]
        attn = None
        for lv, p in enumerate(ch["ps"]):
            term = p * msk[lv * c:(lv + 1) * c]
            attn = term if attn is None else attn + term
        ch["attn"] = attn.astype(BF16)
    row = lax.broadcasted_iota(jnp.int32, (2 * SUBLANE, 1), 0)
    for ch in chains:
        vb = ch["v"].astype(BF16)
        ch["o"] = jnp.dot(ch["attn"], vb, preferred_element_type=F32)
        dec = ch["dec"]
        hi = dec.astype(BF16).astype(F32)
        mid = (dec - hi).astype(BF16).astype(F32)
        extra = jnp.where(row == 0, hi, jnp.where(row == 1, mid, 0.0)).astype(BF16)
        kt_t = _dot_nt(eye, jnp.concatenate([ch["kt"], extra], axis=0))
        ch["dec_col"] = kt_t[:, c:c + 1] + kt_t[:, c + 1:c + 2]
        ch["upd"] = jnp.dot(kt_t[:, 0:c].astype(BF16), vb, preferred_element_type=F32)


def _dot_split2_l(m, x):
    hi = x.astype(BF16)
    mid = (x - hi.astype(F32)).astype(BF16)
    d = lambda p: jnp.dot(m, p, preferred_element_type=F32)
    return d(hi) + d(mid)


def _gla_kernel(q_ref, k_ref, v_ref, a_ref, w2_ref, b2_ref, s0_ref,
                wf_ref, nqf_ref, nkf_ref, mf_ref, wb_ref, nqb_ref, nkb_ref, mb_ref, eye_ref,
                o_ref, s_ref, gaf, gab, sf, sb, *, t):
    c = GLA_CHUNK
    un = GLA_UNROLL
    nc = t // c
    pb = min(t, 512)

    def prep(i, carry):
        r = pl.multiple_of(i * pb, pb)
        a = a_ref[0, pl.ds(r, pb), :]
        for z, ga in ((0, gaf), (1, gab)):
            logit = _dot_f32x3(a, w2_ref[z]) + b2_ref[z]
            ga[pl.ds(r, pb), :] = _log_sigmoid(logit) * (LOG2E / GLA_GATE_NORM)
        o_ref[0, pl.ds(r, pb), :] = jnp.zeros((pb, GLA_HV), F32)
        return carry

    lax.fori_loop(0, t // pb, prep, 0)
    sf[...] = s0_ref[0, 0, 0]
    sb[...] = s0_ref[0, 0, 1]
    scale = GLA_HK ** -0.5
    fconst = (wf_ref, nqf_ref, nkf_ref, mf_ref)
    bconst = (wb_ref, nqb_ref, nkb_ref, mb_ref)

    def body(ci, carry):
        chains = []
        for fwd, ga, consts in ((True, gaf, fconst), (False, gab, bconst)):
            cvals = tuple(x[...] for x in consts)
            for u in range(un):
                idx = ci * un + u
                r = pl.multiple_of((idx if fwd else nc - 1 - idx) * c, c)
                chains.append(dict(r=r, fwd=fwd, consts=cvals, ga=ga[pl.ds(r, c), :],
                                   q=q_ref[0, pl.ds(r, c), :].astype(F32) * scale,
                                   k=k_ref[0, pl.ds(r, c), :].astype(F32), v=v_ref[0, pl.ds(r, c), :]))
        _gla_group(chains, eye_ref[...])
        for fwd, st_ref in ((True, sf), (False, sb)):
            st = st_ref[...]
            for ch in chains:
                if ch["fwd"] != fwd:
                    continue
                o = ch["o"] + jnp.dot(ch["qt"], st.astype(BF16), preferred_element_type=F32)
                o_ref[0, pl.ds(ch["r"], c), :] += o
                st = st * ch["dec_col"] + ch["upd"]
            st_ref[...] = st
        return carry

    lax.fori_loop(0, nc // un, body, 0)
    s_ref[0, 0, 0] = sf[...]
    s_ref[0, 0, 1] = sb[...]


def _gla_scan(pmain, pa, w2p, b2p, s0):
    b, t, _ = pmain.shape
    h = GLA_HEADS
    assert t % (GLA_CHUNK * GLA_UNROLL) == 0
    consts = _gla_consts(True) + _gla_consts(False) + (jnp.asarray(np.eye(GLA_HK), BF16),)
    kblk = GLA_DK // GLA_HK
    vblk = (2 * GLA_DK) // GLA_HV
    cspec = lambda a: pl.BlockSpec(a.shape, lambda i, j: (0,) * a.ndim)
    return pl.pallas_call(
        functools.partial(_gla_kernel, t=t),
        grid=(b, h),
        in_specs=[pl.BlockSpec((1, t, GLA_HK), lambda i, j: (i, 0, j)),
                  pl.BlockSpec((1, t, GLA_HK), lambda i, j: (i, 0, kblk + j)),
                  pl.BlockSpec((1, t, GLA_HV), lambda i, j: (i, 0, vblk + j)),
                  pl.BlockSpec((1, t, LANE), lambda i, j: (i, 0, 0)),
                  pl.BlockSpec((2, LANE, GLA_HK), lambda i, j: (0, 0, j)),
                  pl.BlockSpec((2, 1, GLA_HK), lambda i, j: (0, 0, j)),
                  pl.BlockSpec((1, 1, 2, GLA_HK, GLA_HV), lambda i, j: (i, j, 0, 0, 0))]
                 + [cspec(a) for a in consts],
        out_specs=[pl.BlockSpec((1, t, GLA_HV), lambda i, j: (i, 0, j)),
                   pl.BlockSpec((1, 1, 2, GLA_HK, GLA_HV), lambda i, j: (i, j, 0, 0, 0))],
        out_shape=[jax.ShapeDtypeStruct((b, t, GLA_DV), F32),
                   jax.ShapeDtypeStruct((b, h, 2, GLA_HK, GLA_HV), F32)],
        scratch_shapes=[pltpu.VMEM((t, GLA_HK), F32), pltpu.VMEM((t, GLA_HK), F32),
                        pltpu.VMEM((GLA_HK, GLA_HV), F32), pltpu.VMEM((GLA_HK, GLA_HV), F32)],
        compiler_params=_cparams(("arbitrary", "arbitrary")),
        name="gla_scan",
    )(pmain, pmain, pmain, pa, w2p, b2p, s0, *consts)


def _ssd_consts():
    c = SSD_CHUNK
    i = np.arange(c)
    tril = (i[:, None] >= i[None, :]).astype(np.float32)
    triu = (i[:, None] <= i[None, :]).astype(np.float32)
    eye = np.eye(SSD_STATE, dtype=np.float32)
    sel16 = np.eye(16, LANE, dtype=np.float32)
    e4 = np.zeros((2, LANE, SSD_GW), np.float32)
    for z in range(2):
        for r in range(SSD_REP):
            e4[z, z * SSD_REP + r, r * SSD_HEADDIM:(r + 1) * SSD_HEADDIM] = 1.0
    lm = np.zeros((SSD_REP * c, SSD_GW), np.float32)
    for r in range(SSD_REP):
        lm[r * c:(r + 1) * c, r * SSD_HEADDIM:(r + 1) * SSD_HEADDIM] = 1.0
    return (jnp.asarray(tril, BF16), jnp.asarray(triu, BF16), jnp.asarray(tril), jnp.asarray(triu),
            jnp.asarray(eye, BF16), jnp.asarray(sel16, BF16), jnp.asarray(e4, BF16), jnp.asarray(lm))


def _conv_block(in_ref, w, bias, i, nblk, rows, t, taps):
    halo = 2 * SUBLANE
    r = pl.multiple_of(i * rows, rows)
    cur = in_ref[0, pl.ds(r, rows), :].astype(F32)
    rp = pl.multiple_of(jnp.maximum(r - halo, 0), halo)
    rn = pl.multiple_of(jnp.minimum(r + rows, t - halo), halo)
    prev = jnp.where(i > 0, in_ref[0, pl.ds(rp, halo), :].astype(F32), 0.0)
    nxt = jnp.where(i < nblk - 1, in_ref[0, pl.ds(rn, halo), :].astype(F32), 0.0)
    ext = jnp.concatenate([prev, cur, nxt], axis=0)
    half = taps // 2
    acc = bias
    for j in range(taps):
        off = halo - half + j
        acc = acc + w[j:j + 1] * ext[off:off + rows]
    return r, acc


def _ssd_group(chains, eye, sel16, lm):
    c = SSD_CHUNK
    for ch in chains:
        ch["cum"] = _dot_exact_l(ch["tri"], ch["das"])
        ch["dt_e"] = _dot_split2_r(ch["ds"], ch["e4z"])
        ch["ccb"], bcb = ch["cc"].astype(BF16), ch["bc"].astype(BF16)
        ch["cb"] = _dot_nt(ch["ccb"], bcb)
        ch["bc_t"] = _dot_nt(eye, bcb).astype(BF16)
    for ch in chains:
        ch["cum_t"] = _dot_exact_nt(sel16, ch["cum"])
        ch["cum_e"] = _dot_split2_r(ch["cum"], ch["e4z"])
    for ch in chains:
        cum, cum_t, cum_e = ch["cum"], ch["cum_t"], ch["cum_e"]
        tot_e = cum_e[c - 1:c] if ch["fwd"] else cum_e[0:1]
        xdt = ch["xg"] * ch["dt_e"]
        cb = ch["cb"] * ch["mask"]
        ms = []
        for r in range(SSD_REP):
            q = ch["z"] * SSD_REP + r
            seg = cum[:, q:q + 1] - cum_t[q:q + 1, :]
            ms.append((cb * jnp.exp2(jnp.minimum(seg, 0.0))).astype(BF16))
        ch["mcat"] = jnp.concatenate(ms, axis=1)
        ch["xbd"] = (jnp.concatenate([xdt] * SSD_REP, axis=0) * lm).astype(BF16)
        ch["w"] = (xdt * jnp.exp2(tot_e - cum_e)).astype(BF16)
        ch["dece"] = jnp.exp2(cum_e)
        ch["dec"] = jnp.exp2(tot_e)
    for ch in chains:
        ch["y"] = jnp.dot(ch["mcat"], ch["xbd"], preferred_element_type=F32)
        ch["upd"] = jnp.dot(ch["bc_t"], ch["w"], preferred_element_type=F32)


def _ssd_kernel(x_ref, bm_ref, cm_ref, dt_ref, wx_ref, wb_ref, wc_ref, bx_ref, bb_ref, bcb_ref,
                dtb_ref, alog_ref, dsk_ref, s0_ref,
                tl_ref, tu_ref, ml_ref, mu_ref, eye_ref, sel16_ref, e4_ref, lm_ref,
                y_ref, s_ref, xc, bcs, ccs, dsel, dasel, sf, sb, *, t):
    c = SSD_CHUNK
    nc = t // c
    un = min(SSD_UNROLL, nc)
    pb = c
    nblk = t // pb

    shift = (LANE - 2 * SSD_REP * pl.program_id(1)) % LANE

    def prep(i):
        for in_ref, w_ref, b_ref, out in ((x_ref, wx_ref, bx_ref, xc), (bm_ref, wb_ref, bb_ref, bcs),
                                          (cm_ref, wc_ref, bcb_ref, ccs)):
            r, acc = _conv_block(in_ref, w_ref[...], b_ref[...], i, nblk, pb, t, SSD_CONV)
            val = _silu(acc)
            out[pl.ds(r, pb), :] = val
            if out is xc:
                y_ref[0, pl.ds(r, pb), :] = val * dsk_ref[...]
        r = pl.multiple_of(i * pb, pb)
        dtp = _softplus(dt_ref[0, pl.ds(r, pb), :] + dtb_ref[...])
        a2 = jnp.exp(alog_ref[...]) * (-LOG2E)
        dsel[pl.ds(r, pb), :] = pltpu.roll(dtp, shift, 1)
        dasel[pl.ds(r, pb), :] = pltpu.roll(dtp * a2, shift, 1)

    for blk in sorted(set(range(un)) | set(nc - 1 - u for u in range(un))):
        prep(jnp.int32(blk))
    sf[...] = s0_ref[0, 0, 0]
    sb[...] = s0_ref[0, 0, 1]
    n_ahead = max(nc // (2 * un) - 1, 0)

    def body(ci, carry, ahead):
        chains = []
        for fwd, z, tri_ref, mk_ref in ((True, 0, tl_ref, ml_ref), (False, 1, tu_ref, mu_ref)):
            tri, mask, e4z = tri_ref[...], mk_ref[...], e4_ref[z]
            for u in range(un):
                idx = ci * un + u
                r = pl.multiple_of((idx if fwd else nc - 1 - idx) * c, c)
                chains.append(dict(r=r, fwd=fwd, z=z, tri=tri, mask=mask, e4z=e4z,
                                   xg=xc[pl.ds(r, c), :], bc=bcs[pl.ds(r, c), :], cc=ccs[pl.ds(r, c), :],
                                   ds=dsel[pl.ds(r, c), :], das=dasel[pl.ds(r, c), :]))
        if ahead:
            for u in range(un):
                nxt = (ci + 1) * un + u
                prep(nxt)
                prep(nc - 1 - nxt)
        _ssd_group(chains, eye_ref[...], sel16_ref[...], lm_ref[...])
        for fwd, st_ref in ((True, sf), (False, sb)):
            st = st_ref[...]
            for ch in chains:
                if ch["fwd"] != fwd:
                    continue
                y = ch["y"] + jnp.dot(ch["ccb"], st.astype(BF16), preferred_element_type=F32) * ch["dece"]
                y_ref[0, pl.ds(ch["r"], c), :] += y
                st = st * ch["dec"] + ch["upd"]
            st_ref[...] = st
        return carry

    lax.fori_loop(0, n_ahead, functools.partial(body, ahead=True), 0)
    lax.fori_loop(n_ahead, nc // un, functools.partial(body, ahead=False), 0)
    s_ref[0, 0, 0] = sf[...]
    s_ref[0, 0, 1] = sb[...]


def _ssd_scan(pmain, pdt, conv_w, conv_b, dtb, alog, dskip, s0):
    b, t, _ = pmain.shape
    g = SSD_GROUPS
    assert t % (SSD_CHUNK * min(SSD_UNROLL, t // SSD_CHUNK)) == 0
    tril, triu, mtril, mtriu, eye, sel16, e4, lm = _ssd_consts()
    xblk = SSD_DI // SSD_GW
    bblk = (2 * SSD_DI) // SSD_STATE
    cblk = bblk + SSD_GN // SSD_STATE
    wbblk = SSD_DI // SSD_STATE
    wcblk = wbblk + SSD_GN // SSD_STATE
    cspec = lambda a: pl.BlockSpec(a.shape, lambda i, j: (0,) * a.ndim)
    return pl.pallas_call(
        functools.partial(_ssd_kernel, t=t),
        grid=(b, g),
        in_specs=[pl.BlockSpec((1, t, SSD_GW), lambda i, j: (i, 0, xblk + j)),
                  pl.BlockSpec((1, t, SSD_STATE), lambda i, j: (i, 0, bblk + j)),
                  pl.BlockSpec((1, t, SSD_STATE), lambda i, j: (i, 0, cblk + j)),
                  pl.BlockSpec((1, t, LANE), lambda i, j: (i, 0, 0)),
                  pl.BlockSpec((SSD_CONV, SSD_GW), lambda i, j: (0, j)),
                  pl.BlockSpec((SSD_CONV, SSD_STATE), lambda i, j: (0, wbblk + j)),
                  pl.BlockSpec((SSD_CONV, SSD_STATE), lambda i, j: (0, wcblk + j)),
                  pl.BlockSpec((1, SSD_GW), lambda i, j: (0, j)),
                  pl.BlockSpec((1, SSD_STATE), lambda i, j: (0, wbblk + j)),
                  pl.BlockSpec((1, SSD_STATE), lambda i, j: (0, wcblk + j)),
                  pl.BlockSpec((1, LANE), lambda i, j: (0, 0)),
                  pl.BlockSpec((1, LANE), lambda i, j: (0, 0)),
                  pl.BlockSpec((1, SSD_GW), lambda i, j: (0, j)),
                  pl.BlockSpec((1, 1, 2, SSD_STATE, SSD_GW), lambda i, j: (i, j, 0, 0, 0)),
                  cspec(tril), cspec(triu), cspec(mtril), cspec(mtriu), cspec(eye), cspec(sel16),
                  cspec(e4), cspec(lm)],
        out_specs=[pl.BlockSpec((1, t, SSD_GW), lambda i, j: (i, 0, j)),
                   pl.BlockSpec((1, 1, 2, SSD_STATE, SSD_GW), lambda i, j: (i, j, 0, 0, 0))],
        out_shape=[jax.ShapeDtypeStruct((b, t, SSD_DI), F32),
                   jax.ShapeDtypeStruct((b, g, 2, SSD_STATE, SSD_GW), F32)],
        scratch_shapes=[pltpu.VMEM((t, SSD_GW), F32), pltpu.VMEM((t, SSD_STATE), F32),
                        pltpu.VMEM((t, SSD_STATE), F32), pltpu.VMEM((t, LANE), F32),
                        pltpu.VMEM((t, LANE), F32),
                        pltpu.VMEM((SSD_STATE, SSD_GW), F32), pltpu.VMEM((SSD_STATE, SSD_GW), F32)],
        compiler_params=_cparams(("arbitrary", "arbitrary")),
        name="ssd_scan",
    )(pmain, pmain, pmain, pdt, conv_w, conv_w, conv_w, conv_b, conv_b, conv_b,
      dtb, alog, dskip, s0, tril, triu, mtril, mtriu, eye, sel16, e4, lm)


def _conv3_kernel(p_ref, w_ref, b_ref, *out_refs, t):
    pb = min(t, 256)
    nblk = t // pb

    def blk(i, carry):
        r, acc = _conv_block(p_ref, w_ref[...], b_ref[...], i, nblk, pb, t, HY_SHORT)
        out_refs[0][0, pl.ds(r, pb), :] = acc
        if len(out_refs) > 1:
            out_refs[1][0, pl.ds(r, pb), :] = acc.astype(BF16)
        return carry

    lax.fori_loop(0, nblk, blk, 0)


def _conv3(p, w, bias, col0, ncols, with_bf16):
    b, t, _ = p.shape
    tn = 256
    off = col0 // tn
    out_shape = [jax.ShapeDtypeStruct((b, t, ncols), F32)]
    out_specs = [pl.BlockSpec((1, t, tn), lambda i, j: (i, 0, j))]
    if with_bf16:
        out_shape.append(jax.ShapeDtypeStruct((b, t, ncols), BF16))
        out_specs.append(pl.BlockSpec((1, t, tn), lambda i, j: (i, 0, j)))
    return pl.pallas_call(
        functools.partial(_conv3_kernel, t=t),
        grid=(b, ncols // tn),
        in_specs=[pl.BlockSpec((1, t, tn), lambda i, j: (i, 0, off + j)),
                  pl.BlockSpec((HY_SHORT, tn), lambda i, j: (0, off + j)),
                  pl.BlockSpec((1, tn), lambda i, j: (0, off + j))],
        out_specs=out_specs,
        out_shape=out_shape,
        compiler_params=_cparams(("arbitrary", "arbitrary")),
        name="hy_conv3",
    )(p, w, bias)


HY_TIME_LANE = HY_FW


def _hy_pos_emb(l, radix):
    bands = (HY_EMB - 1) // 2
    pos = (radix * np.arange(l // radix)[None, :] + np.arange(radix)[:, None]).reshape(-1).astype(np.float64)
    t = (pos / (l - 1))[:, None]
    w = 2 * math.pi * pos[:, None] / l
    ang = np.linspace(1e-4, bands - 1, bands)[None, :] * w
    z = np.concatenate([t, np.cos(ang), -np.sin(ang)], axis=-1)
    zp = np.zeros((l, LANE), np.float32)
    zp[:, :HY_EMB] = z
    zp[:, HY_TIME_LANE] = t[:, 0]
    return zp


def _filt_kernel(z_ref, w1_ref, b1_ref, w2_ref, b2_ref, w3_ref, b3_ref, w4_ref, fr_ref, dl_ref, o_ref):
    fr = fr_ref[...]
    z = z_ref[...]
    h = jnp.sin(fr * (_dot_f32x3(z, w1_ref[...]) + b1_ref[...]))
    h = jnp.sin(fr * (_dot_f32x3(h, w2_ref[...]) + b2_ref[...]))
    h = jnp.sin(fr * (_dot_f32x3(h, w3_ref[...]) + b3_ref[...]))
    hh = _dot_f32x3(h, w4_ref[...])
    tt = z[:, HY_TIME_LANE:HY_TIME_LANE + 1]
    o_ref[...] = hh * jnp.exp(-tt * dl_ref[...])


def _hy_filters(l, radix, w1, b1, w2, b2, w3, b3, w4, freq):
    d = D_MODEL
    tl = min(l, 512)
    pad2 = lambda a: jnp.zeros((LANE, LANE), F32).at[:a.shape[0], :a.shape[1]].set(a)
    pad1 = lambda a: jnp.zeros((1, LANE), F32).at[0, :a.shape[0]].set(a)
    w4p = jnp.zeros((LANE, HY_ORDER * 2 * d), F32).at[:HY_FW].set(w4)
    deltas = np.abs(np.linspace(math.log(HY_DECAY_TARGET) / HY_FAST_DECAY,
                                math.log(HY_DECAY_TARGET) / HY_SLOW_DECAY, d))
    dl = jnp.asarray(np.tile(deltas, HY_ORDER * 2)[None, :], F32)
    full = lambda a: pl.BlockSpec(a.shape, lambda i, j: (0, 0))
    small = [pad2(w1), pad1(b1), pad2(w2), pad1(b2), pad2(w3), pad1(b3)]
    return pl.pallas_call(
        _filt_kernel,
        grid=(l // tl, HY_ORDER * 2),
        in_specs=[pl.BlockSpec((tl, LANE), lambda i, j: (i, 0))] + [full(a) for a in small]
                 + [pl.BlockSpec((LANE, d), lambda i, j: (0, j)), pl.BlockSpec((1, LANE), lambda i, j: (0, 0)),
                    pl.BlockSpec((1, d), lambda i, j: (0, j))],
        out_specs=pl.BlockSpec((tl, d), lambda i, j: (i, j)),
        out_shape=jax.ShapeDtypeStruct((l, HY_ORDER * 2 * d), F32),
        compiler_params=_cparams(("arbitrary", "arbitrary")),
        name="hy_filt",
    )(jnp.asarray(_hy_pos_emb(l, radix)), *small, w4p, pad1(freq), dl)


def _dft_tables_np(l):
    n = 2 * l
    k = np.arange(l)
    ang = 2 * math.pi * ((k[:, None] * k[None, :]) % n) / n
    cm = np.cos(ang)
    sm = -np.sin(ang)
    alt = np.where(k % 2 == 0, 1.0, -1.0)
    s_fwd = sm.copy()
    s_fwd[0, :] = alt
    s_inv = sm.copy()
    s_inv[:, 0] = alt
    return cm, s_fwd, s_inv


def _dft_tab_kernel(ca_ref, sa_ref, cb_ref, sb_ref, c_ref, sf_ref, si_ref, *, rb):
    a = pl.program_id(0)
    ca, sa = ca_ref[0], sa_ref[0]
    cb, sb = cb_ref[...], sb_ref[...]
    cm = ca * cb - sa * sb
    sm = -(sa * cb + ca * sb)
    l = cm.shape[1]
    row = lax.broadcasted_iota(jnp.int32, (rb, l), 0)
    col = lax.broadcasted_iota(jnp.int32, (rb, l), 1)
    alt_col = jnp.where(col % 2 == 0, 1.0, -1.0)
    alt_row = jnp.where(row % 2 == 0, 1.0, -1.0)
    c_ref[...] = cm.astype(BF16)
    sf_ref[...] = jnp.where(jnp.logical_and(row == 0, a == 0), alt_col, sm).astype(BF16)
    si_ref[...] = jnp.where(col == 0, alt_row, sm).astype(BF16)


def _dft_tables(l):
    if l <= 512:
        return tuple(jnp.asarray(m, BF16) for m in _dft_tables_np(l))
    rb = 64
    na = l // rb
    n = 2 * l
    nn = np.arange(l)
    aa = np.arange(na)
    bb = np.arange(rb)
    ang_a = 2 * math.pi * ((aa[:, None] * rb * nn[None, :]) % n) / n
    ang_b = 2 * math.pi * ((bb[:, None] * nn[None, :]) % n) / n
    ca = jnp.asarray(np.cos(ang_a)[:, None, :], F32)
    sa = jnp.asarray(np.sin(ang_a)[:, None, :], F32)
    cb = jnp.asarray(np.cos(ang_b), F32)
    sb = jnp.asarray(np.sin(ang_b), F32)
    rowspec = pl.BlockSpec((1, 1, l), lambda a: (a, 0, 0))
    tabspec = pl.BlockSpec((rb, l), lambda a: (0, 0))
    outspec = pl.BlockSpec((rb, l), lambda a: (a, 0))
    return tuple(pl.pallas_call(
        functools.partial(_dft_tab_kernel, rb=rb),
        grid=(na,),
        in_specs=[rowspec, rowspec, tabspec, tabspec],
        out_specs=[outspec] * 3,
        out_shape=[jax.ShapeDtypeStruct((l, l), BF16)] * 3,
        compiler_params=_cparams(("arbitrary",)),
        name="dft_tables",
    )(ca, sa, cb, sb))


def _spec_filt_kernel(c_ref, s_ref, a_ref, b_ref, h_ref, accr, acci, accn, *, l):
    kk = pl.program_id(2)

    @pl.when(kk == 0)
    def _():
        accr[...] = jnp.zeros_like(accr)
        acci[...] = jnp.zeros_like(acci)
        accn[...] = jnp.zeros_like(accn)

    a, bw = a_ref[...], b_ref[...]
    sm = (a + bw).astype(BF16)
    df = (a - bw).astype(BF16)
    accr[...] += jnp.dot(c_ref[...], sm, preferred_element_type=F32)
    acci[...] += jnp.dot(s_ref[...], df, preferred_element_type=F32)
    accn[...] += jnp.dot(s_ref[...], sm, preferred_element_type=F32)

    @pl.when(kk == pl.num_programs(2) - 1)
    def _():
        tm = accr.shape[0]
        row0 = (lax.broadcasted_iota(jnp.int32, (tm, 1), 0) + pl.program_id(0) * tm) == 0
        scale = jnp.where(row0, 0.5 / l, 1.0 / l)
        h_ref[0] = accr[...] * scale
        h_ref[1] = jnp.where(row0, accn[...], acci[...]) * scale


def _spec_filt(cm, s_fwd, fa, fb, l):
    ncol = fa.shape[1]
    tm, tn, tk = min(l, 512), 512, min(l, 1024)
    return pl.pallas_call(
        functools.partial(_spec_filt_kernel, l=l),
        grid=(l // tm, ncol // tn, l // tk),
        in_specs=[pl.BlockSpec((tm, tk), lambda m, j, k: (m, k)),
                  pl.BlockSpec((tm, tk), lambda m, j, k: (m, k)),
                  pl.BlockSpec((tk, tn), lambda m, j, k: (k, j)),
                  pl.BlockSpec((tk, tn), lambda m, j, k: (k, j))],
        out_specs=pl.BlockSpec((2, tm, tn), lambda m, j, k: (0, m, j)),
        out_shape=jax.ShapeDtypeStruct((2, l, ncol), F32),
        scratch_shapes=[pltpu.VMEM((tm, tn), F32)] * 3,
        compiler_params=_cparams(("arbitrary", "arbitrary", "arbitrary")),
        name="hy_spec_filt",
    )(cm, s_fwd, fa, fb)


def _poly_filters(hh, lp, radix):
    d = D_MODEL
    zero = jnp.zeros((1, d), F32)
    cols_a, cols_b = [], []
    for o in range(HY_ORDER):
        fwd = hh[:, (2 * o) * d:(2 * o + 1) * d]
        bwd = hh[:, (2 * o + 1) * d:(2 * o + 2) * d]
        fp = lambda q: fwd[q * lp:(q + 1) * lp]
        bp = lambda q: bwd[q * lp:(q + 1) * lp]
        for dd in range(-(radix - 1), radix):
            if dd >= 0:
                a = fp(dd)
            else:
                a = jnp.concatenate([bp(-dd)[0:1], fp(radix + dd)[:-1]], axis=0)
            if dd <= 0:
                bb = jnp.concatenate([zero, bp(-dd)[1:]], axis=0)
            else:
                bb = jnp.concatenate([zero, bp(radix - dd)[:-1]], axis=0)
            cols_a.append(a)
            cols_b.append(bb)
    return jnp.concatenate(cols_a, axis=1), jnp.concatenate(cols_b, axis=1)


def _poly_spec_kernel(c_ref, s_ref, *refs, radix):
    u_refs, g_refs = refs[:radix], refs[radix:2 * radix]
    z_ref, fr, fi = refs[2 * radix:]

    @pl.when(pl.program_id(2) == 0)
    def _():
        for q in range(radix):
            u = u_refs[q][0]
            fr[q] = jnp.dot(c_ref[...], u, preferred_element_type=F32)
            fi[q] = jnp.dot(s_ref[...], u, preferred_element_type=F32)

    tm = fr.shape[1]
    row0 = (lax.broadcasted_iota(jnp.int32, (tm, 1), 0) + pl.program_id(1) * tm) == 0
    zr = zi = None
    for i in range(radix):
        q = radix - 1 - i
        gr, gi = g_refs[i][0], g_refs[i][1]
        xr, xi = fr[q], fi[q]
        tr = xr * gr - jnp.where(row0, 0.0, xi * gi)
        ti = jnp.where(row0, xi * gi, xr * gi + xi * gr)
        zr = tr if zr is None else zr + tr
        zi = ti if zi is None else zi + ti
    z_ref[0] = zr.astype(BF16)
    z_ref[1] = zi.astype(BF16)


def _poly_spec(cm, s_fwd, ub, gspec, order, radix):
    b, lp, _ = ub.shape
    d = D_MODEL
    tm, tn = min(lp, 256), 512
    nd = d // tn
    nsub = 2 * radix - 1
    u_specs = [pl.BlockSpec((1, lp, tn), functools.partial(lambda j, m, r, q: (j // nd, 0, q * nd + j % nd), q=q))
               for q in range(radix)]
    g_specs = [pl.BlockSpec((2, tm, tn),
                            functools.partial(lambda j, m, r, i: (0, m, (order * nsub + r + i) * nd + j % nd), i=i))
               for i in range(radix)]
    return pl.pallas_call(
        functools.partial(_poly_spec_kernel, radix=radix),
        grid=(b * nd, lp // tm, radix),
        in_specs=[pl.BlockSpec((tm, lp), lambda j, m, r: (m, 0)),
                  pl.BlockSpec((tm, lp), lambda j, m, r: (m, 0))] + u_specs + g_specs,
        out_specs=pl.BlockSpec((2, tm, tn), lambda j, m, r: (0, m, r * (b * nd) + j)),
        out_shape=jax.ShapeDtypeStruct((2, lp, radix * b * d), BF16),
        scratch_shapes=[pltpu.VMEM((radix, tm, tn), F32)] * 2,
        compiler_params=_cparams(("arbitrary", "arbitrary", "arbitrary")),
        name="hy_poly_spec",
    )(cm, s_fwd, *([ub] * radix), *([gspec] * radix))


def _poly_inv_kernel(c_ref, s_ref, z_ref, u_ref, g_ref, bias_ref, *out_refs):
    conv = (jnp.dot(c_ref[...], z_ref[0], preferred_element_type=F32)
            + jnp.dot(s_ref[...], z_ref[1], preferred_element_type=F32))
    res = g_ref[0] * (conv + u_ref[0] * bias_ref[...])
    out_refs[0][0] = res
    if len(out_refs) > 1:
        out_refs[1][0] = res.astype(BF16)


def _poly_inv(cm, s_inv, zspec, u, gate, gsel, bias, with_bf16, radix):
    b, lp, _ = u.shape
    d = D_MODEL
    tt, tn = min(lp, 512), 512
    nd = d // tn
    blk = pl.BlockSpec((1, tt, tn), lambda t, j, r: (j // nd, t, r * nd + j % nd))
    out_shape = [jax.ShapeDtypeStruct((b, lp, radix * d), F32)]
    out_specs = [blk]
    if with_bf16:
        out_shape.append(jax.ShapeDtypeStruct((b, lp, radix * d), BF16))
        out_specs.append(blk)
    return pl.pallas_call(
        _poly_inv_kernel,
        grid=(lp // tt, b * nd, radix),
        in_specs=[pl.BlockSpec((tt, lp), lambda t, j, r: (t, 0)),
                  pl.BlockSpec((tt, lp), lambda t, j, r: (t, 0)),
                  pl.BlockSpec((2, lp, tn), lambda t, j, r: (0, 0, r * (b * nd) + j)),
                  blk,
                  pl.BlockSpec((1, tt, tn), lambda t, j, r: (j // nd, t, (2 * r + gsel) * nd + j % nd)),
                  pl.BlockSpec((1, tn), lambda t, j, r: (0, j % nd))],
        out_specs=out_specs,
        out_shape=out_shape,
        compiler_params=_cparams(("arbitrary", "arbitrary", "arbitrary")),
        name="hy_poly_inv",
    )(cm, s_inv, zspec, u, gate, bias)


def _hyena_run(p, conv_w, conv_b, fw, h_bias):
    b, l, _ = p.shape
    d = D_MODEL
    radix = max(1, min(HY_RADIX, l // HY_MIN_PHASE_LEN))
    lp = l // radix
    poly = lambda a: a.reshape(b, lp, radix * a.shape[-1])
    v, vb = _conv3(p, conv_w, conv_b, 0, d, True)
    x12 = poly(_conv3(p, conv_w, conv_b, d, 2 * d, False)[0])
    cm, s_fwd, s_inv = _dft_tables(lp)
    hh = _hy_filters(l, radix, *fw)
    gspec = _spec_filt(cm, s_fwd, *_poly_filters(hh, lp, radix), lp)
    z1 = _poly_spec(cm, s_fwd, poly(vb), gspec, 0, radix)
    z, zb = _poly_inv(cm, s_inv, z1, poly(v), x12, 0, h_bias[0:1], True, radix)
    z2 = _poly_spec(cm, s_fwd, zb, gspec, 1, radix)
    zz = _poly_inv(cm, s_inv, z2, z, x12, 1, h_bias[1:2], False, radix)[0]
    return zz.reshape(b, l, d)


def _snake_kernel(x_ref, j_ref, o_ref):
    jm = j_ref[...]
    for g in range(x_ref.shape[1] // (2 * GRID_W)):
        r0 = g * 2 * GRID_W
        o_ref[0, r0:r0 + GRID_W, :] = x_ref[0, r0:r0 + GRID_W, :]
        o_ref[0, r0 + GRID_W:r0 + 2 * GRID_W, :] = _dot_exact_l(jm, x_ref[0, r0 + GRID_W:r0 + 2 * GRID_W, :])


def _snake(h):
    b, l, ch = h.shape
    tm = 512
    jm = jnp.asarray(np.eye(GRID_W)[::-1].copy(), BF16)
    return pl.pallas_call(
        _snake_kernel,
        grid=(b, l // tm),
        in_specs=[pl.BlockSpec((1, tm, ch), lambda i, j: (i, j, 0)),
                  pl.BlockSpec((GRID_W, GRID_W), lambda i, j: (0, 0))],
        out_specs=pl.BlockSpec((1, tm, ch), lambda i, j: (i, j, 0)),
        out_shape=jax.ShapeDtypeStruct((b, l, ch), F32),
        compiler_params=_cparams(("arbitrary", "arbitrary")),
        name="snake",
    )(h, jm)


def _pad_cols(w, n):
    return jnp.zeros((w.shape[0], n), w.dtype).at[:, :w.shape[1]].set(w)


def kernel(x, c, ctx, c_ctx, ada_w, ada_b, ln_g, ln_b, ffn_w1, ffn_w2, gla_w_in, gla_w_a2, gla_b_a2, gla_norm, gla_w_out, ssd_w_in, ssd_conv_w, ssd_conv_b, ssd_dt_bias, ssd_a_log, ssd_d, ssd_norm, ssd_w_out, hy_w_in, hy_conv_w, hy_conv_b, hy_f_w1, hy_f_b1, hy_f_w2, hy_f_b2, hy_f_w3, hy_f_b3, hy_f_w4, hy_f_freq, hy_bias, hy_w_out):
    bsz, _, d = x.shape
    hl = _snake(x)
    hc = ctx.reshape(1, -1, d)
    cvec = jnp.zeros((16, d), F32).at[:bsz].set(c).at[bsz].set(c_ctx)
    mods = _ada(cvec, ada_w, ada_b).reshape(DEPTH, 16, 6, d)
    mods = jnp.concatenate([mods, jnp.zeros((DEPTH, 16, 2, d), F32)], axis=2)
    per_batch = lambda a: a.reshape(bsz, -1, a.shape[-1])
    like = lambda a, h: a.reshape(h.shape[0], -1, a.shape[-1])

    for i in range(DEPTH):
        kind, j = i % N_MIXERS, i // N_MIXERS
        need_ctx = i < DEPTH - 1
        ml = mods[i, :bsz]
        mc = mods[i, bsz][None]
        g0, b0 = ln_g[i, 0][None], ln_b[i, 0][None]
        g1, b1 = ln_g[i, 1][None], ln_b[i, 1][None]
        w1 = ffn_w1[i].astype(BF16)
        w2 = ffn_w2[i].astype(BF16)
        streams = [(hc, mc, True), (hl, ml, False)]
        if kind == 0:
            w_main = gla_w_in[j][:, :GLA_MAIN].astype(BF16)
            w_a = _pad_cols(gla_w_in[j][:, GLA_MAIN:], LANE).astype(BF16)
            w2p = jnp.zeros((2, LANE, GLA_DK), F32)
            for z in range(2):
                w2p = w2p.at[z, z * GLA_RANK:(z + 1) * GLA_RANK].set(gla_w_a2[j, z])
            b2p = gla_b_a2[j][:, None, :]
            w_out = gla_w_out[j].astype(BF16)
            ng = gla_norm[j][None]
            state = jnp.zeros((bsz, GLA_HEADS, 2, GLA_HK, GLA_HV), F32)
            new = []
            for h, m, is_ctx in streams:
                pmain = _proj(h, m, w_main, 1024, BF16)
                pa = _proj(h, m, w_a, LANE, F32)
                o, st = _gla_scan(per_batch(pmain), per_batch(pa), w2p, b2p, state)
                if is_ctx:
                    state = st
                if is_ctx and not need_ctx:
                    new.append(h)
                    continue
                new.append(_gla_out(like(o, h), pmain, h, m, ng, w_out, g0, b0))
            hc, hl = new
        elif kind == 1:
            perm = np.arange(2 * SSD_HEADS).reshape(2, SSD_GROUPS, SSD_REP).transpose(1, 0, 2).reshape(-1)
            w_main = ssd_w_in[j][:, :SSD_MAIN].astype(BF16)
            w_dt = _pad_cols(ssd_w_in[j][:, SSD_MAIN:][:, perm], LANE).astype(BF16)
            dtb = _pad_cols(ssd_dt_bias[j].reshape(1, -1)[:, perm], LANE)
            alog = _pad_cols(ssd_a_log[j].reshape(1, -1)[:, perm], LANE)
            dskip = jnp.repeat(ssd_d[j], SSD_HEADDIM)[None]
            cw = ssd_conv_w[j]
            cbias = ssd_conv_b[j][None]
            w_out = ssd_w_out[j].astype(BF16)
            ng = ssd_norm[j][None]
            state = jnp.zeros((bsz, SSD_GROUPS, 2, SSD_STATE, SSD_GW), F32)
            new = []
            for h, m, is_ctx in streams:
                pmain = _proj(h, m, w_main, 1024, BF16)
                pdt = _proj(h, m, w_dt, LANE, F32)
                y, st = _ssd_scan(per_batch(pmain), per_batch(pdt), cw, cbias, dtb, alog, dskip, state)
                if is_ctx:
                    state = st
                if is_ctx and not need_ctx:
                    new.append(h)
                    continue
                new.append(_ssd_out(like(y, h), pmain, h, m, ng, w_out, g0, b0))
            hc, hl = new
        else:
            w_in = hy_w_in[j].astype(BF16)
            w_out = hy_w_out[j].astype(BF16)
            fw = (hy_f_w1[j], hy_f_b1[j], hy_f_w2[j], hy_f_b2[j], hy_f_w3[j], hy_f_b3[j], hy_f_w4[j], hy_f_freq[j])
            new = []
            for h, m, is_ctx in streams:
                if is_ctx and not need_ctx:
                    new.append(h)
                    continue
                p = per_batch(_proj(h, m, w_in, 1024, BF16))
                zz = _hyena_run(p, hy_conv_w[j], hy_conv_b[j][None], fw, hy_bias[j])
                new.append(_hy_out(like(zz, h), h, m, w_out, g0, b0))
            hc, hl = new
        hl = _ffn(hl, ml, w1, w2, g1, b1)
        if need_ctx:
            hc = _ffn(hc, mc, w1, w2, g1, b1)
    return _snake(hl)
```

```python
import functools
import math

import numpy as np
import jax
import jax.numpy as jnp
from jax import lax
from jax.experimental import pallas as pl
from jax.experimental.pallas import tpu as pltpu

F32 = jnp.float32
BF16 = jnp.bfloat16

D_MODEL = 1024
DEPTH = 4
GRID_W = 64
N_MIXERS = 3
D_FF = 4 * D_MODEL
DEEPNORM_ALPHA = (2 * DEPTH) ** 0.25
LN_EPS = 1e-5
RMS_EPS = 1e-6

GLA_HEADS = 4
GLA_DK = D_MODEL // 2
GLA_DV = D_MODEL
GLA_HK = GLA_DK // GLA_HEADS
GLA_HV = GLA_DV // GLA_HEADS
GLA_RANK = 16
GLA_GATE_NORM = 16.0
GLA_CHUNK = 64
GLA_UNROLL = 4
GLA_NEG = -1e30
LOG2E = math.log2(math.e)
GLA_MAIN = 2 * GLA_DK + 2 * GLA_DV

SSD_DI = 2 * D_MODEL
SSD_HEADDIM = 64
SSD_HEADS = SSD_DI // SSD_HEADDIM
SSD_GROUPS = 8
SSD_REP = SSD_HEADS // SSD_GROUPS
SSD_STATE = 128
SSD_CONV = 5
SSD_CHUNK = 128
SSD_UNROLL = 2
SSD_GN = SSD_GROUPS * SSD_STATE
SSD_CONV_DIM = SSD_DI + 2 * SSD_GN
SSD_MAIN = SSD_DI + SSD_CONV_DIM
SSD_GW = SSD_REP * SSD_HEADDIM

HY_ORDER = 2
HY_SHORT = 3
HY_EMB = 33
HY_FW = 64
HY_DECAY_TARGET = 1e-2
HY_FAST_DECAY = 0.3
HY_SLOW_DECAY = 1.5
HY_RADIX = 4
HY_MIN_BLOCK_LEN = 1024

LANE = 128
SUBLANE = 8
VMEM_LIMIT = 56 * 1024 * 1024


def _cparams(sem):
    return pltpu.CompilerParams(dimension_semantics=sem, vmem_limit_bytes=VMEM_LIMIT)


def _dot(a, b):
    return jnp.dot(a.astype(BF16), b.astype(BF16), preferred_element_type=F32)


def _dot_nt(a, b):
    return lax.dot_general(a.astype(BF16), b.astype(BF16), (((1,), (1,)), ((), ())),
                           preferred_element_type=F32)


def _split3(x):
    hi = x.astype(BF16)
    r1 = x - hi.astype(F32)
    mid = r1.astype(BF16)
    lo = (r1 - mid.astype(F32)).astype(BF16)
    return hi, mid, lo


def _dot_exact_l(m01, x):
    hi, mid, lo = _split3(x)
    d = lambda p: jnp.dot(m01, p, preferred_element_type=F32)
    return d(hi) + d(mid) + d(lo)


def _dot_exact_r(x, m01):
    hi, mid, lo = _split3(x)
    d = lambda p: jnp.dot(p, m01, preferred_element_type=F32)
    return d(hi) + d(mid) + d(lo)


def _dot_split2_r(x, m01):
    hi = x.astype(BF16)
    mid = (x - hi.astype(F32)).astype(BF16)
    d = lambda p: jnp.dot(p, m01, preferred_element_type=F32)
    return d(hi) + d(mid)


def _dot_exact_nt(m01, x):
    hi, mid, lo = _split3(x)
    d = lambda p: lax.dot_general(m01, p, (((1,), (1,)), ((), ())), preferred_element_type=F32)
    return d(hi) + d(mid) + d(lo)


def _dot_f32(a, b):
    ah, am, al = _split3(a)
    bh, bm, bl = _split3(b)
    d = lambda p, q: jnp.dot(p, q, preferred_element_type=F32)
    return (d(ah, bh) + (d(ah, bm) + d(am, bh)) + (d(ah, bl) + d(al, bh) + d(am, bm)))


def _dot_f32x3(a, b):
    ah, am, _ = _split3(a)
    bh, bm, _ = _split3(b)
    d = lambda p, q: jnp.dot(p, q, preferred_element_type=F32)
    return d(ah, bh) + (d(ah, bm) + d(am, bh))


def _silu(x):
    return x * jax.nn.sigmoid(x)


def _softplus(x):
    return jnp.maximum(x, 0.0) + jnp.log1p(jnp.exp(-jnp.abs(x)))


def _log_sigmoid(x):
    return -_softplus(-x)


def _layer_norm(h, g, b):
    mu = jnp.mean(h, -1, keepdims=True)
    d = h - mu
    var = jnp.mean(d * d, -1, keepdims=True)
    return d * lax.rsqrt(var + LN_EPS) * g + b


def _res_ln(x, gate, y, g, b):
    return _layer_norm(DEEPNORM_ALPHA * x + gate * y, g, b)


def _ada_kernel(c_ref, w_ref, b_ref, o_ref):
    s = _silu(c_ref[...])
    o_ref[0] = _dot_f32(s, w_ref[0]) + b_ref[0]


def _ada(cvec, ada_w, ada_b):
    tn = 1536
    n = 6 * D_MODEL
    return pl.pallas_call(
        _ada_kernel,
        grid=(DEPTH, n // tn),
        in_specs=[pl.BlockSpec((16, D_MODEL), lambda i, j: (0, 0)),
                  pl.BlockSpec((1, D_MODEL, tn), lambda i, j: (i, 0, j)),
                  pl.BlockSpec((1, 1, tn), lambda i, j: (i, 0, j))],
        out_specs=pl.BlockSpec((1, 16, tn), lambda i, j: (i, 0, j)),
        out_shape=jax.ShapeDtypeStruct((DEPTH, 16, n), F32),
        compiler_params=_cparams(("arbitrary", "arbitrary")),
        name="ada",
    )(cvec, ada_w, ada_b.reshape(DEPTH, 1, n))


def _proj_kernel(x_ref, m_ref, w_ref, o_ref, u_scr):
    @pl.when(pl.program_id(2) == 0)
    def _():
        m = m_ref[0]
        u_scr[...] = (x_ref[0] * (1.0 + m[1:2]) + m[0:1]).astype(BF16)

    o_ref[0] = jnp.dot(u_scr[...], w_ref[...], preferred_element_type=F32).astype(o_ref.dtype)


def _proj(x, mods, w, tn, out_dtype):
    b, t, d = x.shape
    n = w.shape[1]
    tm = min(t, 1024)
    return pl.pallas_call(
        _proj_kernel,
        grid=(b, t // tm, n // tn),
        in_specs=[pl.BlockSpec((1, tm, d), lambda i, j, k: (i, j, 0)),
                  pl.BlockSpec((1, 8, d), lambda i, j, k: (i, 0, 0)),
                  pl.BlockSpec((d, tn), lambda i, j, k: (0, k))],
        out_specs=pl.BlockSpec((1, tm, tn), lambda i, j, k: (i, j, k)),
        out_shape=jax.ShapeDtypeStruct((b, t, n), out_dtype),
        scratch_shapes=[pltpu.VMEM((tm, d), BF16)],
        compiler_params=_cparams(("arbitrary", "arbitrary", "arbitrary")),
        name="proj",
    )(x, mods, w)


def _ffn_kernel(x_ref, m_ref, w1_ref, w2_ref, g_ref, b_ref, o_ref, acc_ref, u_scr):
    f = pl.program_id(2)

    @pl.when(f == 0)
    def _():
        m = m_ref[0]
        acc_ref[...] = jnp.zeros_like(acc_ref)
        u_scr[...] = (x_ref[0] * (1.0 + m[4:5]) + m[3:4]).astype(BF16)

    a = jnp.square(jnp.maximum(jnp.dot(u_scr[...], w1_ref[...], preferred_element_type=F32), 0.0))
    acc_ref[...] += _dot(a, w2_ref[...])

    @pl.when(f == pl.num_programs(2) - 1)
    def _():
        o_ref[0] = _res_ln(x_ref[0], m_ref[0][5:6], acc_ref[...], g_ref[...], b_ref[...])


def _ffn(x, mods, w1, w2, g, bb):
    b, t, d = x.shape
    tm = min(t, 1024)
    tf = 512
    return pl.pallas_call(
        _ffn_kernel,
        grid=(b, t // tm, D_FF // tf),
        in_specs=[pl.BlockSpec((1, tm, d), lambda i, j, k: (i, j, 0)),
                  pl.BlockSpec((1, 8, d), lambda i, j, k: (i, 0, 0)),
                  pl.BlockSpec((d, tf), lambda i, j, k: (0, k)),
                  pl.BlockSpec((tf, d), lambda i, j, k: (k, 0)),
                  pl.BlockSpec((1, d), lambda i, j, k: (0, 0)),
                  pl.BlockSpec((1, d), lambda i, j, k: (0, 0))],
        out_specs=pl.BlockSpec((1, tm, d), lambda i, j, k: (i, j, 0)),
        out_shape=jax.ShapeDtypeStruct((b, t, d), F32),
        scratch_shapes=[pltpu.VMEM((tm, d), F32), pltpu.VMEM((tm, d), BF16)],
        compiler_params=_cparams(("arbitrary", "arbitrary", "arbitrary")),
        name="ffn",
    )(x, mods, w1, w2, g, bb)


def _gla_out_kernel(o_ref, gate_ref, x_ref, m_ref, ng_ref, w_ref, g_ref, b_ref, out_ref):
    o = o_ref[0]
    ng = ng_ref[...]
    parts = []
    for h in range(GLA_HEADS):
        oh = o[:, h * GLA_HV:(h + 1) * GLA_HV]
        r = lax.rsqrt(jnp.mean(oh * oh, -1, keepdims=True) + RMS_EPS)
        parts.append(oh * r * ng)
    z = jnp.concatenate(parts, axis=-1) * _silu(gate_ref[0].astype(F32))
    y = _dot(z, w_ref[...])
    out_ref[0] = _res_ln(x_ref[0], m_ref[0][2:3], y, g_ref[...], b_ref[...])


def _gla_out(o, pmain, x, mods, ng, w, g, bb):
    b, t, d = x.shape
    tm = min(t, 512)
    gate_blk = (2 * GLA_DK) // GLA_DV + 1
    return pl.pallas_call(
        _gla_out_kernel,
        grid=(b, t // tm),
        in_specs=[pl.BlockSpec((1, tm, GLA_DV), lambda i, j: (i, j, 0)),
                  pl.BlockSpec((1, tm, GLA_DV), lambda i, j: (i, j, gate_blk)),
                  pl.BlockSpec((1, tm, d), lambda i, j: (i, j, 0)),
                  pl.BlockSpec((1, 8, d), lambda i, j: (i, 0, 0)),
                  pl.BlockSpec((1, GLA_HV), lambda i, j: (0, 0)),
                  pl.BlockSpec((GLA_DV, d), lambda i, j: (0, 0)),
                  pl.BlockSpec((1, d), lambda i, j: (0, 0)),
                  pl.BlockSpec((1, d), lambda i, j: (0, 0))],
        out_specs=pl.BlockSpec((1, tm, d), lambda i, j: (i, j, 0)),
        out_shape=jax.ShapeDtypeStruct((b, t, d), F32),
        compiler_params=_cparams(("arbitrary", "arbitrary")),
        name="gla_out",
    )(o, pmain, x, mods, ng, w, g, bb)


def _ssd_out_kernel(y_ref, z_ref, x_ref, m_ref, ng_ref, w_ref, g_ref, b_ref, out_ref):
    yz = y_ref[0] * _silu(z_ref[0].astype(F32))
    r = lax.rsqrt(jnp.mean(yz * yz, -1, keepdims=True) + RMS_EPS)
    y = _dot(yz * r * ng_ref[...], w_ref[...])
    out_ref[0] = _res_ln(x_ref[0], m_ref[0][2:3], y, g_ref[...], b_ref[...])


def _ssd_out(y, pmain, x, mods, ng, w, g, bb):
    b, t, d = x.shape
    tm = min(t, 512)
    return pl.pallas_call(
        _ssd_out_kernel,
        grid=(b, t // tm),
        in_specs=[pl.BlockSpec((1, tm, SSD_DI), lambda i, j: (i, j, 0)),
                  pl.BlockSpec((1, tm, SSD_DI), lambda i, j: (i, j, 0)),
                  pl.BlockSpec((1, tm, d), lambda i, j: (i, j, 0)),
                  pl.BlockSpec((1, 8, d), lambda i, j: (i, 0, 0)),
                  pl.BlockSpec((1, SSD_DI), lambda i, j: (0, 0)),
                  pl.BlockSpec((SSD_DI, d), lambda i, j: (0, 0)),
                  pl.BlockSpec((1, d), lambda i, j: (0, 0)),
                  pl.BlockSpec((1, d), lambda i, j: (0, 0))],
        out_specs=pl.BlockSpec((1, tm, d), lambda i, j: (i, j, 0)),
        out_shape=jax.ShapeDtypeStruct((b, t, d), F32),
        compiler_params=_cparams(("arbitrary", "arbitrary")),
        name="ssd_out",
    )(y, pmain, x, mods, ng, w, g, bb)


def _hy_out_kernel(z_ref, x_ref, m_ref, w_ref, g_ref, b_ref, out_ref):
    y = _dot(z_ref[0], w_ref[...])
    out_ref[0] = _res_ln(x_ref[0], m_ref[0][2:3], y, g_ref[...], b_ref[...])


def _hy_out(z, x, mods, w, g, bb):
    b, t, d = x.shape
    tm = min(t, 512)
    return pl.pallas_call(
        _hy_out_kernel,
        grid=(b, t // tm),
        in_specs=[pl.BlockSpec((1, tm, d), lambda i, j: (i, j, 0)),
                  pl.BlockSpec((1, tm, d), lambda i, j: (i, j, 0)),
                  pl.BlockSpec((1, 8, d), lambda i, j: (i, 0, 0)),
                  pl.BlockSpec((d, d), lambda i, j: (0, 0)),
                  pl.BlockSpec((1, d), lambda i, j: (0, 0)),
                  pl.BlockSpec((1, d), lambda i, j: (0, 0))],
        out_specs=pl.BlockSpec((1, tm, d), lambda i, j: (i, j, 0)),
        out_shape=jax.ShapeDtypeStruct((b, t, d), F32),
        compiler_params=_cparams(("arbitrary", "arbitrary")),
        name="hy_out",
    )(z, x, mods, w, g, bb)


def _gla_consts(fwd):
    c = GLA_CHUNK
    i = np.arange(c)
    tri = ((i[:, None] >= i[None, :]) if fwd else (i[:, None] <= i[None, :])).astype(np.float32)
    halves = [c >> (s + 1) for s in range(int(math.log2(c)))]
    nl = len(halves)
    fine = [lv for lv, half in enumerate(halves) if half < SUBLANE]
    wst = np.zeros(((len(fine) + 1) * c, c), np.float32)
    wst[:c] = tri
    negq = np.zeros((nl * c, GLA_HK), np.float32)
    negk = np.zeros((nl * c, GLA_HK), np.float32)
    msk = np.zeros(((nl + 1) * c, c), np.float32)
    msk[:c] = np.eye(c)
    for lv, half in enumerate(halves):
        blk = i // (2 * half)
        upper = (i % (2 * half)) >= half
        ref = blk * 2 * half + (half - 1 if fwd else half)
        if lv in fine:
            fi = fine.index(lv)
            wst[(fi + 1) * c:(fi + 2) * c] = tri - tri[ref]
        qside = upper if fwd else ~upper
        negq[lv * c:(lv + 1) * c] = np.where(qside, 0.0, GLA_NEG)[:, None]
        negk[lv * c:(lv + 1) * c] = np.where(~qside, 0.0, GLA_NEG)[:, None]
        msk[(lv + 1) * c:(lv + 2) * c] = ((blk[:, None] == blk[None, :]) & qside[:, None] & (~qside)[None, :])
    return (jnp.asarray(wst, BF16), jnp.asarray(negq), jnp.asarray(negk), jnp.asarray(msk))


def _gla_group(chains, eye):
    c = GLA_CHUNK
    for ch in chains:
        wst = ch["consts"][0]
        ch["est"] = _dot_split2_l(wst, ch["ga"])
    for ch in chains:
        _, negq, negk, _ = ch["consts"]
        q, k, est, fwd = ch["q"], ch["k"], ch["est"], ch["fwd"]
        nl = negq.shape[0] // c
        ncoarse = nl + 1 - est.shape[0] // c
        cum = est[0:c]
        tot = cum[c - 1:c] if fwd else cum[0:1]
        ch["qt"] = (q * jnp.exp2(cum)).astype(BF16)
        ch["kt"] = (k * jnp.exp2(tot - cum)).astype(BF16)
        ch["dec"] = jnp.exp2(tot)
        qs, ks = [q.astype(BF16)], [k.astype(BF16)]
        for lv in range(nl):
            if lv < ncoarse:
                size = c >> lv
                parts = []
                for lo in range(0, c, size):
                    ref = lo + size // 2 - (1 if fwd else 0)
                    parts.append(cum[lo:lo + size] - cum[ref:ref + 1])
                e = parts[0] if len(parts) == 1 else jnp.concatenate(parts, axis=0)
            else:
                e = est[(lv - ncoarse + 1) * c:(lv - ncoarse + 2) * c]
            sl = slice(lv * c, (lv + 1) * c)
            qs.append((q * jnp.exp2(e + negq[sl])).astype(BF16))
            ks.append((k * jnp.exp2(negk[sl] - e)).astype(BF16))
        ch["qs"], ch["ks"] = qs, ks
    for ch in chains:
        ch["ps"] = [_dot_nt(a, b) for a, b in zip(ch["qs"], ch["ks"])]
    for ch in chains:
        msk = ch["consts"][3]
        attn = None
        for lv, p in enumerate(ch["ps"]):
            term = p * msk[lv * c:(lv + 1) * c]
            attn = term if attn is None else attn + term
        ch["attn"] = attn.astype(BF16)
    row = lax.broadcasted_iota(jnp.int32, (2 * SUBLANE, 1), 0)
    for ch in chains:
        vb = ch["v"].astype(BF16)
        ch["o"] = jnp.dot(ch["attn"], vb, preferred_element_type=F32)
        dec = ch["dec"]
        hi = dec.astype(BF16).astype(F32)
        mid = (dec - hi).astype(BF16).astype(F32)
        extra = jnp.where(row == 0, hi, jnp.where(row == 1, mid, 0.0)).astype(BF16)
        kt_t = _dot_nt(eye, jnp.concatenate([ch["kt"], extra], axis=0))
        ch["dec_col"] = kt_t[:, c:c + 1] + kt_t[:, c + 1:c + 2]
        ch["upd"] = jnp.dot(kt_t[:, 0:c].astype(BF16), vb, preferred_element_type=F32)


def _dot_split2_l(m, x):
    hi = x.astype(BF16)
    mid = (x - hi.astype(F32)).astype(BF16)
    d = lambda p: jnp.dot(m, p, preferred_element_type=F32)
    return d(hi) + d(mid)


def _gla_kernel(q_ref, k_ref, v_ref, a_ref, w2_ref, b2_ref, s0_ref,
                wf_ref, nqf_ref, nkf_ref, mf_ref, wb_ref, nqb_ref, nkb_ref, mb_ref, eye_ref,
                o_ref, s_ref, gaf, gab, sf, sb, *, t):
    c = GLA_CHUNK
    un = GLA_UNROLL
    nc = t // c
    pb = min(t, 512)

    def prep(i, carry):
        r = pl.multiple_of(i * pb, pb)
        a = a_ref[0, pl.ds(r, pb), :]
        for z, ga in ((0, gaf), (1, gab)):
            logit = _dot_f32x3(a, w2_ref[z]) + b2_ref[z]
            ga[pl.ds(r, pb), :] = _log_sigmoid(logit) * (LOG2E / GLA_GATE_NORM)
        o_ref[0, pl.ds(r, pb), :] = jnp.zeros((pb, GLA_HV), F32)
        return carry

    lax.fori_loop(0, t // pb, prep, 0)
    sf[...] = s0_ref[0, 0, 0]
    sb[...] = s0_ref[0, 0, 1]
    scale = GLA_HK ** -0.5
    fconst = (wf_ref, nqf_ref, nkf_ref, mf_ref)
    bconst = (wb_ref, nqb_ref, nkb_ref, mb_ref)

    def body(ci, carry):
        chains = []
        for fwd, ga, consts in ((True, gaf, fconst), (False, gab, bconst)):
            cvals = tuple(x[...] for x in consts)
            for u in range(un):
                idx = ci * un + u
                r = pl.multiple_of((idx if fwd else nc - 1 - idx) * c, c)
                chains.append(dict(r=r, fwd=fwd, consts=cvals, ga=ga[pl.ds(r, c), :],
                                   q=q_ref[0, pl.ds(r, c), :].astype(F32) * scale,
                                   k=k_ref[0, pl.ds(r, c), :].astype(F32), v=v_ref[0, pl.ds(r, c), :]))
        _gla_group(chains, eye_ref[...])
        for fwd, st_ref in ((True, sf), (False, sb)):
            st = st_ref[...]
            for ch in chains:
                if ch["fwd"] != fwd:
                    continue
                o = ch["o"] + jnp.dot(ch["qt"], st.astype(BF16), preferred_element_type=F32)
                o_ref[0, pl.ds(ch["r"], c), :] += o
                st = st * ch["dec_col"] + ch["upd"]
            st_ref[...] = st
        return carry

    lax.fori_loop(0, nc // un, body, 0)
    s_ref[0, 0, 0] = sf[...]
    s_ref[0, 0, 1] = sb[...]


def _gla_scan(pmain, pa, w2p, b2p, s0):
    b, t, _ = pmain.shape
    h = GLA_HEADS
    assert t % (GLA_CHUNK * GLA_UNROLL) == 0
    consts = _gla_consts(True) + _gla_consts(False) + (jnp.asarray(np.eye(GLA_HK), BF16),)
    kblk = GLA_DK // GLA_HK
    vblk = (2 * GLA_DK) // GLA_HV
    cspec = lambda a: pl.BlockSpec(a.shape, lambda i, j: (0,) * a.ndim)
    return pl.pallas_call(
        functools.partial(_gla_kernel, t=t),
        grid=(b, h),
        in_specs=[pl.BlockSpec((1, t, GLA_HK), lambda i, j: (i, 0, j)),
                  pl.BlockSpec((1, t, GLA_HK), lambda i, j: (i, 0, kblk + j)),
                  pl.BlockSpec((1, t, GLA_HV), lambda i, j: (i, 0, vblk + j)),
                  pl.BlockSpec((1, t, LANE), lambda i, j: (i, 0, 0)),
                  pl.BlockSpec((2, LANE, GLA_HK), lambda i, j: (0, 0, j)),
                  pl.BlockSpec((2, 1, GLA_HK), lambda i, j: (0, 0, j)),
                  pl.BlockSpec((1, 1, 2, GLA_HK, GLA_HV), lambda i, j: (i, j, 0, 0, 0))]
                 + [cspec(a) for a in consts],
        out_specs=[pl.BlockSpec((1, t, GLA_HV), lambda i, j: (i, 0, j)),
                   pl.BlockSpec((1, 1, 2, GLA_HK, GLA_HV), lambda i, j: (i, j, 0, 0, 0))],
        out_shape=[jax.ShapeDtypeStruct((b, t, GLA_DV), F32),
                   jax.ShapeDtypeStruct((b, h, 2, GLA_HK, GLA_HV), F32)],
        scratch_shapes=[pltpu.VMEM((t, GLA_HK), F32), pltpu.VMEM((t, GLA_HK), F32),
                        pltpu.VMEM((GLA_HK, GLA_HV), F32), pltpu.VMEM((GLA_HK, GLA_HV), F32)],
        compiler_params=_cparams(("arbitrary", "arbitrary")),
        name="gla_scan",
    )(pmain, pmain, pmain, pa, w2p, b2p, s0, *consts)


def _ssd_consts():
    c = SSD_CHUNK
    i = np.arange(c)
    tril = (i[:, None] >= i[None, :]).astype(np.float32)
    triu = (i[:, None] <= i[None, :]).astype(np.float32)
    eye = np.eye(SSD_STATE, dtype=np.float32)
    sel16 = np.eye(16, LANE, dtype=np.float32)
    e4 = np.zeros((2, LANE, SSD_GW), np.float32)
    for z in range(2):
        for r in range(SSD_REP):
            e4[z, z * SSD_REP + r, r * SSD_HEADDIM:(r + 1) * SSD_HEADDIM] = 1.0
    lm = np.zeros((SSD_REP * c, SSD_GW), np.float32)
    for r in range(SSD_REP):
        lm[r * c:(r + 1) * c, r * SSD_HEADDIM:(r + 1) * SSD_HEADDIM] = 1.0
    return (jnp.asarray(tril, BF16), jnp.asarray(triu, BF16), jnp.asarray(tril), jnp.asarray(triu),
            jnp.asarray(eye, BF16), jnp.asarray(sel16, BF16), jnp.asarray(e4, BF16), jnp.asarray(lm))


def _conv_block(in_ref, w, bias, i, nblk, rows, t, taps):
    halo = 2 * SUBLANE
    r = pl.multiple_of(i * rows, rows)
    cur = in_ref[0, pl.ds(r, rows), :].astype(F32)
    rp = pl.multiple_of(jnp.maximum(r - halo, 0), halo)
    rn = pl.multiple_of(jnp.minimum(r + rows, t - halo), halo)
    prev = jnp.where(i > 0, in_ref[0, pl.ds(rp, halo), :].astype(F32), 0.0)
    nxt = jnp.where(i < nblk - 1, in_ref[0, pl.ds(rn, halo), :].astype(F32), 0.0)
    ext = jnp.concatenate([prev, cur, nxt], axis=0)
    half = taps // 2
    acc = bias
    for j in range(taps):
        off = halo - half + j
        acc = acc + w[j:j + 1] * ext[off:off + rows]
    return r, acc


def _ssd_group(chains, eye, sel16, lm):
    c = SSD_CHUNK
    for ch in chains:
        ch["cum"] = _dot_exact_l(ch["tri"], ch["das"])
        ch["dt_e"] = _dot_split2_r(ch["ds"], ch["e4z"])
        ch["ccb"], bcb = ch["cc"].astype(BF16), ch["bc"].astype(BF16)
        ch["cb"] = _dot_nt(ch["ccb"], bcb)
        ch["bc_t"] = _dot_nt(eye, bcb).astype(BF16)
    for ch in chains:
        ch["cum_t"] = _dot_exact_nt(sel16, ch["cum"])
        ch["cum_e"] = _dot_split2_r(ch["cum"], ch["e4z"])
    for ch in chains:
        cum, cum_t, cum_e = ch["cum"], ch["cum_t"], ch["cum_e"]
        tot_e = cum_e[c - 1:c] if ch["fwd"] else cum_e[0:1]
        xdt = ch["xg"] * ch["dt_e"]
        cb = ch["cb"] * ch["mask"]
        ms = []
        for r in range(SSD_REP):
            q = ch["z"] * SSD_REP + r
            seg = cum[:, q:q + 1] - cum_t[q:q + 1, :]
            ms.append((cb * jnp.exp2(jnp.minimum(seg, 0.0))).astype(BF16))
        ch["mcat"] = jnp.concatenate(ms, axis=1)
        ch["xbd"] = (jnp.concatenate([xdt] * SSD_REP, axis=0) * lm).astype(BF16)
        ch["w"] = (xdt * jnp.exp2(tot_e - cum_e)).astype(BF16)
        ch["dece"] = jnp.exp2(cum_e)
        ch["dec"] = jnp.exp2(tot_e)
    for ch in chains:
        ch["y"] = jnp.dot(ch["mcat"], ch["xbd"], preferred_element_type=F32)
        ch["upd"] = jnp.dot(ch["bc_t"], ch["w"], preferred_element_type=F32)


def _ssd_kernel(x_ref, bm_ref, cm_ref, dt_ref, wx_ref, wb_ref, wc_ref, bx_ref, bb_ref, bcb_ref,
                dtb_ref, alog_ref, dsk_ref, s0_ref,
                tl_ref, tu_ref, ml_ref, mu_ref, eye_ref, sel16_ref, e4_ref, lm_ref,
                y_ref, s_ref, xc, bcs, ccs, dsel, dasel, sf, sb, *, t):
    c = SSD_CHUNK
    nc = t // c
    un = min(SSD_UNROLL, nc)
    pb = c
    nblk = t // pb

    shift = (LANE - 2 * SSD_REP * pl.program_id(1)) % LANE

    def prep(i):
        for in_ref, w_ref, b_ref, out in ((x_ref, wx_ref, bx_ref, xc), (bm_ref, wb_ref, bb_ref, bcs),
                                          (cm_ref, wc_ref, bcb_ref, ccs)):
            r, acc = _conv_block(in_ref, w_ref[...], b_ref[...], i, nblk, pb, t, SSD_CONV)
            val = _silu(acc)
            out[pl.ds(r, pb), :] = val
            if out is xc:
                y_ref[0, pl.ds(r, pb), :] = val * dsk_ref[...]
        r = pl.multiple_of(i * pb, pb)
        dtp = _softplus(dt_ref[0, pl.ds(r, pb), :] + dtb_ref[...])
        a2 = jnp.exp(alog_ref[...]) * (-LOG2E)
        dsel[pl.ds(r, pb), :] = pltpu.roll(dtp, shift, 1)
        dasel[pl.ds(r, pb), :] = pltpu.roll(dtp * a2, shift, 1)

    for blk in sorted(set(range(un)) | set(nc - 1 - u for u in range(un))):
        prep(jnp.int32(blk))
    sf[...] = s0_ref[0, 0, 0]
    sb[...] = s0_ref[0, 0, 1]
    n_ahead = max(nc // (2 * un) - 1, 0)

    def body(ci, carry, ahead):
        chains = []
        for fwd, z, tri_ref, mk_ref in ((True, 0, tl_ref, ml_ref), (False, 1, tu_ref, mu_ref)):
            tri, mask, e4z = tri_ref[...], mk_ref[...], e4_ref[z]
            for u in range(un):
                idx = ci * un + u
                r = pl.multiple_of((idx if fwd else nc - 1 - idx) * c, c)
                chains.append(dict(r=r, fwd=fwd, z=z, tri=tri, mask=mask, e4z=e4z,
                                   xg=xc[pl.ds(r, c), :], bc=bcs[pl.ds(r, c), :], cc=ccs[pl.ds(r, c), :],
                                   ds=dsel[pl.ds(r, c), :], das=dasel[pl.ds(r, c), :]))
        if ahead:
            for u in range(un):
                nxt = (ci + 1) * un + u
                prep(nxt)
                prep(nc - 1 - nxt)
        _ssd_group(chains, eye_ref[...], sel16_ref[...], lm_ref[...])
        for fwd, st_ref in ((True, sf), (False, sb)):
            st = st_ref[...]
            for ch in chains:
                if ch["fwd"] != fwd:
                    continue
                y = ch["y"] + jnp.dot(ch["ccb"], st.astype(BF16), preferred_element_type=F32) * ch["dece"]
                y_ref[0, pl.ds(ch["r"], c), :] += y
                st = st * ch["dec"] + ch["upd"]
            st_ref[...] = st
        return carry

    lax.fori_loop(0, n_ahead, functools.partial(body, ahead=True), 0)
    lax.fori_loop(n_ahead, nc // un, functools.partial(body, ahead=False), 0)
    s_ref[0, 0, 0] = sf[...]
    s_ref[0, 0, 1] = sb[...]


def _ssd_scan(pmain, pdt, conv_w, conv_b, dtb, alog, dskip, s0):
    b, t, _ = pmain.shape
    g = SSD_GROUPS
    assert t % (SSD_CHUNK * min(SSD_UNROLL, t // SSD_CHUNK)) == 0
    tril, triu, mtril, mtriu, eye, sel16, e4, lm = _ssd_consts()
    xblk = SSD_DI // SSD_GW
    bblk = (2 * SSD_DI) // SSD_STATE
    cblk = bblk + SSD_GN // SSD_STATE
    wbblk = SSD_DI // SSD_STATE
    wcblk = wbblk + SSD_GN // SSD_STATE
    cspec = lambda a: pl.BlockSpec(a.shape, lambda i, j: (0,) * a.ndim)
    return pl.pallas_call(
        functools.partial(_ssd_kernel, t=t),
        grid=(b, g),
        in_specs=[pl.BlockSpec((1, t, SSD_GW), lambda i, j: (i, 0, xblk + j)),
                  pl.BlockSpec((1, t, SSD_STATE), lambda i, j: (i, 0, bblk + j)),
                  pl.BlockSpec((1, t, SSD_STATE), lambda i, j: (i, 0, cblk + j)),
                  pl.BlockSpec((1, t, LANE), lambda i, j: (i, 0, 0)),
                  pl.BlockSpec((SSD_CONV, SSD_GW), lambda i, j: (0, j)),
                  pl.BlockSpec((SSD_CONV, SSD_STATE), lambda i, j: (0, wbblk + j)),
                  pl.BlockSpec((SSD_CONV, SSD_STATE), lambda i, j: (0, wcblk + j)),
                  pl.BlockSpec((1, SSD_GW), lambda i, j: (0, j)),
                  pl.BlockSpec((1, SSD_STATE), lambda i, j: (0, wbblk + j)),
                  pl.BlockSpec((1, SSD_STATE), lambda i, j: (0, wcblk + j)),
                  pl.BlockSpec((1, LANE), lambda i, j: (0, 0)),
                  pl.BlockSpec((1, LANE), lambda i, j: (0, 0)),
                  pl.BlockSpec((1, SSD_GW), lambda i, j: (0, j)),
                  pl.BlockSpec((1, 1, 2, SSD_STATE, SSD_GW), lambda i, j: (i, j, 0, 0, 0)),
                  cspec(tril), cspec(triu), cspec(mtril), cspec(mtriu), cspec(eye), cspec(sel16),
                  cspec(e4), cspec(lm)],
        out_specs=[pl.BlockSpec((1, t, SSD_GW), lambda i, j: (i, 0, j)),
                   pl.BlockSpec((1, 1, 2, SSD_STATE, SSD_GW), lambda i, j: (i, j, 0, 0, 0))],
        out_shape=[jax.ShapeDtypeStruct((b, t, SSD_DI), F32),
                   jax.ShapeDtypeStruct((b, g, 2, SSD_STATE, SSD_GW), F32)],
        scratch_shapes=[pltpu.VMEM((t, SSD_GW), F32), pltpu.VMEM((t, SSD_STATE), F32),
                        pltpu.VMEM((t, SSD_STATE), F32), pltpu.VMEM((t, LANE), F32),
                        pltpu.VMEM((t, LANE), F32),
                        pltpu.VMEM((SSD_STATE, SSD_GW), F32), pltpu.VMEM((SSD_STATE, SSD_GW), F32)],
        compiler_params=_cparams(("arbitrary", "arbitrary")),
        name="ssd_scan",
    )(pmain, pmain, pmain, pdt, conv_w, conv_w, conv_w, conv_b, conv_b, conv_b,
      dtb, alog, dskip, s0, tril, triu, mtril, mtriu, eye, sel16, e4, lm)


def _conv3_kernel(p_ref, w_ref, b_ref, *out_refs, t):
    pb = min(t, 256)
    nblk = t // pb

    def blk(i, carry):
        r, acc = _conv_block(p_ref, w_ref[...], b_ref[...], i, nblk, pb, t, HY_SHORT)
        out_refs[0][0, pl.ds(r, pb), :] = acc
        if len(out_refs) > 1:
            out_refs[1][0, pl.ds(r, pb), :] = acc.astype(BF16)
        return carry

    lax.fori_loop(0, nblk, blk, 0)


def _conv3(p, w, bias, col0, ncols, with_bf16):
    b, t, _ = p.shape
    tn = 256
    off = col0 // tn
    out_shape = [jax.ShapeDtypeStruct((b, t, ncols), F32)]
    out_specs = [pl.BlockSpec((1, t, tn), lambda i, j: (i, 0, j))]
    if with_bf16:
        out_shape.append(jax.ShapeDtypeStruct((b, t, ncols), BF16))
        out_specs.append(pl.BlockSpec((1, t, tn), lambda i, j: (i, 0, j)))
    return pl.pallas_call(
        functools.partial(_conv3_kernel, t=t),
        grid=(b, ncols // tn),
        in_specs=[pl.BlockSpec((1, t, tn), lambda i, j: (i, 0, off + j)),
                  pl.BlockSpec((HY_SHORT, tn), lambda i, j: (0, off + j)),
                  pl.BlockSpec((1, tn), lambda i, j: (0, off + j))],
        out_specs=out_specs,
        out_shape=out_shape,
        compiler_params=_cparams(("arbitrary", "arbitrary")),
        name="hy_conv3",
    )(p, w, bias)


HY_TN = 512


def _filt_kernel(w1_ref, b1_ref, w2_ref, b2_ref, w3_ref, b3_ref, w4_ref, fr_ref, dl_ref, o_ref, h_scr,
                 *, l, lp, radix, tl):
    nsub = 2 * radix - 1
    bi = pl.program_id(0)
    part = bi // nsub
    delta = bi % nsub - (radix - 1)
    sign = jnp.where(part == 0, jnp.where(delta >= 0, 1, -1), jnp.where(delta >= 1, -1, 1))
    row = lax.broadcasted_iota(jnp.int32, (tl, 1), 0) + pl.program_id(1) * tl
    pos = (jnp.abs(delta) * lp + sign * row).astype(F32)
    valid = jnp.logical_or(part == 0, row > 0)
    tt = pos * (1.0 / (l - 1))

    @pl.when(pl.program_id(2) == 0)
    def _():
        bands = (HY_EMB - 1) // 2
        lane = lax.broadcasted_iota(jnp.int32, (1, LANE), 1)
        band = ((lane - 1) & (bands - 1)).astype(F32) * ((bands - 1 - 1e-4) / (bands - 1)) + 1e-4
        ang = (pos * (2 * math.pi / l)) * band
        z = jnp.where(lane == 0, tt,
                      jnp.where(lane <= bands, jnp.cos(ang), jnp.where(lane <= 2 * bands, -jnp.sin(ang), 0.0)))
        fr = fr_ref[...]
        h = jnp.sin(fr * (_dot_f32x3(z, w1_ref[...]) + b1_ref[...]))
        h = jnp.sin(fr * (_dot_f32x3(h, w2_ref[...]) + b2_ref[...]))
        h_scr[...] = jnp.sin(fr * (_dot_f32x3(h, w3_ref[...]) + b3_ref[...]))

    hh = _dot_f32x3(h_scr[...], w4_ref[...]) * jnp.exp(-tt * dl_ref[...])
    o_ref[0] = jnp.where(valid, hh, 0.0)


def _hy_filters(l, lp, radix, w1, b1, w2, b2, w3, b3, w4, freq):
    d = D_MODEL
    tl = min(lp, 512)
    tn = HY_TN
    nd = d // tn
    nsub = 2 * radix - 1
    assert (HY_EMB - 1) // 2 == 16
    pad2 = lambda a: jnp.zeros((LANE, LANE), F32).at[:a.shape[0], :a.shape[1]].set(a)
    pad1 = lambda a: jnp.zeros((1, LANE), F32).at[0, :a.shape[0]].set(a)
    w4p = jnp.zeros((LANE, HY_ORDER * 2 * d), F32).at[:HY_FW].set(w4)
    deltas = np.abs(np.linspace(math.log(HY_DECAY_TARGET) / HY_FAST_DECAY,
                                math.log(HY_DECAY_TARGET) / HY_SLOW_DECAY, d))
    dl = jnp.asarray(deltas[None, :], F32)
    full = lambda a: pl.BlockSpec(a.shape, lambda bi, m, j: (0, 0))
    small = [pad2(w1), pad1(b1), pad2(w2), pad1(b2), pad2(w3), pad1(b3)]

    def w4_map(bi, m, j):
        didx = bi % nsub
        back = jnp.where(bi < nsub, didx < radix - 1, didx <= radix - 1).astype(jnp.int32)
        return (0, ((j // nd) * 2 + back) * nd + j % nd)

    return pl.pallas_call(
        functools.partial(_filt_kernel, l=l, lp=lp, radix=radix, tl=tl),
        grid=(2 * nsub, lp // tl, HY_ORDER * nd),
        in_specs=[full(a) for a in small]
                 + [pl.BlockSpec((LANE, tn), w4_map), pl.BlockSpec((1, LANE), lambda bi, m, j: (0, 0)),
                    pl.BlockSpec((1, tn), lambda bi, m, j: (0, j % nd))],
        out_specs=pl.BlockSpec((1, tl, tn), lambda bi, m, j: (bi // nsub, m, j * nsub + bi % nsub)),
        out_shape=jax.ShapeDtypeStruct((2, lp, HY_ORDER * nsub * d), F32),
        scratch_shapes=[pltpu.VMEM((tl, LANE), F32)],
        compiler_params=_cparams(("arbitrary", "arbitrary", "arbitrary")),
        name="hy_filt",
    )(*small, w4p, pad1(freq), dl)


def _dft_tables_np(l):
    n = 2 * l
    k = np.arange(l)
    ang = 2 * math.pi * ((k[:, None] * k[None, :]) % n) / n
    cm = np.cos(ang)
    sm = -np.sin(ang)
    alt = np.where(k % 2 == 0, 1.0, -1.0)
    s_fwd = sm.copy()
    s_fwd[0, :] = alt
    s_inv = sm.copy()
    s_inv[:, 0] = alt
    return cm, s_fwd, s_inv


def _dft_tab_kernel(ca_ref, sa_ref, cb_ref, sb_ref, c_ref, sf_ref, si_ref, *, rb):
    a = pl.program_id(0)
    ca, sa = ca_ref[0], sa_ref[0]
    cb, sb = cb_ref[...], sb_ref[...]
    cm = ca * cb - sa * sb
    sm = -(sa * cb + ca * sb)
    l = cm.shape[1]
    row = lax.broadcasted_iota(jnp.int32, (rb, l), 0)
    col = lax.broadcasted_iota(jnp.int32, (rb, l), 1)
    alt_col = jnp.where(col % 2 == 0, 1.0, -1.0)
    alt_row = jnp.where(row % 2 == 0, 1.0, -1.0)
    c_ref[...] = cm.astype(BF16)
    sf_ref[...] = jnp.where(jnp.logical_and(row == 0, a == 0), alt_col, sm).astype(BF16)
    si_ref[...] = jnp.where(col == 0, alt_row, sm).astype(BF16)


def _dft_tables(l):
    if l <= 512:
        return tuple(jnp.asarray(m, BF16) for m in _dft_tables_np(l))
    rb = 64
    na = l // rb
    n = 2 * l
    nn = np.arange(l)
    aa = np.arange(na)
    bb = np.arange(rb)
    ang_a = 2 * math.pi * ((aa[:, None] * rb * nn[None, :]) % n) / n
    ang_b = 2 * math.pi * ((bb[:, None] * nn[None, :]) % n) / n
    ca = jnp.asarray(np.cos(ang_a)[:, None, :], F32)
    sa = jnp.asarray(np.sin(ang_a)[:, None, :], F32)
    cb = jnp.asarray(np.cos(ang_b), F32)
    sb = jnp.asarray(np.sin(ang_b), F32)
    rowspec = pl.BlockSpec((1, 1, l), lambda a: (a, 0, 0))
    tabspec = pl.BlockSpec((rb, l), lambda a: (0, 0))
    outspec = pl.BlockSpec((rb, l), lambda a: (a, 0))
    return tuple(pl.pallas_call(
        functools.partial(_dft_tab_kernel, rb=rb),
        grid=(na,),
        in_specs=[rowspec, rowspec, tabspec, tabspec],
        out_specs=[outspec] * 3,
        out_shape=[jax.ShapeDtypeStruct((l, l), BF16)] * 3,
        compiler_params=_cparams(("arbitrary",)),
        name="dft_tables",
    )(ca, sa, cb, sb))


def _spec_filt_kernel(c_ref, s_ref, a_ref, b_ref, h_ref, accr, acci, accn, *, l):
    kk = pl.program_id(2)

    @pl.when(kk == 0)
    def _():
        accr[...] = jnp.zeros_like(accr)
        acci[...] = jnp.zeros_like(acci)
        accn[...] = jnp.zeros_like(accn)

    a, bw = a_ref[0], b_ref[0]
    sm = (a + bw).astype(BF16)
    df = (a - bw).astype(BF16)
    accr[...] += jnp.dot(c_ref[...], sm, preferred_element_type=F32)
    acci[...] += jnp.dot(s_ref[...], df, preferred_element_type=F32)
    accn[...] += jnp.dot(s_ref[...], sm, preferred_element_type=F32)

    @pl.when(kk == pl.num_programs(2) - 1)
    def _():
        tm = accr.shape[0]
        row0 = (lax.broadcasted_iota(jnp.int32, (tm, 1), 0) + pl.program_id(0) * tm) == 0
        scale = jnp.where(row0, 0.5 / l, 1.0 / l)
        h_ref[0] = accr[...] * scale
        h_ref[1] = jnp.where(row0, accn[...], acci[...]) * scale


def _spec_filt(cm, s_fwd, fab, l):
    ncol = fab.shape[2]
    tm, tn, tk = min(l, 512), HY_TN, min(l, 1024)
    return pl.pallas_call(
        functools.partial(_spec_filt_kernel, l=l),
        grid=(l // tm, ncol // tn, l // tk),
        in_specs=[pl.BlockSpec((tm, tk), lambda m, j, k: (m, k)),
                  pl.BlockSpec((tm, tk), lambda m, j, k: (m, k)),
                  pl.BlockSpec((1, tk, tn), lambda m, j, k: (0, k, j)),
                  pl.BlockSpec((1, tk, tn), lambda m, j, k: (1, k, j))],
        out_specs=pl.BlockSpec((2, tm, tn), lambda m, j, k: (0, m, j)),
        out_shape=jax.ShapeDtypeStruct((2, l, ncol), F32),
        scratch_shapes=[pltpu.VMEM((tm, tn), F32)] * 3,
        compiler_params=_cparams(("arbitrary", "arbitrary", "arbitrary")),
        name="hy_spec_filt",
    )(cm, s_fwd, fab, fab)


def _seg_spec_kernel(c_ref, s_ref, *refs, radix):
    u_refs, g_ref, z_ref = refs[:radix], refs[radix], refs[radix + 1]
    tm, tn = z_ref.shape[2], z_ref.shape[3]
    spec = []
    for q in range(radix):
        u = u_refs[q][0]
        spec.append((jnp.dot(c_ref[...], u, preferred_element_type=F32),
                     jnp.dot(s_ref[...], u, preferred_element_type=F32)))
    row0 = (lax.broadcasted_iota(jnp.int32, (tm, 1), 0) + pl.program_id(1) * tm) == 0
    for t in range(radix):
        zr = zi = None
        for q in range(radix):
            di = t - q + radix - 1
            gr = g_ref[0, :, di * tn:(di + 1) * tn]
            gi = g_ref[1, :, di * tn:(di + 1) * tn]
            xr, xi = spec[q]
            tr = xr * gr - jnp.where(row0, 0.0, xi * gi)
            ti = jnp.where(row0, xi * gi, xr * gi + xi * gr)
            zr = tr if zr is None else zr + tr
            zi = ti if zi is None else zi + ti
        z_ref[t, 0] = zr.astype(BF16)
        z_ref[t, 1] = zi.astype(BF16)


def _seg_spec(cm, s_fwd, ub, gspec, order, radix):
    b, l, d = ub.shape
    lp = l // radix
    tm, tn = min(lp, 256), HY_TN
    nd = d // tn
    nsub = 2 * radix - 1
    u_specs = [pl.BlockSpec((1, lp, tn), functools.partial(lambda n, m, i, q: (i, q, n), q=q)) for q in range(radix)]
    return pl.pallas_call(
        functools.partial(_seg_spec_kernel, radix=radix),
        grid=(nd, lp // tm, b),
        in_specs=[pl.BlockSpec((tm, lp), lambda n, m, i: (m, 0)),
                  pl.BlockSpec((tm, lp), lambda n, m, i: (m, 0))] + u_specs
                 + [pl.BlockSpec((2, tm, nsub * tn), lambda n, m, i: (0, m, order * nd + n))],
        out_specs=pl.BlockSpec((radix, 2, tm, tn), lambda n, m, i: (0, 0, m, i * nd + n)),
        out_shape=jax.ShapeDtypeStruct((radix, 2, lp, b * d), BF16),
        compiler_params=_cparams(("arbitrary", "arbitrary", "arbitrary")),
        name="hy_seg_spec",
    )(cm, s_fwd, *([ub] * radix), gspec)


def _seg_inv_kernel(c_ref, s_ref, z_ref, u_ref, g_ref, bias_ref, *out_refs):
    conv = (jnp.dot(c_ref[...], z_ref[0, 0], preferred_element_type=F32)
            + jnp.dot(s_ref[...], z_ref[0, 1], preferred_element_type=F32))
    res = g_ref[0] * (conv + u_ref[0] * bias_ref[...])
    out_refs[0][0] = res
    if len(out_refs) > 1:
        out_refs[1][0] = res.astype(BF16)


def _seg_inv(cm, s_inv, zspec, u, gate, gsel, bias, with_bf16, radix):
    b, l, d = u.shape
    lp = l // radix
    tt, tn = min(lp, 512), HY_TN
    nd = d // tn
    nt = lp // tt
    blk = pl.BlockSpec((1, tt, tn), lambda t, j, r: (j // nd, r * nt + t, j % nd))
    out_shape = [jax.ShapeDtypeStruct((b, l, d), F32)]
    out_specs = [blk]
    if with_bf16:
        out_shape.append(jax.ShapeDtypeStruct((b, l, d), BF16))
        out_specs.append(blk)
    return pl.pallas_call(
        _seg_inv_kernel,
        grid=(nt, b * nd, radix),
        in_specs=[pl.BlockSpec((tt, lp), lambda t, j, r: (t, 0)),
                  pl.BlockSpec((tt, lp), lambda t, j, r: (t, 0)),
                  pl.BlockSpec((1, 2, lp, tn), lambda t, j, r: (r, 0, 0, j)),
                  blk,
                  pl.BlockSpec((1, tt, tn), lambda t, j, r: (j // nd, r * nt + t, gsel * nd + j % nd)),
                  pl.BlockSpec((1, tn), lambda t, j, r: (0, j % nd))],
        out_specs=out_specs,
        out_shape=out_shape,
        compiler_params=_cparams(("arbitrary", "arbitrary", "arbitrary")),
        name="hy_seg_inv",
    )(cm, s_inv, zspec, u, gate, bias)


def _hyena_run(p, conv_w, conv_b, fw, h_bias):
    b, l, _ = p.shape
    d = D_MODEL
    radix = max(1, min(HY_RADIX, l // HY_MIN_BLOCK_LEN))
    lp = l // radix
    v, vb = _conv3(p, conv_w, conv_b, 0, d, True)
    x12 = _conv3(p, conv_w, conv_b, d, 2 * d, False)[0]
    cm, s_fwd, s_inv = _dft_tables(lp)
    gspec = _spec_filt(cm, s_fwd, _hy_filters(l, lp, radix, *fw), lp)
    z1 = _seg_spec(cm, s_fwd, vb, gspec, 0, radix)
    z, zb = _seg_inv(cm, s_inv, z1, v, x12, 0, h_bias[0:1], True, radix)
    z2 = _seg_spec(cm, s_fwd, zb, gspec, 1, radix)
    return _seg_inv(cm, s_inv, z2, z, x12, 1, h_bias[1:2], False, radix)[0]


def _snake_kernel(x_ref, j_ref, o_ref):
    jm = j_ref[...]
    for g in range(x_ref.shape[1] // (2 * GRID_W)):
        r0 = g * 2 * GRID_W
        o_ref[0, r0:r0 + GRID_W, :] = x_ref[0, r0:r0 + GRID_W, :]
        o_ref[0, r0 + GRID_W:r0 + 2 * GRID_W, :] = _dot_exact_l(jm, x_ref[0, r0 + GRID_W:r0 + 2 * GRID_W, :])


def _snake(h):
    b, l, ch = h.shape
    tm = 512
    jm = jnp.asarray(np.eye(GRID_W)[::-1].copy(), BF16)
    return pl.pallas_call(
        _snake_kernel,
        grid=(b, l // tm),
        in_specs=[pl.BlockSpec((1, tm, ch), lambda i, j: (i, j, 0)),
                  pl.BlockSpec((GRID_W, GRID_W), lambda i, j: (0, 0))],
        out_specs=pl.BlockSpec((1, tm, ch), lambda i, j: (i, j, 0)),
        out_shape=jax.ShapeDtypeStruct((b, l, ch), F32),
        compiler_params=_cparams(("arbitrary", "arbitrary")),
        name="snake",
    )(h, jm)


def _pad_cols(w, n):
    return jnp.zeros((w.shape[0], n), w.dtype).at[:, :w.shape[1]].set(w)


def kernel(x, c, ctx, c_ctx, ada_w, ada_b, ln_g, ln_b, ffn_w1, ffn_w2, gla_w_in, gla_w_a2, gla_b_a2, gla_norm, gla_w_out, ssd_w_in, ssd_conv_w, ssd_conv_b, ssd_dt_bias, ssd_a_log, ssd_d, ssd_norm, ssd_w_out, hy_w_in, hy_conv_w, hy_conv_b, hy_f_w1, hy_f_b1, hy_f_w2, hy_f_b2, hy_f_w3, hy_f_b3, hy_f_w4, hy_f_freq, hy_bias, hy_w_out):
    bsz, _, d = x.shape
    hl = _snake(x)
    hc = ctx.reshape(1, -1, d)
    cvec = jnp.zeros((16, d), F32).at[:bsz].set(c).at[bsz].set(c_ctx)
    mods = _ada(cvec, ada_w, ada_b).reshape(DEPTH, 16, 6, d)
    mods = jnp.concatenate([mods, jnp.zeros((DEPTH, 16, 2, d), F32)], axis=2)
    per_batch = lambda a: a.reshape(bsz, -1, a.shape[-1])
    like = lambda a, h: a.reshape(h.shape[0], -1, a.shape[-1])

    for i in range(DEPTH):
        kind, j = i % N_MIXERS, i // N_MIXERS
        need_ctx = i < DEPTH - 1
        ml = mods[i, :bsz]
        mc = mods[i, bsz][None]
        g0, b0 = ln_g[i, 0][None], ln_b[i, 0][None]
        g1, b1 = ln_g[i, 1][None], ln_b[i, 1][None]
        w1 = ffn_w1[i].astype(BF16)
        w2 = ffn_w2[i].astype(BF16)
        streams = [(hc, mc, True), (hl, ml, False)]
        if kind == 0:
            w_main = gla_w_in[j][:, :GLA_MAIN].astype(BF16)
            w_a = _pad_cols(gla_w_in[j][:, GLA_MAIN:], LANE).astype(BF16)
            w2p = jnp.zeros((2, LANE, GLA_DK), F32)
            for z in range(2):
                w2p = w2p.at[z, z * GLA_RANK:(z + 1) * GLA_RANK].set(gla_w_a2[j, z])
            b2p = gla_b_a2[j][:, None, :]
            w_out = gla_w_out[j].astype(BF16)
            ng = gla_norm[j][None]
            state = jnp.zeros((bsz, GLA_HEADS, 2, GLA_HK, GLA_HV), F32)
            new = []
            for h, m, is_ctx in streams:
                pmain = _proj(h, m, w_main, 1024, BF16)
                pa = _proj(h, m, w_a, LANE, F32)
                o, st = _gla_scan(per_batch(pmain), per_batch(pa), w2p, b2p, state)
                if is_ctx:
                    state = st
                if is_ctx and not need_ctx:
                    new.append(h)
                    continue
                new.append(_gla_out(like(o, h), pmain, h, m, ng, w_out, g0, b0))
            hc, hl = new
        elif kind == 1:
            perm = np.arange(2 * SSD_HEADS).reshape(2, SSD_GROUPS, SSD_REP).transpose(1, 0, 2).reshape(-1)
            w_main = ssd_w_in[j][:, :SSD_MAIN].astype(BF16)
            w_dt = _pad_cols(ssd_w_in[j][:, SSD_MAIN:][:, perm], LANE).astype(BF16)
            dtb = _pad_cols(ssd_dt_bias[j].reshape(1, -1)[:, perm], LANE)
            alog = _pad_cols(ssd_a_log[j].reshape(1, -1)[:, perm], LANE)
            dskip = jnp.repeat(ssd_d[j], SSD_HEADDIM)[None]
            cw = ssd_conv_w[j]
            cbias = ssd_conv_b[j][None]
            w_out = ssd_w_out[j].astype(BF16)
            ng = ssd_norm[j][None]
            state = jnp.zeros((bsz, SSD_GROUPS, 2, SSD_STATE, SSD_GW), F32)
            new = []
            for h, m, is_ctx in streams:
                pmain = _proj(h, m, w_main, 1024, BF16)
                pdt = _proj(h, m, w_dt, LANE, F32)
                y, st = _ssd_scan(per_batch(pmain), per_batch(pdt), cw, cbias, dtb, alog, dskip, state)
                if is_ctx:
                    state = st
                if is_ctx and not need_ctx:
                    new.append(h)
                    continue
                new.append(_ssd_out(like(y, h), pmain, h, m, ng, w_out, g0, b0))
            hc, hl = new
        else:
            w_in = hy_w_in[j].astype(BF16)
            w_out = hy_w_out[j].astype(BF16)
            fw = (hy_f_w1[j], hy_f_b1[j], hy_f_w2[j], hy_f_b2[j], hy_f_w3[j], hy_f_b3[j], hy_f_w4[j], hy_f_freq[j])
            new = []
            for h, m, is_ctx in streams:
                if is_ctx and not need_ctx:
                    new.append(h)
                    continue
                p = per_batch(_proj(h, m, w_in, 1024, BF16))
                zz = _hyena_run(p, hy_conv_w[j], hy_conv_b[j][None], fw, hy_bias[j])
                new.append(_hy_out(like(zz, h), h, m, w_out, g0, b0))
            hc, hl = new
        hl = _ffn(hl, ml, w1, w2, g1, b1)
        if need_ctx:
            hc = _ffn(hc, mc, w1, w2, g1, b1)
    return _snake(hl)
```

```python
import functools
import math

import numpy as np
import jax
import jax.numpy as jnp
from jax import lax
from jax.experimental import pallas as pl
from jax.experimental.pallas import tpu as pltpu

F32 = jnp.float32
BF16 = jnp.bfloat16

D_MODEL = 1024
DEPTH = 4
GRID_W = 64
N_MIXERS = 3
D_FF = 4 * D_MODEL
DEEPNORM_ALPHA = (2 * DEPTH) ** 0.25
LN_EPS = 1e-5
RMS_EPS = 1e-6

GLA_HEADS = 4
GLA_DK = D_MODEL // 2
GLA_DV = D_MODEL
GLA_HK = GLA_DK // GLA_HEADS
GLA_HV = GLA_DV // GLA_HEADS
GLA_RANK = 16
GLA_GATE_NORM = 16.0
GLA_CHUNK = 64
GLA_UNROLL = 4
GLA_NEG = -1e30
LOG2E = math.log2(math.e)
GLA_MAIN = 2 * GLA_DK + 2 * GLA_DV

SSD_DI = 2 * D_MODEL
SSD_HEADDIM = 64
SSD_HEADS = SSD_DI // SSD_HEADDIM
SSD_GROUPS = 8
SSD_REP = SSD_HEADS // SSD_GROUPS
SSD_STATE = 128
SSD_CONV = 5
SSD_CHUNK = 128
SSD_UNROLL = 2
SSD_GN = SSD_GROUPS * SSD_STATE
SSD_CONV_DIM = SSD_DI + 2 * SSD_GN
SSD_MAIN = SSD_DI + SSD_CONV_DIM
SSD_GW = SSD_REP * SSD_HEADDIM

HY_ORDER = 2
HY_SHORT = 3
HY_EMB = 33
HY_FW = 64
HY_DECAY_TARGET = 1e-2
HY_FAST_DECAY = 0.3
HY_SLOW_DECAY = 1.5
HY_RADIX = 4
HY_MIN_BLOCK_LEN = 1024

LANE = 128
SUBLANE = 8
VMEM_LIMIT = 56 * 1024 * 1024


def _cparams(sem):
    return pltpu.CompilerParams(dimension_semantics=sem, vmem_limit_bytes=VMEM_LIMIT)


def _dot(a, b):
    return jnp.dot(a.astype(BF16), b.astype(BF16), preferred_element_type=F32)


def _dot_nt(a, b):
    return lax.dot_general(a.astype(BF16), b.astype(BF16), (((1,), (1,)), ((), ())),
                           preferred_element_type=F32)


def _split3(x):
    hi = x.astype(BF16)
    r1 = x - hi.astype(F32)
    mid = r1.astype(BF16)
    lo = (r1 - mid.astype(F32)).astype(BF16)
    return hi, mid, lo


def _dot_exact_l(m01, x):
    hi, mid, lo = _split3(x)
    d = lambda p: jnp.dot(m01, p, preferred_element_type=F32)
    return d(hi) + d(mid) + d(lo)


def _dot_exact_r(x, m01):
    hi, mid, lo = _split3(x)
    d = lambda p: jnp.dot(p, m01, preferred_element_type=F32)
    return d(hi) + d(mid) + d(lo)


def _dot_split2_r(x, m01):
    hi = x.astype(BF16)
    mid = (x - hi.astype(F32)).astype(BF16)
    d = lambda p: jnp.dot(p, m01, preferred_element_type=F32)
    return d(hi) + d(mid)


def _dot_exact_nt(m01, x):
    hi, mid, lo = _split3(x)
    d = lambda p: lax.dot_general(m01, p, (((1,), (1,)), ((), ())), preferred_element_type=F32)
    return d(hi) + d(mid) + d(lo)


def _dot_f32(a, b):
    ah, am, al = _split3(a)
    bh, bm, bl = _split3(b)
    d = lambda p, q: jnp.dot(p, q, preferred_element_type=F32)
    return (d(ah, bh) + (d(ah, bm) + d(am, bh)) + (d(ah, bl) + d(al, bh) + d(am, bm)))


def _dot_f32x3(a, b):
    ah, am, _ = _split3(a)
    bh, bm, _ = _split3(b)
    d = lambda p, q: jnp.dot(p, q, preferred_element_type=F32)
    return d(ah, bh) + (d(ah, bm) + d(am, bh))


def _silu(x):
    return x * jax.nn.sigmoid(x)


def _softplus(x):
    return jnp.maximum(x, 0.0) + jnp.log1p(jnp.exp(-jnp.abs(x)))


def _log_sigmoid(x):
    return -_softplus(-x)


def _layer_norm(h, g, b):
    mu = jnp.mean(h, -1, keepdims=True)
    d = h - mu
    var = jnp.mean(d * d, -1, keepdims=True)
    return d * lax.rsqrt(var + LN_EPS) * g + b


def _res_ln(x, gate, y, g, b):
    return _layer_norm(DEEPNORM_ALPHA * x + gate * y, g, b)


def _ada_kernel(c_ref, w_ref, b_ref, o_ref):
    s = _silu(c_ref[...])
    o_ref[0] = _dot_f32(s, w_ref[0]) + b_ref[0]


def _ada(cvec, ada_w, ada_b):
    tn = 1536
    n = 6 * D_MODEL
    return pl.pallas_call(
        _ada_kernel,
        grid=(DEPTH, n // tn),
        in_specs=[pl.BlockSpec((16, D_MODEL), lambda i, j: (0, 0)),
                  pl.BlockSpec((1, D_MODEL, tn), lambda i, j: (i, 0, j)),
                  pl.BlockSpec((1, 1, tn), lambda i, j: (i, 0, j))],
        out_specs=pl.BlockSpec((1, 16, tn), lambda i, j: (i, 0, j)),
        out_shape=jax.ShapeDtypeStruct((DEPTH, 16, n), F32),
        compiler_params=_cparams(("arbitrary", "arbitrary")),
        name="ada",
    )(cvec, ada_w, ada_b.reshape(DEPTH, 1, n))


def _proj_kernel(x_ref, m_ref, w_ref, o_ref, u_scr):
    @pl.when(pl.program_id(2) == 0)
    def _():
        m = m_ref[0]
        u_scr[...] = (x_ref[0] * (1.0 + m[1:2]) + m[0:1]).astype(BF16)

    o_ref[0] = jnp.dot(u_scr[...], w_ref[...], preferred_element_type=F32).astype(o_ref.dtype)


def _proj(x, mods, w, tn, out_dtype):
    b, t, d = x.shape
    n = w.shape[1]
    tm = min(t, 1024)
    return pl.pallas_call(
        _proj_kernel,
        grid=(b, t // tm, n // tn),
        in_specs=[pl.BlockSpec((1, tm, d), lambda i, j, k: (i, j, 0)),
                  pl.BlockSpec((1, 8, d), lambda i, j, k: (i, 0, 0)),
                  pl.BlockSpec((d, tn), lambda i, j, k: (0, k))],
        out_specs=pl.BlockSpec((1, tm, tn), lambda i, j, k: (i, j, k)),
        out_shape=jax.ShapeDtypeStruct((b, t, n), out_dtype),
        scratch_shapes=[pltpu.VMEM((tm, d), BF16)],
        compiler_params=_cparams(("arbitrary", "arbitrary", "arbitrary")),
        name="proj",
    )(x, mods, w)


def _ffn_kernel(x_ref, m_ref, w1_ref, w2_ref, g_ref, b_ref, o_ref, acc_ref, u_scr):
    f = pl.program_id(2)

    @pl.when(f == 0)
    def _():
        m = m_ref[0]
        acc_ref[...] = jnp.zeros_like(acc_ref)
        u_scr[...] = (x_ref[0] * (1.0 + m[4:5]) + m[3:4]).astype(BF16)

    a = jnp.square(jnp.maximum(jnp.dot(u_scr[...], w1_ref[...], preferred_element_type=F32), 0.0))
    acc_ref[...] += _dot(a, w2_ref[...])

    @pl.when(f == pl.num_programs(2) - 1)
    def _():
        o_ref[0] = _res_ln(x_ref[0], m_ref[0][5:6], acc_ref[...], g_ref[...], b_ref[...])


def _ffn(x, mods, w1, w2, g, bb):
    b, t, d = x.shape
    tm = min(t, 1024)
    tf = 512
    return pl.pallas_call(
        _ffn_kernel,
        grid=(b, t // tm, D_FF // tf),
        in_specs=[pl.BlockSpec((1, tm, d), lambda i, j, k: (i, j, 0)),
                  pl.BlockSpec((1, 8, d), lambda i, j, k: (i, 0, 0)),
                  pl.BlockSpec((d, tf), lambda i, j, k: (0, k)),
                  pl.BlockSpec((tf, d), lambda i, j, k: (k, 0)),
                  pl.BlockSpec((1, d), lambda i, j, k: (0, 0)),
                  pl.BlockSpec((1, d), lambda i, j, k: (0, 0))],
        out_specs=pl.BlockSpec((1, tm, d), lambda i, j, k: (i, j, 0)),
        out_shape=jax.ShapeDtypeStruct((b, t, d), F32),
        scratch_shapes=[pltpu.VMEM((tm, d), F32), pltpu.VMEM((tm, d), BF16)],
        compiler_params=_cparams(("arbitrary", "arbitrary", "arbitrary")),
        name="ffn",
    )(x, mods, w1, w2, g, bb)


def _gla_out_kernel(o_ref, gate_ref, x_ref, m_ref, ng_ref, w_ref, g_ref, b_ref, out_ref):
    o = o_ref[0]
    ng = ng_ref[...]
    parts = []
    for h in range(GLA_HEADS):
        oh = o[:, h * GLA_HV:(h + 1) * GLA_HV]
        r = lax.rsqrt(jnp.mean(oh * oh, -1, keepdims=True) + RMS_EPS)
        parts.append(oh * r * ng)
    z = jnp.concatenate(parts, axis=-1) * _silu(gate_ref[0].astype(F32))
    y = _dot(z, w_ref[...])
    out_ref[0] = _res_ln(x_ref[0], m_ref[0][2:3], y, g_ref[...], b_ref[...])


def _gla_out(o, pmain, x, mods, ng, w, g, bb):
    b, t, d = x.shape
    tm = min(t, 512)
    gate_blk = (2 * GLA_DK) // GLA_DV + 1
    return pl.pallas_call(
        _gla_out_kernel,
        grid=(b, t // tm),
        in_specs=[pl.BlockSpec((1, tm, GLA_DV), lambda i, j: (i, j, 0)),
                  pl.BlockSpec((1, tm, GLA_DV), lambda i, j: (i, j, gate_blk)),
                  pl.BlockSpec((1, tm, d), lambda i, j: (i, j, 0)),
                  pl.BlockSpec((1, 8, d), lambda i, j: (i, 0, 0)),
                  pl.BlockSpec((1, GLA_HV), lambda i, j: (0, 0)),
                  pl.BlockSpec((GLA_DV, d), lambda i, j: (0, 0)),
                  pl.BlockSpec((1, d), lambda i, j: (0, 0)),
                  pl.BlockSpec((1, d), lambda i, j: (0, 0))],
        out_specs=pl.BlockSpec((1, tm, d), lambda i, j: (i, j, 0)),
        out_shape=jax.ShapeDtypeStruct((b, t, d), F32),
        compiler_params=_cparams(("arbitrary", "arbitrary")),
        name="gla_out",
    )(o, pmain, x, mods, ng, w, g, bb)


def _ssd_out_kernel(y_ref, z_ref, x_ref, m_ref, ng_ref, w_ref, g_ref, b_ref, out_ref):
    yz = y_ref[0] * _silu(z_ref[0].astype(F32))
    r = lax.rsqrt(jnp.mean(yz * yz, -1, keepdims=True) + RMS_EPS)
    y = _dot(yz * r * ng_ref[...], w_ref[...])
    out_ref[0] = _res_ln(x_ref[0], m_ref[0][2:3], y, g_ref[...], b_ref[...])


def _ssd_out(y, pmain, x, mods, ng, w, g, bb):
    b, t, d = x.shape
    tm = min(t, 512)
    return pl.pallas_call(
        _ssd_out_kernel,
        grid=(b, t // tm),
        in_specs=[pl.BlockSpec((1, tm, SSD_DI), lambda i, j: (i, j, 0)),
                  pl.BlockSpec((1, tm, SSD_DI), lambda i, j: (i, j, 0)),
                  pl.BlockSpec((1, tm, d), lambda i, j: (i, j, 0)),
                  pl.BlockSpec((1, 8, d), lambda i, j: (i, 0, 0)),
                  pl.BlockSpec((1, SSD_DI), lambda i, j: (0, 0)),
                  pl.BlockSpec((SSD_DI, d), lambda i, j: (0, 0)),
                  pl.BlockSpec((1, d), lambda i, j: (0, 0)),
                  pl.BlockSpec((1, d), lambda i, j: (0, 0))],
        out_specs=pl.BlockSpec((1, tm, d), lambda i, j: (i, j, 0)),
        out_shape=jax.ShapeDtypeStruct((b, t, d), F32),
        compiler_params=_cparams(("arbitrary", "arbitrary")),
        name="ssd_out",
    )(y, pmain, x, mods, ng, w, g, bb)


def _hy_out_kernel(z_ref, x_ref, m_ref, w_ref, g_ref, b_ref, out_ref):
    y = _dot(z_ref[0], w_ref[...])
    out_ref[0] = _res_ln(x_ref[0], m_ref[0][2:3], y, g_ref[...], b_ref[...])


def _hy_out(z, x, mods, w, g, bb):
    b, t, d = x.shape
    tm = min(t, 512)
    return pl.pallas_call(
        _hy_out_kernel,
        grid=(b, t // tm),
        in_specs=[pl.BlockSpec((1, tm, d), lambda i, j: (i, j, 0)),
                  pl.BlockSpec((1, tm, d), lambda i, j: (i, j, 0)),
                  pl.BlockSpec((1, 8, d), lambda i, j: (i, 0, 0)),
                  pl.BlockSpec((d, d), lambda i, j: (0, 0)),
                  pl.BlockSpec((1, d), lambda i, j: (0, 0)),
                  pl.BlockSpec((1, d), lambda i, j: (0, 0))],
        out_specs=pl.BlockSpec((1, tm, d), lambda i, j: (i, j, 0)),
        out_shape=jax.ShapeDtypeStruct((b, t, d), F32),
        compiler_params=_cparams(("arbitrary", "arbitrary")),
        name="hy_out",
    )(z, x, mods, w, g, bb)


def _gla_consts(fwd):
    c = GLA_CHUNK
    i = np.arange(c)
    tri = ((i[:, None] >= i[None, :]) if fwd else (i[:, None] <= i[None, :])).astype(np.float32)
    halves = [c >> (s + 1) for s in range(int(math.log2(c)))]
    nl = len(halves)
    fine = [lv for lv, half in enumerate(halves) if half < SUBLANE]
    wst = np.zeros(((len(fine) + 1) * c, c), np.float32)
    wst[:c] = tri
    negq = np.zeros((nl * c, GLA_HK), np.float32)
    negk = np.zeros((nl * c, GLA_HK), np.float32)
    msk = np.zeros(((nl + 1) * c, c), np.float32)
    msk[:c] = np.eye(c)
    for lv, half in enumerate(halves):
        blk = i // (2 * half)
        upper = (i % (2 * half)) >= half
        ref = blk * 2 * half + (half - 1 if fwd else half)
        if lv in fine:
            fi = fine.index(lv)
            wst[(fi + 1) * c:(fi + 2) * c] = tri - tri[ref]
        qside = upper if fwd else ~upper
        negq[lv * c:(lv + 1) * c] = np.where(qside, 0.0, GLA_NEG)[:, None]
        negk[lv * c:(lv + 1) * c] = np.where(~qside, 0.0, GLA_NEG)[:, None]
        msk[(lv + 1) * c:(lv + 2) * c] = ((blk[:, None] == blk[None, :]) & qside[:, None] & (~qside)[None, :])
    return (jnp.asarray(wst, BF16), jnp.asarray(negq), jnp.asarray(negk), jnp.asarray(msk))


def _gla_group(chains, eye):
    c = GLA_CHUNK
    for ch in chains:
        wst = ch["consts"][0]
        ch["est"] = _dot_split2_l(wst, ch["ga"])
    for ch in chains:
        _, negq, negk, _ = ch["consts"]
        q, k, est, fwd = ch["q"], ch["k"], ch["est"], ch["fwd"]
        nl = negq.shape[0] // c
        ncoarse = nl + 1 - est.shape[0] // c
        cum = est[0:c]
        tot = cum[c - 1:c] if fwd else cum[0:1]
        ch["qt"] = (q * jnp.exp2(cum)).astype(BF16)
        ch["kt"] = (k * jnp.exp2(tot - cum)).astype(BF16)
        ch["dec"] = jnp.exp2(tot)
        qs, ks = [q.astype(BF16)], [k.astype(BF16)]
        for lv in range(nl):
            if lv < ncoarse:
                size = c >> lv
                parts = []
                for lo in range(0, c, size):
                    ref = lo + size // 2 - (1 if fwd else 0)
                    parts.append(cum[lo:lo + size] - cum[ref:ref + 1])
                e = parts[0] if len(parts) == 1 else jnp.concatenate(parts, axis=0)
            else:
                e = est[(lv - ncoarse + 1) * c:(lv - ncoarse + 2) * c]
            sl = slice(lv * c, (lv + 1) * c)
            qs.append((q * jnp.exp2(e + negq[sl])).astype(BF16))
            ks.append((k * jnp.exp2(negk[sl] - e)).astype(BF16))
        ch["qs"], ch["ks"] = qs, ks
    for ch in chains:
        ch["ps"] = [_dot_nt(a, b) for a, b in zip(ch["qs"], ch["ks"])]
    for ch in chains:
        msk = ch["consts"][3]
        attn = None
        for lv, p in enumerate(ch["ps"]):
            term = p * msk[lv * c:(lv + 1) * c]
            attn = term if attn is None else attn + term
        ch["attn"] = attn.astype(BF16)
    row = lax.broadcasted_iota(jnp.int32, (2 * SUBLANE, 1), 0)
    for ch in chains:
        vb = ch["v"].astype(BF16)
        ch["o"] = jnp.dot(ch["attn"], vb, preferred_element_type=F32)
        dec = ch["dec"]
        hi = dec.astype(BF16).astype(F32)
        mid = (dec - hi).astype(BF16).astype(F32)
        extra = jnp.where(row == 0, hi, jnp.where(row == 1, mid, 0.0)).astype(BF16)
        kt_t = _dot_nt(eye, jnp.concatenate([ch["kt"], extra], axis=0))
        ch["dec_col"] = kt_t[:, c:c + 1] + kt_t[:, c + 1:c + 2]
        ch["upd"] = jnp.dot(kt_t[:, 0:c].astype(BF16), vb, preferred_element_type=F32)


def _dot_split2_l(m, x):
    hi = x.astype(BF16)
    mid = (x - hi.astype(F32)).astype(BF16)
    d = lambda p: jnp.dot(m, p, preferred_element_type=F32)
    return d(hi) + d(mid)


def _gla_kernel(q_ref, k_ref, v_ref, a_ref, w2_ref, b2_ref, s0_ref,
                wf_ref, nqf_ref, nkf_ref, mf_ref, wb_ref, nqb_ref, nkb_ref, mb_ref, eye_ref,
                o_ref, s_ref, gaf, gab, sf, sb, *, t):
    c = GLA_CHUNK
    nc = t // c
    un = min(GLA_UNROLL, nc)
    pb = min(t, 512)

    def prep(i, carry):
        r = pl.multiple_of(i * pb, pb)
        a = a_ref[0, pl.ds(r, pb), :]
        for z, ga in ((0, gaf), (1, gab)):
            logit = _dot_f32x3(a, w2_ref[z]) + b2_ref[z]
            ga[pl.ds(r, pb), :] = _log_sigmoid(logit) * (LOG2E / GLA_GATE_NORM)
        o_ref[0, pl.ds(r, pb), :] = jnp.zeros((pb, GLA_HV), F32)
        return carry

    lax.fori_loop(0, t // pb, prep, 0)
    sf[...] = s0_ref[0, 0, 0]
    sb[...] = s0_ref[0, 0, 1]
    scale = GLA_HK ** -0.5
    fconst = (wf_ref, nqf_ref, nkf_ref, mf_ref)
    bconst = (wb_ref, nqb_ref, nkb_ref, mb_ref)

    def body(ci, carry):
        chains = []
        for fwd, ga, consts in ((True, gaf, fconst), (False, gab, bconst)):
            cvals = tuple(x[...] for x in consts)
            for u in range(un):
                idx = ci * un + u
                r = pl.multiple_of((idx if fwd else nc - 1 - idx) * c, c)
                chains.append(dict(r=r, fwd=fwd, consts=cvals, ga=ga[pl.ds(r, c), :],
                                   q=q_ref[0, pl.ds(r, c), :].astype(F32) * scale,
                                   k=k_ref[0, pl.ds(r, c), :].astype(F32), v=v_ref[0, pl.ds(r, c), :]))
        _gla_group(chains, eye_ref[...])
        for fwd, st_ref in ((True, sf), (False, sb)):
            st = st_ref[...]
            for ch in chains:
                if ch["fwd"] != fwd:
                    continue
                o = ch["o"] + jnp.dot(ch["qt"], st.astype(BF16), preferred_element_type=F32)
                o_ref[0, pl.ds(ch["r"], c), :] += o
                st = st * ch["dec_col"] + ch["upd"]
            st_ref[...] = st
        return carry

    lax.fori_loop(0, nc // un, body, 0)
    s_ref[0, 0, 0] = sf[...]
    s_ref[0, 0, 1] = sb[...]


def _gla_scan(pmain, pa, w2p, b2p, s0):
    b, t, _ = pmain.shape
    h = GLA_HEADS
    assert t % (GLA_CHUNK * min(GLA_UNROLL, t // GLA_CHUNK)) == 0
    consts = _gla_consts(True) + _gla_consts(False) + (jnp.asarray(np.eye(GLA_HK), BF16),)
    kblk = GLA_DK // GLA_HK
    vblk = (2 * GLA_DK) // GLA_HV
    cspec = lambda a: pl.BlockSpec(a.shape, lambda i, j: (0,) * a.ndim)
    return pl.pallas_call(
        functools.partial(_gla_kernel, t=t),
        grid=(b, h),
        in_specs=[pl.BlockSpec((1, t, GLA_HK), lambda i, j: (i, 0, j)),
                  pl.BlockSpec((1, t, GLA_HK), lambda i, j: (i, 0, kblk + j)),
                  pl.BlockSpec((1, t, GLA_HV), lambda i, j: (i, 0, vblk + j)),
                  pl.BlockSpec((1, t, LANE), lambda i, j: (i, 0, 0)),
                  pl.BlockSpec((2, LANE, GLA_HK), lambda i, j: (0, 0, j)),
                  pl.BlockSpec((2, 1, GLA_HK), lambda i, j: (0, 0, j)),
                  pl.BlockSpec((1, 1, 2, GLA_HK, GLA_HV), lambda i, j: (i, j, 0, 0, 0))]
                 + [cspec(a) for a in consts],
        out_specs=[pl.BlockSpec((1, t, GLA_HV), lambda i, j: (i, 0, j)),
                   pl.BlockSpec((1, 1, 2, GLA_HK, GLA_HV), lambda i, j: (i, j, 0, 0, 0))],
        out_shape=[jax.ShapeDtypeStruct((b, t, GLA_DV), F32),
                   jax.ShapeDtypeStruct((b, h, 2, GLA_HK, GLA_HV), F32)],
        scratch_shapes=[pltpu.VMEM((t, GLA_HK), F32), pltpu.VMEM((t, GLA_HK), F32),
                        pltpu.VMEM((GLA_HK, GLA_HV), F32), pltpu.VMEM((GLA_HK, GLA_HV), F32)],
        compiler_params=_cparams(("arbitrary", "arbitrary")),
        name="gla_scan",
    )(pmain, pmain, pmain, pa, w2p, b2p, s0, *consts)


def _ssd_consts():
    c = SSD_CHUNK
    i = np.arange(c)
    tril = (i[:, None] >= i[None, :]).astype(np.float32)
    triu = (i[:, None] <= i[None, :]).astype(np.float32)
    eye = np.eye(SSD_STATE, dtype=np.float32)
    sel16 = np.eye(16, LANE, dtype=np.float32)
    e4 = np.zeros((2, LANE, SSD_GW), np.float32)
    for z in range(2):
        for r in range(SSD_REP):
            e4[z, z * SSD_REP + r, r * SSD_HEADDIM:(r + 1) * SSD_HEADDIM] = 1.0
    lm = np.zeros((SSD_REP * c, SSD_GW), np.float32)
    for r in range(SSD_REP):
        lm[r * c:(r + 1) * c, r * SSD_HEADDIM:(r + 1) * SSD_HEADDIM] = 1.0
    return (jnp.asarray(tril, BF16), jnp.asarray(triu, BF16), jnp.asarray(tril), jnp.asarray(triu),
            jnp.asarray(eye, BF16), jnp.asarray(sel16, BF16), jnp.asarray(e4, BF16), jnp.asarray(lm))


def _conv_block(in_ref, w, bias, i, nblk, rows, t, taps):
    halo = 2 * SUBLANE
    r = pl.multiple_of(i * rows, rows)
    rp = pl.multiple_of(jnp.maximum(r - halo, 0), halo)
    rn = pl.multiple_of(jnp.minimum(r + rows, t - halo), halo)
    half = taps // 2
    outs = []
    for c0 in range(0, in_ref.shape[2], LANE):
        sl = slice(c0, c0 + LANE)
        cur = in_ref[0, pl.ds(r, rows), sl].astype(F32)
        prev = jnp.where(i > 0, in_ref[0, pl.ds(rp, halo), sl].astype(F32), 0.0)
        nxt = jnp.where(i < nblk - 1, in_ref[0, pl.ds(rn, halo), sl].astype(F32), 0.0)
        ext = jnp.concatenate([prev, cur, nxt], axis=0)
        acc = bias[:, sl]
        for j in range(taps):
            off = halo - half + j
            acc = acc + w[j:j + 1, sl] * ext[off:off + rows]
        outs.append(acc)
    return r, (outs[0] if len(outs) == 1 else jnp.concatenate(outs, axis=1))


def _ssd_group(chains, eye, sel16, lm):
    c = SSD_CHUNK
    for ch in chains:
        ch["cum"] = _dot_exact_l(ch["tri"], ch["das"])
        ch["dt_e"] = _dot_split2_r(ch["ds"], ch["e4z"])
        ch["ccb"], bcb = ch["cc"].astype(BF16), ch["bc"].astype(BF16)
        ch["cb"] = _dot_nt(ch["ccb"], bcb)
        ch["bc_t"] = _dot_nt(eye, bcb).astype(BF16)
    for ch in chains:
        ch["cum_t"] = _dot_exact_nt(sel16, ch["cum"])
        ch["cum_e"] = _dot_split2_r(ch["cum"], ch["e4z"])
    for ch in chains:
        cum, cum_t, cum_e = ch["cum"], ch["cum_t"], ch["cum_e"]
        tot_e = cum_e[c - 1:c] if ch["fwd"] else cum_e[0:1]
        xdt = ch["xg"] * ch["dt_e"]
        cb = ch["cb"] * ch["mask"]
        ms = []
        for r in range(SSD_REP):
            q = ch["z"] * SSD_REP + r
            seg = cum[:, q:q + 1] - cum_t[q:q + 1, :]
            ms.append((cb * jnp.exp2(jnp.minimum(seg, 0.0))).astype(BF16))
        ch["mcat"] = jnp.concatenate(ms, axis=1)
        ch["xbd"] = (jnp.concatenate([xdt] * SSD_REP, axis=0) * lm).astype(BF16)
        ch["w"] = (xdt * jnp.exp2(tot_e - cum_e)).astype(BF16)
        ch["dece"] = jnp.exp2(cum_e)
        ch["dec"] = jnp.exp2(tot_e)
    for ch in chains:
        ch["y"] = jnp.dot(ch["mcat"], ch["xbd"], preferred_element_type=F32)
        ch["upd"] = jnp.dot(ch["bc_t"], ch["w"], preferred_element_type=F32)


def _ssd_kernel(x_ref, bm_ref, cm_ref, dt_ref, wx_ref, wb_ref, wc_ref, bx_ref, bb_ref, bcb_ref,
                dtb_ref, alog_ref, dsk_ref, s0_ref,
                tl_ref, tu_ref, ml_ref, mu_ref, eye_ref, sel16_ref, e4_ref, lm_ref,
                y_ref, s_ref, xc, bcs, ccs, dsel, dasel, sf, sb, *, t):
    c = SSD_CHUNK
    nc = t // c
    un = min(SSD_UNROLL, nc)
    pb = c
    nblk = t // pb

    shift = (LANE - 2 * SSD_REP * pl.program_id(1)) % LANE

    def prep(i):
        for in_ref, w_ref, b_ref, out in ((x_ref, wx_ref, bx_ref, xc), (bm_ref, wb_ref, bb_ref, bcs),
                                          (cm_ref, wc_ref, bcb_ref, ccs)):
            r, acc = _conv_block(in_ref, w_ref[...], b_ref[...], i, nblk, pb, t, SSD_CONV)
            val = _silu(acc)
            out[pl.ds(r, pb), :] = val
            if out is xc:
                y_ref[0, pl.ds(r, pb), :] = val * dsk_ref[...]
        r = pl.multiple_of(i * pb, pb)
        dtp = _softplus(dt_ref[0, pl.ds(r, pb), :] + dtb_ref[...])
        a2 = jnp.exp(alog_ref[...]) * (-LOG2E)
        dsel[pl.ds(r, pb), :] = pltpu.roll(dtp, shift, 1)
        dasel[pl.ds(r, pb), :] = pltpu.roll(dtp * a2, shift, 1)

    for blk in sorted(set(range(un)) | set(nc - 1 - u for u in range(un))):
        prep(jnp.int32(blk))
    sf[...] = s0_ref[0, 0, 0]
    sb[...] = s0_ref[0, 0, 1]
    n_ahead = max(nc // (2 * un) - 1, 0)

    def body(ci, carry, ahead):
        chains = []
        for fwd, z, tri_ref, mk_ref in ((True, 0, tl_ref, ml_ref), (False, 1, tu_ref, mu_ref)):
            tri, mask, e4z = tri_ref[...], mk_ref[...], e4_ref[z]
            for u in range(un):
                idx = ci * un + u
                r = pl.multiple_of((idx if fwd else nc - 1 - idx) * c, c)
                chains.append(dict(r=r, fwd=fwd, z=z, tri=tri, mask=mask, e4z=e4z,
                                   xg=xc[pl.ds(r, c), :], bc=bcs[pl.ds(r, c), :], cc=ccs[pl.ds(r, c), :],
                                   ds=dsel[pl.ds(r, c), :], das=dasel[pl.ds(r, c), :]))
        if ahead:
            for u in range(un):
                nxt = (ci + 1) * un + u
                prep(nxt)
                prep(nc - 1 - nxt)
        _ssd_group(chains, eye_ref[...], sel16_ref[...], lm_ref[...])
        for fwd, st_ref in ((True, sf), (False, sb)):
            st = st_ref[...]
            for ch in chains:
                if ch["fwd"] != fwd:
                    continue
                y = ch["y"] + jnp.dot(ch["ccb"], st.astype(BF16), preferred_element_type=F32) * ch["dece"]
                y_ref[0, pl.ds(ch["r"], c), :] += y
                st = st * ch["dec"] + ch["upd"]
            st_ref[...] = st
        return carry

    lax.fori_loop(0, n_ahead, functools.partial(body, ahead=True), 0)
    lax.fori_loop(n_ahead, nc // un, functools.partial(body, ahead=False), 0)
    s_ref[0, 0, 0] = sf[...]
    s_ref[0, 0, 1] = sb[...]


def _ssd_scan(pmain, pdt, conv_w, conv_b, dtb, alog, dskip, s0):
    b, t, _ = pmain.shape
    g = SSD_GROUPS
    assert t % (SSD_CHUNK * min(SSD_UNROLL, t // SSD_CHUNK)) == 0
    tril, triu, mtril, mtriu, eye, sel16, e4, lm = _ssd_consts()
    xblk = SSD_DI // SSD_GW
    bblk = (2 * SSD_DI) // SSD_STATE
    cblk = bblk + SSD_GN // SSD_STATE
    wbblk = SSD_DI // SSD_STATE
    wcblk = wbblk + SSD_GN // SSD_STATE
    cspec = lambda a: pl.BlockSpec(a.shape, lambda i, j: (0,) * a.ndim)
    return pl.pallas_call(
        functools.partial(_ssd_kernel, t=t),
        grid=(b, g),
        in_specs=[pl.BlockSpec((1, t, SSD_GW), lambda i, j: (i, 0, xblk + j)),
                  pl.BlockSpec((1, t, SSD_STATE), lambda i, j: (i, 0, bblk + j)),
                  pl.BlockSpec((1, t, SSD_STATE), lambda i, j: (i, 0, cblk + j)),
                  pl.BlockSpec((1, t, LANE), lambda i, j: (i, 0, 0)),
                  pl.BlockSpec((SSD_CONV, SSD_GW), lambda i, j: (0, j)),
                  pl.BlockSpec((SSD_CONV, SSD_STATE), lambda i, j: (0, wbblk + j)),
                  pl.BlockSpec((SSD_CONV, SSD_STATE), lambda i, j: (0, wcblk + j)),
                  pl.BlockSpec((1, SSD_GW), lambda i, j: (0, j)),
                  pl.BlockSpec((1, SSD_STATE), lambda i, j: (0, wbblk + j)),
                  pl.BlockSpec((1, SSD_STATE), lambda i, j: (0, wcblk + j)),
                  pl.BlockSpec((1, LANE), lambda i, j: (0, 0)),
                  pl.BlockSpec((1, LANE), lambda i, j: (0, 0)),
                  pl.BlockSpec((1, SSD_GW), lambda i, j: (0, j)),
                  pl.BlockSpec((1, 1, 2, SSD_STATE, SSD_GW), lambda i, j: (i, j, 0, 0, 0)),
                  cspec(tril), cspec(triu), cspec(mtril), cspec(mtriu), cspec(eye), cspec(sel16),
                  cspec(e4), cspec(lm)],
        out_specs=[pl.BlockSpec((1, t, SSD_GW), lambda i, j: (i, 0, j)),
                   pl.BlockSpec((1, 1, 2, SSD_STATE, SSD_GW), lambda i, j: (i, j, 0, 0, 0))],
        out_shape=[jax.ShapeDtypeStruct((b, t, SSD_DI), F32),
                   jax.ShapeDtypeStruct((b, g, 2, SSD_STATE, SSD_GW), F32)],
        scratch_shapes=[pltpu.VMEM((t, SSD_GW), F32), pltpu.VMEM((t, SSD_STATE), F32),
                        pltpu.VMEM((t, SSD_STATE), F32), pltpu.VMEM((t, LANE), F32),
                        pltpu.VMEM((t, LANE), F32),
                        pltpu.VMEM((SSD_STATE, SSD_GW), F32), pltpu.VMEM((SSD_STATE, SSD_GW), F32)],
        compiler_params=_cparams(("arbitrary", "arbitrary")),
        name="ssd_scan",
    )(pmain, pmain, pmain, pdt, conv_w, conv_w, conv_w, conv_b, conv_b, conv_b,
      dtb, alog, dskip, s0, tril, triu, mtril, mtriu, eye, sel16, e4, lm)


def _conv3_kernel(p_ref, w_ref, b_ref, *out_refs, t):
    pb = min(t, 256)
    nblk = t // pb

    def blk(i, carry):
        r, acc = _conv_block(p_ref, w_ref[...], b_ref[...], i, nblk, pb, t, HY_SHORT)
        for o_ref in out_refs:
            o_ref[0, pl.ds(r, pb), :] = acc.astype(o_ref.dtype)
        return carry

    lax.fori_loop(0, nblk, blk, 0)


def _conv3(p, w, bias, col0, ncols, dtypes):
    b, t, _ = p.shape
    tn = 256
    off = col0 // tn
    out_shape = [jax.ShapeDtypeStruct((b, t, ncols), dt) for dt in dtypes]
    out_specs = [pl.BlockSpec((1, t, tn), lambda i, j: (i, 0, j)) for _ in dtypes]
    return pl.pallas_call(
        functools.partial(_conv3_kernel, t=t),
        grid=(b, ncols // tn),
        in_specs=[pl.BlockSpec((1, t, tn), lambda i, j: (i, 0, off + j)),
                  pl.BlockSpec((HY_SHORT, tn), lambda i, j: (0, off + j)),
                  pl.BlockSpec((1, tn), lambda i, j: (0, off + j))],
        out_specs=out_specs,
        out_shape=out_shape,
        compiler_params=_cparams(("arbitrary", "arbitrary")),
        name="hy_conv3",
    )(p, w, bias)


HY_TN = 512


HY_TIME_LANE = HY_FW


def _filt_trunk_kernel(w1_ref, b1_ref, w2_ref, b2_ref, w3_ref, b3_ref, fr_ref, o_ref, *, l, tl):
    row = lax.broadcasted_iota(jnp.int32, (tl, 1), 0) + pl.program_id(1) * tl
    pos = jnp.where(pl.program_id(0) == 0, row, l - row).astype(F32)
    tt = pos * (1.0 / (l - 1))
    bands = (HY_EMB - 1) // 2
    lane = lax.broadcasted_iota(jnp.int32, (1, LANE), 1)
    band = ((lane - 1) & (bands - 1)).astype(F32) * ((bands - 1 - 1e-4) / (bands - 1)) + 1e-4
    ang = (pos * (2 * math.pi / l)) * band
    z = jnp.where(lane == 0, tt,
                  jnp.where(lane <= bands, jnp.cos(ang), jnp.where(lane <= 2 * bands, -jnp.sin(ang), 0.0)))
    fr = fr_ref[...]
    h = jnp.sin(fr * (_dot_f32x3(z, w1_ref[...]) + b1_ref[...]))
    h = jnp.sin(fr * (_dot_f32x3(h, w2_ref[...]) + b2_ref[...]))
    h = jnp.sin(fr * (_dot_f32x3(h, w3_ref[...]) + b3_ref[...]))
    o_ref[0] = jnp.where(lane == HY_TIME_LANE, tt, h)


def _filt_final_kernel(h_ref, w4_ref, dl_ref, o_ref, *, radix, tl):
    h = h_ref[0]
    tt = h[:, HY_TIME_LANE:HY_TIME_LANE + 1]
    row = lax.broadcasted_iota(jnp.int32, (tl, 1), 0) + pl.program_id(1) * tl
    valid = jnp.logical_or(pl.program_id(0) < 2 * radix - 1, row > 0)
    hh = _dot_f32x3(h, w4_ref[...]) * jnp.exp(-tt * dl_ref[...])
    o_ref[0] = jnp.where(valid, hh, 0.0)


def _hy_filters(l, lp, radix, w1, b1, w2, b2, w3, b3, w4, freq):
    d = D_MODEL
    tl = min(lp, 512)
    tn = HY_TN
    nd = d // tn
    nsub = 2 * radix - 1
    nrb = lp // tl
    assert (HY_EMB - 1) // 2 == 16
    pad2 = lambda a: jnp.zeros((LANE, LANE), F32).at[:a.shape[0], :a.shape[1]].set(a)
    pad1 = lambda a: jnp.zeros((1, LANE), F32).at[0, :a.shape[0]].set(a)
    w4p = jnp.zeros((LANE, HY_ORDER * 2 * d), F32).at[:HY_FW].set(w4)
    deltas = np.abs(np.linspace(math.log(HY_DECAY_TARGET) / HY_FAST_DECAY,
                                math.log(HY_DECAY_TARGET) / HY_SLOW_DECAY, d))
    dl = jnp.asarray(deltas[None, :], F32)
    small = [pad2(w1), pad1(b1), pad2(w2), pad1(b2), pad2(w3), pad1(b3), pad1(freq)]
    trunk = pl.pallas_call(
        functools.partial(_filt_trunk_kernel, l=l, tl=tl),
        grid=(2, l // tl),
        in_specs=[pl.BlockSpec(a.shape, lambda s, m: (0, 0)) for a in small],
        out_specs=pl.BlockSpec((1, tl, LANE), lambda s, m: (s, m, 0)),
        out_shape=jax.ShapeDtypeStruct((2, l, LANE), F32),
        compiler_params=_cparams(("arbitrary", "arbitrary")),
        name="hy_filt_trunk",
    )(*small)

    def lag_is_negative(bi):
        didx = bi % nsub
        return jnp.where(bi < nsub, didx < radix - 1, didx <= radix - 1)

    def h_map(bi, m, j):
        didx = bi % nsub
        base = jnp.abs(didx - (radix - 1)) * nrb
        desc = jnp.where(bi < nsub, didx < radix - 1, didx >= radix)
        return (desc.astype(jnp.int32), jnp.where(desc, radix * nrb - base, base) + m, 0)

    def w4_map(bi, m, j):
        return (0, ((j // nd) * 2 + lag_is_negative(bi).astype(jnp.int32)) * nd + j % nd)

    return pl.pallas_call(
        functools.partial(_filt_final_kernel, radix=radix, tl=tl),
        grid=(2 * nsub, nrb, HY_ORDER * nd),
        in_specs=[pl.BlockSpec((1, tl, LANE), h_map), pl.BlockSpec((LANE, tn), w4_map),
                  pl.BlockSpec((1, tn), lambda bi, m, j: (0, j % nd))],
        out_specs=pl.BlockSpec((1, tl, tn), lambda bi, m, j: (bi // nsub, m, j * nsub + bi % nsub)),
        out_shape=jax.ShapeDtypeStruct((2, lp, HY_ORDER * nsub * d), F32),
        compiler_params=_cparams(("arbitrary", "arbitrary", "arbitrary")),
        name="hy_filt",
    )(trunk, w4p, dl)


def _dft_tables_np(l):
    n = 2 * l
    k = np.arange(l)
    ang = 2 * math.pi * ((k[:, None] * k[None, :]) % n) / n
    cm = np.cos(ang)
    sm = -np.sin(ang)
    alt = np.where(k % 2 == 0, 1.0, -1.0)
    s_fwd = sm.copy()
    s_fwd[0, :] = alt
    s_inv = sm.copy()
    s_inv[:, 0] = alt
    return cm, s_fwd, s_inv


def _dft_tab_kernel(ca_ref, sa_ref, cb_ref, sb_ref, c_ref, sf_ref, si_ref, *, rb):
    a = pl.program_id(0)
    ca, sa = ca_ref[0], sa_ref[0]
    cb, sb = cb_ref[...], sb_ref[...]
    cm = ca * cb - sa * sb
    sm = -(sa * cb + ca * sb)
    l = cm.shape[1]
    row = lax.broadcasted_iota(jnp.int32, (rb, l), 0)
    col = lax.broadcasted_iota(jnp.int32, (rb, l), 1)
    alt_col = jnp.where(col % 2 == 0, 1.0, -1.0)
    alt_row = jnp.where(row % 2 == 0, 1.0, -1.0)
    c_ref[...] = cm.astype(BF16)
    sf_ref[...] = jnp.where(jnp.logical_and(row == 0, a == 0), alt_col, sm).astype(BF16)
    si_ref[...] = jnp.where(col == 0, alt_row, sm).astype(BF16)


def _dft_tables(l):
    if l <= 512:
        return tuple(jnp.asarray(m, BF16) for m in _dft_tables_np(l))
    rb = 64
    na = l // rb
    n = 2 * l
    nn = np.arange(l)
    aa = np.arange(na)
    bb = np.arange(rb)
    ang_a = 2 * math.pi * ((aa[:, None] * rb * nn[None, :]) % n) / n
    ang_b = 2 * math.pi * ((bb[:, None] * nn[None, :]) % n) / n
    ca = jnp.asarray(np.cos(ang_a)[:, None, :], F32)
    sa = jnp.asarray(np.sin(ang_a)[:, None, :], F32)
    cb = jnp.asarray(np.cos(ang_b), F32)
    sb = jnp.asarray(np.sin(ang_b), F32)
    rowspec = pl.BlockSpec((1, 1, l), lambda a: (a, 0, 0))
    tabspec = pl.BlockSpec((rb, l), lambda a: (0, 0))
    outspec = pl.BlockSpec((rb, l), lambda a: (a, 0))
    return tuple(pl.pallas_call(
        functools.partial(_dft_tab_kernel, rb=rb),
        grid=(na,),
        in_specs=[rowspec, rowspec, tabspec, tabspec],
        out_specs=[outspec] * 3,
        out_shape=[jax.ShapeDtypeStruct((l, l), BF16)] * 3,
        compiler_params=_cparams(("arbitrary",)),
        name="dft_tables",
    )(ca, sa, cb, sb))


def _spec_filt_kernel(c_ref, s_ref, a_ref, b_ref, h_ref, accr, acci, accn, *, l):
    kk = pl.program_id(2)

    @pl.when(kk == 0)
    def _():
        accr[...] = jnp.zeros_like(accr)
        acci[...] = jnp.zeros_like(acci)
        accn[...] = jnp.zeros_like(accn)

    a, bw = a_ref[0], b_ref[0]
    sm = (a + bw).astype(BF16)
    df = (a - bw).astype(BF16)
    accr[...] += jnp.dot(c_ref[...], sm, preferred_element_type=F32)
    acci[...] += jnp.dot(s_ref[...], df, preferred_element_type=F32)
    accn[...] += jnp.dot(s_ref[...], sm, preferred_element_type=F32)

    @pl.when(kk == pl.num_programs(2) - 1)
    def _():
        tm = accr.shape[0]
        row0 = (lax.broadcasted_iota(jnp.int32, (tm, 1), 0) + pl.program_id(0) * tm) == 0
        scale = jnp.where(row0, 0.5 / l, 1.0 / l)
        h_ref[0] = accr[...] * scale
        h_ref[1] = jnp.where(row0, accn[...], acci[...]) * scale


def _spec_filt(cm, s_fwd, fab, l):
    ncol = fab.shape[2]
    tm, tn, tk = min(l, 512), HY_TN, min(l, 1024)
    return pl.pallas_call(
        functools.partial(_spec_filt_kernel, l=l),
        grid=(l // tm, ncol // tn, l // tk),
        in_specs=[pl.BlockSpec((tm, tk), lambda m, j, k: (m, k)),
                  pl.BlockSpec((tm, tk), lambda m, j, k: (m, k)),
                  pl.BlockSpec((1, tk, tn), lambda m, j, k: (0, k, j)),
                  pl.BlockSpec((1, tk, tn), lambda m, j, k: (1, k, j))],
        out_specs=pl.BlockSpec((2, tm, tn), lambda m, j, k: (0, m, j)),
        out_shape=jax.ShapeDtypeStruct((2, l, ncol), F32),
        scratch_shapes=[pltpu.VMEM((tm, tn), F32)] * 3,
        compiler_params=_cparams(("arbitrary", "arbitrary", "arbitrary")),
        name="hy_spec_filt",
    )(cm, s_fwd, fab, fab)


def _seg_spec_kernel(c_ref, s_ref, *refs, radix):
    u_refs, g_ref, z_ref = refs[:radix], refs[radix], refs[radix + 1]
    tm, tn = z_ref.shape[2], z_ref.shape[3]
    spec = []
    for q in range(radix):
        u = u_refs[q][0]
        spec.append((jnp.dot(c_ref[...], u, preferred_element_type=F32),
                     jnp.dot(s_ref[...], u, preferred_element_type=F32)))
    row0 = (lax.broadcasted_iota(jnp.int32, (tm, 1), 0) + pl.program_id(1) * tm) == 0
    for t in range(radix):
        zr = zi = None
        for q in range(radix):
            di = t - q + radix - 1
            gr = g_ref[0, :, di * tn:(di + 1) * tn]
            gi = g_ref[1, :, di * tn:(di + 1) * tn]
            xr, xi = spec[q]
            tr = xr * gr - jnp.where(row0, 0.0, xi * gi)
            ti = jnp.where(row0, xi * gi, xr * gi + xi * gr)
            zr = tr if zr is None else zr + tr
            zi = ti if zi is None else zi + ti
        z_ref[t, 0] = zr.astype(BF16)
        z_ref[t, 1] = zi.astype(BF16)


def _seg_spec(cm, s_fwd, ub, gspec, order, radix):
    b, l, d = ub.shape
    lp = l // radix
    tm, tn = min(lp, 256), HY_TN
    nd = d // tn
    nsub = 2 * radix - 1
    u_specs = [pl.BlockSpec((1, lp, tn), functools.partial(lambda n, m, i, q: (i, q, n), q=q)) for q in range(radix)]
    return pl.pallas_call(
        functools.partial(_seg_spec_kernel, radix=radix),
        grid=(nd, lp // tm, b),
        in_specs=[pl.BlockSpec((tm, lp), lambda n, m, i: (m, 0)),
                  pl.BlockSpec((tm, lp), lambda n, m, i: (m, 0))] + u_specs
                 + [pl.BlockSpec((2, tm, nsub * tn), lambda n, m, i: (0, m, order * nd + n))],
        out_specs=pl.BlockSpec((radix, 2, tm, tn), lambda n, m, i: (0, 0, m, i * nd + n)),
        out_shape=jax.ShapeDtypeStruct((radix, 2, lp, b * d), BF16),
        compiler_params=_cparams(("arbitrary", "arbitrary", "arbitrary")),
        name="hy_seg_spec",
    )(cm, s_fwd, *([ub] * radix), gspec)


def _seg_inv_kernel(c_ref, s_ref, z_ref, u_ref, g_ref, bias_ref, *out_refs):
    conv = (jnp.dot(c_ref[...], z_ref[0, 0], preferred_element_type=F32)
            + jnp.dot(s_ref[...], z_ref[0, 1], preferred_element_type=F32))
    res = g_ref[0].astype(F32) * (conv + u_ref[0] * bias_ref[...])
    for o_ref in out_refs:
        o_ref[0] = res.astype(o_ref.dtype)


def _seg_inv(cm, s_inv, zspec, u, gate, gsel, bias, dtypes, radix):
    b, l, d = u.shape
    lp = l // radix
    tt, tn = min(lp, 512), HY_TN
    nd = d // tn
    nt = lp // tt
    blk = pl.BlockSpec((1, tt, tn), lambda t, j, r: (j // nd, r * nt + t, j % nd))
    out_shape = [jax.ShapeDtypeStruct((b, l, d), dt) for dt in dtypes]
    out_specs = [blk for _ in dtypes]
    return pl.pallas_call(
        _seg_inv_kernel,
        grid=(nt, b * nd, radix),
        in_specs=[pl.BlockSpec((tt, lp), lambda t, j, r: (t, 0)),
                  pl.BlockSpec((tt, lp), lambda t, j, r: (t, 0)),
                  pl.BlockSpec((1, 2, lp, tn), lambda t, j, r: (r, 0, 0, j)),
                  blk,
                  pl.BlockSpec((1, tt, tn), lambda t, j, r: (j // nd, r * nt + t, gsel * nd + j % nd)),
                  pl.BlockSpec((1, tn), lambda t, j, r: (0, j % nd))],
        out_specs=out_specs,
        out_shape=out_shape,
        compiler_params=_cparams(("arbitrary", "arbitrary", "arbitrary")),
        name="hy_seg_inv",
    )(cm, s_inv, zspec, u, gate, bias)


def _hyena_run(p, conv_w, conv_b, fw, h_bias):
    b, l, _ = p.shape
    d = D_MODEL
    radix = max(1, min(HY_RADIX, l // HY_MIN_BLOCK_LEN))
    lp = l // radix
    v, vb = _conv3(p, conv_w, conv_b, 0, d, (F32, BF16))
    x12 = _conv3(p, conv_w, conv_b, d, 2 * d, (BF16,))[0]
    cm, s_fwd, s_inv = _dft_tables(lp)
    gspec = _spec_filt(cm, s_fwd, _hy_filters(l, lp, radix, *fw), lp)
    z1 = _seg_spec(cm, s_fwd, vb, gspec, 0, radix)
    z, zb = _seg_inv(cm, s_inv, z1, v, x12, 0, h_bias[0:1], (F32, BF16), radix)
    z2 = _seg_spec(cm, s_fwd, zb, gspec, 1, radix)
    return _seg_inv(cm, s_inv, z2, z, x12, 1, h_bias[1:2], (BF16,), radix)[0]


def _snake_kernel(x_ref, j_ref, o_ref):
    jm = j_ref[...]
    for g in range(x_ref.shape[1] // (2 * GRID_W)):
        r0 = g * 2 * GRID_W
        o_ref[0, r0:r0 + GRID_W, :] = x_ref[0, r0:r0 + GRID_W, :]
        o_ref[0, r0 + GRID_W:r0 + 2 * GRID_W, :] = _dot_exact_l(jm, x_ref[0, r0 + GRID_W:r0 + 2 * GRID_W, :])


def _snake(h):
    b, l, ch = h.shape
    tm = 512
    jm = jnp.asarray(np.eye(GRID_W)[::-1].copy(), BF16)
    return pl.pallas_call(
        _snake_kernel,
        grid=(b, l // tm),
        in_specs=[pl.BlockSpec((1, tm, ch), lambda i, j: (i, j, 0)),
                  pl.BlockSpec((GRID_W, GRID_W), lambda i, j: (0, 0))],
        out_specs=pl.BlockSpec((1, tm, ch), lambda i, j: (i, j, 0)),
        out_shape=jax.ShapeDtypeStruct((b, l, ch), F32),
        compiler_params=_cparams(("arbitrary", "arbitrary")),
        name="snake",
    )(h, jm)


def _pad_cols(w, n):
    return jnp.zeros((w.shape[0], n), w.dtype).at[:, :w.shape[1]].set(w)


def kernel(x, c, ctx, c_ctx, ada_w, ada_b, ln_g, ln_b, ffn_w1, ffn_w2, gla_w_in, gla_w_a2, gla_b_a2, gla_norm, gla_w_out, ssd_w_in, ssd_conv_w, ssd_conv_b, ssd_dt_bias, ssd_a_log, ssd_d, ssd_norm, ssd_w_out, hy_w_in, hy_conv_w, hy_conv_b, hy_f_w1, hy_f_b1, hy_f_w2, hy_f_b2, hy_f_w3, hy_f_b3, hy_f_w4, hy_f_freq, hy_bias, hy_w_out):
    bsz, _, d = x.shape
    hl = _snake(x)
    hc = ctx.reshape(1, -1, d)
    cvec = jnp.zeros((16, d), F32).at[:bsz].set(c).at[bsz].set(c_ctx)
    mods = _ada(cvec, ada_w, ada_b).reshape(DEPTH, 16, 6, d)
    mods = jnp.concatenate([mods, jnp.zeros((DEPTH, 16, 2, d), F32)], axis=2)
    per_batch = lambda a: a.reshape(bsz, -1, a.shape[-1])
    like = lambda a, h: a.reshape(h.shape[0], -1, a.shape[-1])

    for i in range(DEPTH):
        kind, j = i % N_MIXERS, i // N_MIXERS
        need_ctx = i < DEPTH - 1
        ml = mods[i, :bsz]
        mc = mods[i, bsz][None]
        g0, b0 = ln_g[i, 0][None], ln_b[i, 0][None]
        g1, b1 = ln_g[i, 1][None], ln_b[i, 1][None]
        w1 = ffn_w1[i].astype(BF16)
        w2 = ffn_w2[i].astype(BF16)
        streams = [(hc, mc, True), (hl, ml, False)]
        if kind == 0:
            w_main = gla_w_in[j][:, :GLA_MAIN].astype(BF16)
            w_a = _pad_cols(gla_w_in[j][:, GLA_MAIN:], LANE).astype(BF16)
            w2p = jnp.zeros((2, LANE, GLA_DK), F32)
            for z in range(2):
                w2p = w2p.at[z, z * GLA_RANK:(z + 1) * GLA_RANK].set(gla_w_a2[j, z])
            b2p = gla_b_a2[j][:, None, :]
            w_out = gla_w_out[j].astype(BF16)
            ng = gla_norm[j][None]
            state = jnp.zeros((bsz, GLA_HEADS, 2, GLA_HK, GLA_HV), F32)
            new = []
            for h, m, is_ctx in streams:
                pmain = _proj(h, m, w_main, 1024, BF16)
                pa = _proj(h, m, w_a, LANE, F32)
                o, st = _gla_scan(per_batch(pmain), per_batch(pa), w2p, b2p, state)
                if is_ctx:
                    state = st
                if is_ctx and not need_ctx:
                    new.append(h)
                    continue
                new.append(_gla_out(like(o, h), pmain, h, m, ng, w_out, g0, b0))
            hc, hl = new
        elif kind == 1:
            perm = np.arange(2 * SSD_HEADS).reshape(2, SSD_GROUPS, SSD_REP).transpose(1, 0, 2).reshape(-1)
            w_main = ssd_w_in[j][:, :SSD_MAIN].astype(BF16)
            w_dt = _pad_cols(ssd_w_in[j][:, SSD_MAIN:][:, perm], LANE).astype(BF16)
            dtb = _pad_cols(ssd_dt_bias[j].reshape(1, -1)[:, perm], LANE)
            alog = _pad_cols(ssd_a_log[j].reshape(1, -1)[:, perm], LANE)
            dskip = jnp.repeat(ssd_d[j], SSD_HEADDIM)[None]
            cw = ssd_conv_w[j]
            cbias = ssd_conv_b[j][None]
            w_out = ssd_w_out[j].astype(BF16)
            ng = ssd_norm[j][None]
            state = jnp.zeros((bsz, SSD_GROUPS, 2, SSD_STATE, SSD_GW), F32)
            new = []
            for h, m, is_ctx in streams:
                pmain = _proj(h, m, w_main, 1024, BF16)
                pdt = _proj(h, m, w_dt, LANE, F32)
                y, st = _ssd_scan(per_batch(pmain), per_batch(pdt), cw, cbias, dtb, alog, dskip, state)
                if is_ctx:
                    state = st
                if is_ctx and not need_ctx:
                    new.append(h)
                    continue
                new.append(_ssd_out(like(y, h), pmain, h, m, ng, w_out, g0, b0))
            hc, hl = new
        else:
            w_in = hy_w_in[j].astype(BF16)
            w_out = hy_w_out[j].astype(BF16)
            fw = (hy_f_w1[j], hy_f_b1[j], hy_f_w2[j], hy_f_b2[j], hy_f_w3[j], hy_f_b3[j], hy_f_w4[j], hy_f_freq[j])
            new = []
            for h, m, is_ctx in streams:
                if is_ctx and not need_ctx:
                    new.append(h)
                    continue
                p = per_batch(_proj(h, m, w_in, 1024, BF16))
                zz = _hyena_run(p, hy_conv_w[j], hy_conv_b[j][None], fw, hy_bias[j])
                new.append(_hy_out(like(zz, h), h, m, w_out, g0, b0))
            hc, hl = new
        hl = _ffn(hl, ml, w1, w2, g1, b1)
        if need_ctx:
            hc = _ffn(hc, mc, w1, w2, g1, b1)
    return _snake(hl)
```

```python
import functools
import math

import numpy as np
import jax
import jax.numpy as jnp
from jax import lax
from jax.experimental import pallas as pl
from jax.experimental.pallas import tpu as pltpu

F32 = jnp.float32
BF16 = jnp.bfloat16

D_MODEL = 1024
DEPTH = 4
GRID_W = 64
N_MIXERS = 3
D_FF = 4 * D_MODEL
DEEPNORM_ALPHA = (2 * DEPTH) ** 0.25
LN_EPS = 1e-5
RMS_EPS = 1e-6

GLA_HEADS = 4
GLA_DK = D_MODEL // 2
GLA_DV = D_MODEL
GLA_HK = GLA_DK // GLA_HEADS
GLA_HV = GLA_DV // GLA_HEADS
GLA_RANK = 16
GLA_GATE_NORM = 16.0
GLA_CHUNK = 64
GLA_UNROLL = 4
GLA_NEG = -1e30
LOG2E = math.log2(math.e)
GLA_MAIN = 2 * GLA_DK + 2 * GLA_DV

SSD_DI = 2 * D_MODEL
SSD_HEADDIM = 64
SSD_HEADS = SSD_DI // SSD_HEADDIM
SSD_GROUPS = 8
SSD_REP = SSD_HEADS // SSD_GROUPS
SSD_STATE = 128
SSD_CONV = 5
SSD_CHUNK = 128
SSD_UNROLL = 2
SSD_GN = SSD_GROUPS * SSD_STATE
SSD_CONV_DIM = SSD_DI + 2 * SSD_GN
SSD_MAIN = SSD_DI + SSD_CONV_DIM
SSD_GW = SSD_REP * SSD_HEADDIM

HY_ORDER = 2
HY_SHORT = 3
HY_EMB = 33
HY_FW = 64
HY_DECAY_TARGET = 1e-2
HY_FAST_DECAY = 0.3
HY_SLOW_DECAY = 1.5
HY_RADIX = 4
HY_MIN_BLOCK_LEN = 1024

LANE = 128
SUBLANE = 8
VMEM_LIMIT = 56 * 1024 * 1024


def _cparams(sem):
    return pltpu.CompilerParams(dimension_semantics=sem, vmem_limit_bytes=VMEM_LIMIT)


def _dot(a, b):
    return jnp.dot(a.astype(BF16), b.astype(BF16), preferred_element_type=F32)


def _dot_nt(a, b):
    return lax.dot_general(a.astype(BF16), b.astype(BF16), (((1,), (1,)), ((), ())),
                           preferred_element_type=F32)


def _split3(x):
    hi = x.astype(BF16)
    r1 = x - hi.astype(F32)
    mid = r1.astype(BF16)
    lo = (r1 - mid.astype(F32)).astype(BF16)
    return hi, mid, lo


def _dot_exact_l(m01, x):
    hi, mid, lo = _split3(x)
    d = lambda p: jnp.dot(m01, p, preferred_element_type=F32)
    return d(hi) + d(mid) + d(lo)


def _dot_exact_r(x, m01):
    hi, mid, lo = _split3(x)
    d = lambda p: jnp.dot(p, m01, preferred_element_type=F32)
    return d(hi) + d(mid) + d(lo)


def _dot_split2_r(x, m01):
    hi = x.astype(BF16)
    mid = (x - hi.astype(F32)).astype(BF16)
    d = lambda p: jnp.dot(p, m01, preferred_element_type=F32)
    return d(hi) + d(mid)


def _dot_exact_nt(m01, x):
    hi, mid, lo = _split3(x)
    d = lambda p: lax.dot_general(m01, p, (((1,), (1,)), ((), ())), preferred_element_type=F32)
    return d(hi) + d(mid) + d(lo)


def _dot_f32(a, b):
    ah, am, al = _split3(a)
    bh, bm, bl = _split3(b)
    d = lambda p, q: jnp.dot(p, q, preferred_element_type=F32)
    return (d(ah, bh) + (d(ah, bm) + d(am, bh)) + (d(ah, bl) + d(al, bh) + d(am, bm)))


def _dot_f32x3(a, b):
    ah, am, _ = _split3(a)
    bh, bm, _ = _split3(b)
    d = lambda p, q: jnp.dot(p, q, preferred_element_type=F32)
    return d(ah, bh) + (d(ah, bm) + d(am, bh))


def _silu(x):
    return x * jax.nn.sigmoid(x)


def _softplus(x):
    return jnp.maximum(x, 0.0) + jnp.log(1.0 + jnp.exp(-jnp.abs(x)))


def _log_sigmoid(x):
    return -_softplus(-x)


def _layer_norm(h, g, b):
    mu = jnp.mean(h, -1, keepdims=True)
    d = h - mu
    var = jnp.mean(d * d, -1, keepdims=True)
    return d * lax.rsqrt(var + LN_EPS) * g + b


def _res_ln(x, gate, y, g, b):
    return _layer_norm(DEEPNORM_ALPHA * x + gate * y, g, b)


def _ada_kernel(c_ref, w_ref, b_ref, o_ref):
    s = _silu(c_ref[...])
    o_ref[0] = _dot_f32(s, w_ref[0]) + b_ref[0]


def _ada(cvec, ada_w, ada_b):
    tn = 1536
    n = 6 * D_MODEL
    return pl.pallas_call(
        _ada_kernel,
        grid=(DEPTH, n // tn),
        in_specs=[pl.BlockSpec((16, D_MODEL), lambda i, j: (0, 0)),
                  pl.BlockSpec((1, D_MODEL, tn), lambda i, j: (i, 0, j)),
                  pl.BlockSpec((1, 1, tn), lambda i, j: (i, 0, j))],
        out_specs=pl.BlockSpec((1, 16, tn), lambda i, j: (i, 0, j)),
        out_shape=jax.ShapeDtypeStruct((DEPTH, 16, n), F32),
        compiler_params=_cparams(("arbitrary", "arbitrary")),
        name="ada",
    )(cvec, ada_w, ada_b.reshape(DEPTH, 1, n))


def _proj_kernel(x_ref, m_ref, w_ref, o_ref, u_scr):
    @pl.when(pl.program_id(2) == 0)
    def _():
        m = m_ref[0]
        u_scr[...] = (x_ref[0] * (1.0 + m[1:2]) + m[0:1]).astype(BF16)

    o_ref[0] = jnp.dot(u_scr[...], w_ref[...], preferred_element_type=F32).astype(o_ref.dtype)


def _proj(x, mods, w, tn, out_dtype):
    b, t, d = x.shape
    n = w.shape[1]
    tm = min(t, 1024)
    return pl.pallas_call(
        _proj_kernel,
        grid=(b, t // tm, n // tn),
        in_specs=[pl.BlockSpec((1, tm, d), lambda i, j, k: (i, j, 0)),
                  pl.BlockSpec((1, 8, d), lambda i, j, k: (i, 0, 0)),
                  pl.BlockSpec((d, tn), lambda i, j, k: (0, k))],
        out_specs=pl.BlockSpec((1, tm, tn), lambda i, j, k: (i, j, k)),
        out_shape=jax.ShapeDtypeStruct((b, t, n), out_dtype),
        scratch_shapes=[pltpu.VMEM((tm, d), BF16)],
        compiler_params=_cparams(("arbitrary", "arbitrary", "arbitrary")),
        name="proj",
    )(x, mods, w)


def _ffn_kernel(x_ref, m_ref, w1_ref, w2_ref, g_ref, b_ref, o_ref, acc_ref, u_scr):
    f = pl.program_id(2)

    @pl.when(f == 0)
    def _():
        m = m_ref[0]
        acc_ref[...] = jnp.zeros_like(acc_ref)
        u_scr[...] = (x_ref[0] * (1.0 + m[4:5]) + m[3:4]).astype(BF16)

    a = jnp.square(jnp.maximum(jnp.dot(u_scr[...], w1_ref[...], preferred_element_type=F32), 0.0))
    acc_ref[...] += _dot(a, w2_ref[...])

    @pl.when(f == pl.num_programs(2) - 1)
    def _():
        o_ref[0] = _res_ln(x_ref[0], m_ref[0][5:6], acc_ref[...], g_ref[...], b_ref[...])


def _ffn(x, mods, w1, w2, g, bb):
    b, t, d = x.shape
    tm = min(t, 1024)
    tf = 512
    return pl.pallas_call(
        _ffn_kernel,
        grid=(b, t // tm, D_FF // tf),
        in_specs=[pl.BlockSpec((1, tm, d), lambda i, j, k: (i, j, 0)),
                  pl.BlockSpec((1, 8, d), lambda i, j, k: (i, 0, 0)),
                  pl.BlockSpec((d, tf), lambda i, j, k: (0, k)),
                  pl.BlockSpec((tf, d), lambda i, j, k: (k, 0)),
                  pl.BlockSpec((1, d), lambda i, j, k: (0, 0)),
                  pl.BlockSpec((1, d), lambda i, j, k: (0, 0))],
        out_specs=pl.BlockSpec((1, tm, d), lambda i, j, k: (i, j, 0)),
        out_shape=jax.ShapeDtypeStruct((b, t, d), F32),
        scratch_shapes=[pltpu.VMEM((tm, d), F32), pltpu.VMEM((tm, d), BF16)],
        compiler_params=_cparams(("arbitrary", "arbitrary", "arbitrary")),
        name="ffn",
    )(x, mods, w1, w2, g, bb)


def _gla_out_kernel(o_ref, gate_ref, x_ref, m_ref, ng_ref, w_ref, g_ref, b_ref, out_ref):
    o = o_ref[0]
    ng = ng_ref[...]
    parts = []
    for h in range(GLA_HEADS):
        oh = o[:, h * GLA_HV:(h + 1) * GLA_HV]
        r = lax.rsqrt(jnp.mean(oh * oh, -1, keepdims=True) + RMS_EPS)
        parts.append(oh * r * ng)
    z = jnp.concatenate(parts, axis=-1) * _silu(gate_ref[0].astype(F32))
    y = _dot(z, w_ref[...])
    out_ref[0] = _res_ln(x_ref[0], m_ref[0][2:3], y, g_ref[...], b_ref[...])


def _gla_out(o, pmain, x, mods, ng, w, g, bb):
    b, t, d = x.shape
    tm = min(t, 512)
    gate_blk = (2 * GLA_DK) // GLA_DV + 1
    return pl.pallas_call(
        _gla_out_kernel,
        grid=(b, t // tm),
        in_specs=[pl.BlockSpec((1, tm, GLA_DV), lambda i, j: (i, j, 0)),
                  pl.BlockSpec((1, tm, GLA_DV), lambda i, j: (i, j, gate_blk)),
                  pl.BlockSpec((1, tm, d), lambda i, j: (i, j, 0)),
                  pl.BlockSpec((1, 8, d), lambda i, j: (i, 0, 0)),
                  pl.BlockSpec((1, GLA_HV), lambda i, j: (0, 0)),
                  pl.BlockSpec((GLA_DV, d), lambda i, j: (0, 0)),
                  pl.BlockSpec((1, d), lambda i, j: (0, 0)),
                  pl.BlockSpec((1, d), lambda i, j: (0, 0))],
        out_specs=pl.BlockSpec((1, tm, d), lambda i, j: (i, j, 0)),
        out_shape=jax.ShapeDtypeStruct((b, t, d), F32),
        compiler_params=_cparams(("arbitrary", "arbitrary")),
        name="gla_out",
    )(o, pmain, x, mods, ng, w, g, bb)


def _ssd_out_kernel(y_ref, z_ref, x_ref, m_ref, ng_ref, w_ref, g_ref, b_ref, out_ref):
    yz = y_ref[0] * _silu(z_ref[0].astype(F32))
    r = lax.rsqrt(jnp.mean(yz * yz, -1, keepdims=True) + RMS_EPS)
    y = _dot(yz * r * ng_ref[...], w_ref[...])
    out_ref[0] = _res_ln(x_ref[0], m_ref[0][2:3], y, g_ref[...], b_ref[...])


def _ssd_out(y, pmain, x, mods, ng, w, g, bb):
    b, t, d = x.shape
    tm = min(t, 512)
    return pl.pallas_call(
        _ssd_out_kernel,
        grid=(b, t // tm),
        in_specs=[pl.BlockSpec((1, tm, SSD_DI), lambda i, j: (i, j, 0)),
                  pl.BlockSpec((1, tm, SSD_DI), lambda i, j: (i, j, 0)),
                  pl.BlockSpec((1, tm, d), lambda i, j: (i, j, 0)),
                  pl.BlockSpec((1, 8, d), lambda i, j: (i, 0, 0)),
                  pl.BlockSpec((1, SSD_DI), lambda i, j: (0, 0)),
                  pl.BlockSpec((SSD_DI, d), lambda i, j: (0, 0)),
                  pl.BlockSpec((1, d), lambda i, j: (0, 0)),
                  pl.BlockSpec((1, d), lambda i, j: (0, 0))],
        out_specs=pl.BlockSpec((1, tm, d), lambda i, j: (i, j, 0)),
        out_shape=jax.ShapeDtypeStruct((b, t, d), F32),
        compiler_params=_cparams(("arbitrary", "arbitrary")),
        name="ssd_out",
    )(y, pmain, x, mods, ng, w, g, bb)


def _hy_out_kernel(z_ref, x_ref, m_ref, w_ref, g_ref, b_ref, out_ref):
    y = _dot(z_ref[0], w_ref[...])
    out_ref[0] = _res_ln(x_ref[0], m_ref[0][2:3], y, g_ref[...], b_ref[...])


def _hy_out(z, x, mods, w, g, bb):
    b, t, d = x.shape
    tm = min(t, 512)
    return pl.pallas_call(
        _hy_out_kernel,
        grid=(b, t // tm),
        in_specs=[pl.BlockSpec((1, tm, d), lambda i, j: (i, j, 0)),
                  pl.BlockSpec((1, tm, d), lambda i, j: (i, j, 0)),
                  pl.BlockSpec((1, 8, d), lambda i, j: (i, 0, 0)),
                  pl.BlockSpec((d, d), lambda i, j: (0, 0)),
                  pl.BlockSpec((1, d), lambda i, j: (0, 0)),
                  pl.BlockSpec((1, d), lambda i, j: (0, 0))],
        out_specs=pl.BlockSpec((1, tm, d), lambda i, j: (i, j, 0)),
        out_shape=jax.ShapeDtypeStruct((b, t, d), F32),
        compiler_params=_cparams(("arbitrary", "arbitrary")),
        name="hy_out",
    )(z, x, mods, w, g, bb)


def _gla_consts(fwd):
    c = GLA_CHUNK
    i = np.arange(c)
    tri = ((i[:, None] >= i[None, :]) if fwd else (i[:, None] <= i[None, :])).astype(np.float32)
    halves = [c >> (s + 1) for s in range(int(math.log2(c)))]
    nl = len(halves)
    fine = [lv for lv, half in enumerate(halves) if half < SUBLANE]
    wst = np.zeros(((len(fine) + 1) * c, c), np.float32)
    wst[:c] = tri
    negq = np.zeros((nl * c, GLA_HK), np.float32)
    negk = np.zeros((nl * c, GLA_HK), np.float32)
    msk = np.zeros(((nl + 1) * c, c), np.float32)
    msk[:c] = np.eye(c)
    for lv, half in enumerate(halves):
        blk = i // (2 * half)
        upper = (i % (2 * half)) >= half
        ref = blk * 2 * half + (half - 1 if fwd else half)
        if lv in fine:
            fi = fine.index(lv)
            wst[(fi + 1) * c:(fi + 2) * c] = tri - tri[ref]
        qside = upper if fwd else ~upper
        negq[lv * c:(lv + 1) * c] = np.where(qside, 0.0, GLA_NEG)[:, None]
        negk[lv * c:(lv + 1) * c] = np.where(~qside, 0.0, GLA_NEG)[:, None]
        msk[(lv + 1) * c:(lv + 2) * c] = ((blk[:, None] == blk[None, :]) & qside[:, None] & (~qside)[None, :])
    return (jnp.asarray(wst, BF16), jnp.asarray(negq), jnp.asarray(negk), jnp.asarray(msk))


def _gla_group(chains, eye):
    c = GLA_CHUNK
    for ch in chains:
        wst = ch["consts"][0]
        ch["est"] = _dot_split2_l(wst, ch["ga"])
    for ch in chains:
        _, negq, negk, _ = ch["consts"]
        q, k, est, fwd = ch["q"], ch["k"], ch["est"], ch["fwd"]
        nl = negq.shape[0] // c
        ncoarse = nl + 1 - est.shape[0] // c
        cum = est[0:c]
        tot = cum[c - 1:c] if fwd else cum[0:1]
        ch["qt"] = (q * jnp.exp2(cum)).astype(BF16)
        ch["kt"] = (k * jnp.exp2(tot - cum)).astype(BF16)
        ch["dec"] = jnp.exp2(tot)
        qs, ks = [q.astype(BF16)], [k.astype(BF16)]
        for lv in range(nl):
            if lv < ncoarse:
                size = c >> lv
                parts = []
                for lo in range(0, c, size):
                    ref = lo + size // 2 - (1 if fwd else 0)
                    parts.append(cum[lo:lo + size] - cum[ref:ref + 1])
                e = parts[0] if len(parts) == 1 else jnp.concatenate(parts, axis=0)
            else:
                e = est[(lv - ncoarse + 1) * c:(lv - ncoarse + 2) * c]
            sl = slice(lv * c, (lv + 1) * c)
            qs.append((q * jnp.exp2(e + negq[sl])).astype(BF16))
            ks.append((k * jnp.exp2(negk[sl] - e)).astype(BF16))
        ch["qs"], ch["ks"] = qs, ks
    for ch in chains:
        ch["ps"] = [_dot_nt(a, b) for a, b in zip(ch["qs"], ch["ks"])]
    for ch in chains:
        msk = ch["consts"][3]
        attn = None
        for lv, p in enumerate(ch["ps"]):
            term = p * msk[lv * c:(lv + 1) * c]
            attn = term if attn is None else attn + term
        ch["attn"] = attn.astype(BF16)
    row = lax.broadcasted_iota(jnp.int32, (2 * SUBLANE, 1), 0)
    for ch in chains:
        vb = ch["v"].astype(BF16)
        ch["o"] = jnp.dot(ch["attn"], vb, preferred_element_type=F32)
        dec = ch["dec"]
        hi = dec.astype(BF16).astype(F32)
        mid = (dec - hi).astype(BF16).astype(F32)
        extra = jnp.where(row == 0, hi, jnp.where(row == 1, mid, 0.0)).astype(BF16)
        kt_t = _dot_nt(eye, jnp.concatenate([ch["kt"], extra], axis=0))
        ch["dec_col"] = kt_t[:, c:c + 1] + kt_t[:, c + 1:c + 2]
        ch["upd"] = jnp.dot(kt_t[:, 0:c].astype(BF16), vb, preferred_element_type=F32)


def _dot_split2_l(m, x):
    hi = x.astype(BF16)
    mid = (x - hi.astype(F32)).astype(BF16)
    d = lambda p: jnp.dot(m, p, preferred_element_type=F32)
    return d(hi) + d(mid)


def _gla_kernel(q_ref, k_ref, v_ref, a_ref, w2_ref, b2_ref, s0_ref,
                wf_ref, nqf_ref, nkf_ref, mf_ref, wb_ref, nqb_ref, nkb_ref, mb_ref, eye_ref,
                o_ref, s_ref, gaf, gab, sf, sb, *, t):
    c = GLA_CHUNK
    nc = t // c
    un = min(GLA_UNROLL, nc)
    pb = min(t, 512)

    def prep(i, carry):
        r = pl.multiple_of(i * pb, pb)
        a = a_ref[0, pl.ds(r, pb), :]
        for z, ga in ((0, gaf), (1, gab)):
            logit = _dot_f32x3(a, w2_ref[z]) + b2_ref[z]
            ga[pl.ds(r, pb), :] = _log_sigmoid(logit) * (LOG2E / GLA_GATE_NORM)
        o_ref[0, pl.ds(r, pb), :] = jnp.zeros((pb, GLA_HV), F32)
        return carry

    lax.fori_loop(0, t // pb, prep, 0)
    sf[...] = s0_ref[0, 0, 0]
    sb[...] = s0_ref[0, 0, 1]
    scale = GLA_HK ** -0.5
    fconst = (wf_ref, nqf_ref, nkf_ref, mf_ref)
    bconst = (wb_ref, nqb_ref, nkb_ref, mb_ref)

    def body(ci, carry):
        chains = []
        for fwd, ga, consts in ((True, gaf, fconst), (False, gab, bconst)):
            cvals = tuple(x[...] for x in consts)
            for u in range(un):
                idx = ci * un + u
                r = pl.multiple_of((idx if fwd else nc - 1 - idx) * c, c)
                chains.append(dict(r=r, fwd=fwd, consts=cvals, ga=ga[pl.ds(r, c), :],
                                   q=q_ref[0, pl.ds(r, c), :].astype(F32) * scale,
                                   k=k_ref[0, pl.ds(r, c), :].astype(F32), v=v_ref[0, pl.ds(r, c), :]))
        _gla_group(chains, eye_ref[...])
        for fwd, st_ref in ((True, sf), (False, sb)):
            st = st_ref[...]
            for ch in chains:
                if ch["fwd"] != fwd:
                    continue
                o = ch["o"] + jnp.dot(ch["qt"], st.astype(BF16), preferred_element_type=F32)
                o_ref[0, pl.ds(ch["r"], c), :] += o
                st = st * ch["dec_col"] + ch["upd"]
            st_ref[...] = st
        return carry

    lax.fori_loop(0, nc // un, body, 0)
    s_ref[0, 0, 0] = sf[...]
    s_ref[0, 0, 1] = sb[...]


def _gla_scan(pmain, pa, w2p, b2p, s0):
    b, t, _ = pmain.shape
    h = GLA_HEADS
    assert t % (GLA_CHUNK * min(GLA_UNROLL, t // GLA_CHUNK)) == 0
    consts = _gla_consts(True) + _gla_consts(False) + (jnp.asarray(np.eye(GLA_HK), BF16),)
    kblk = GLA_DK // GLA_HK
    vblk = (2 * GLA_DK) // GLA_HV
    cspec = lambda a: pl.BlockSpec(a.shape, lambda i, j: (0,) * a.ndim)
    return pl.pallas_call(
        functools.partial(_gla_kernel, t=t),
        grid=(b, h),
        in_specs=[pl.BlockSpec((1, t, GLA_HK), lambda i, j: (i, 0, j)),
                  pl.BlockSpec((1, t, GLA_HK), lambda i, j: (i, 0, kblk + j)),
                  pl.BlockSpec((1, t, GLA_HV), lambda i, j: (i, 0, vblk + j)),
                  pl.BlockSpec((1, t, LANE), lambda i, j: (i, 0, 0)),
                  pl.BlockSpec((2, LANE, GLA_HK), lambda i, j: (0, 0, j)),
                  pl.BlockSpec((2, 1, GLA_HK), lambda i, j: (0, 0, j)),
                  pl.BlockSpec((1, 1, 2, GLA_HK, GLA_HV), lambda i, j: (i, j, 0, 0, 0))]
                 + [cspec(a) for a in consts],
        out_specs=[pl.BlockSpec((1, t, GLA_HV), lambda i, j: (i, 0, j)),
                   pl.BlockSpec((1, 1, 2, GLA_HK, GLA_HV), lambda i, j: (i, j, 0, 0, 0))],
        out_shape=[jax.ShapeDtypeStruct((b, t, GLA_DV), F32),
                   jax.ShapeDtypeStruct((b, h, 2, GLA_HK, GLA_HV), F32)],
        scratch_shapes=[pltpu.VMEM((t, GLA_HK), F32), pltpu.VMEM((t, GLA_HK), F32),
                        pltpu.VMEM((GLA_HK, GLA_HV), F32), pltpu.VMEM((GLA_HK, GLA_HV), F32)],
        compiler_params=_cparams(("arbitrary", "arbitrary")),
        name="gla_scan",
    )(pmain, pmain, pmain, pa, w2p, b2p, s0, *consts)


def _ssd_consts():
    c = SSD_CHUNK
    i = np.arange(c)
    tril = (i[:, None] >= i[None, :]).astype(np.float32)
    triu = (i[:, None] <= i[None, :]).astype(np.float32)
    eye = np.eye(SSD_STATE, dtype=np.float32)
    sel16 = np.eye(16, LANE, dtype=np.float32)
    e4 = np.zeros((2, LANE, SSD_GW), np.float32)
    for z in range(2):
        for r in range(SSD_REP):
            e4[z, z * SSD_REP + r, r * SSD_HEADDIM:(r + 1) * SSD_HEADDIM] = 1.0
    lm = np.zeros((SSD_REP * c, SSD_GW), np.float32)
    for r in range(SSD_REP):
        lm[r * c:(r + 1) * c, r * SSD_HEADDIM:(r + 1) * SSD_HEADDIM] = 1.0
    return (jnp.asarray(tril, BF16), jnp.asarray(triu, BF16), jnp.asarray(tril), jnp.asarray(triu),
            jnp.asarray(eye, BF16), jnp.asarray(sel16, BF16), jnp.asarray(e4, BF16), jnp.asarray(lm))


def _conv_block(in_ref, w, bias, i, nblk, rows, t, taps):
    halo = 2 * SUBLANE
    r = pl.multiple_of(i * rows, rows)
    rp = pl.multiple_of(jnp.maximum(r - halo, 0), halo)
    rn = pl.multiple_of(jnp.minimum(r + rows, t - halo), halo)
    half = taps // 2
    outs = []
    for c0 in range(0, in_ref.shape[2], LANE):
        sl = slice(c0, c0 + LANE)
        cur = in_ref[0, pl.ds(r, rows), sl].astype(F32)
        prev = jnp.where(i > 0, in_ref[0, pl.ds(rp, halo), sl].astype(F32), 0.0)
        nxt = jnp.where(i < nblk - 1, in_ref[0, pl.ds(rn, halo), sl].astype(F32), 0.0)
        ext = jnp.concatenate([prev, cur, nxt], axis=0)
        acc = bias[:, sl]
        for j in range(taps):
            off = halo - half + j
            acc = acc + w[j:j + 1, sl] * ext[off:off + rows]
        outs.append(acc)
    return r, (outs[0] if len(outs) == 1 else jnp.concatenate(outs, axis=1))


def _ssd_group(chains, eye, sel16, lm):
    c = SSD_CHUNK
    for ch in chains:
        ch["cum"] = _dot_exact_l(ch["tri"], ch["das"])
        ch["dt_e"] = _dot_split2_r(ch["ds"], ch["e4z"])
        ch["ccb"], bcb = ch["cc"].astype(BF16), ch["bc"].astype(BF16)
        ch["cb"] = _dot_nt(ch["ccb"], bcb)
        ch["bc_t"] = _dot_nt(eye, bcb).astype(BF16)
    for ch in chains:
        ch["cum_t"] = _dot_exact_nt(sel16, ch["cum"])
        ch["cum_e"] = _dot_split2_r(ch["cum"], ch["e4z"])
    for ch in chains:
        cum, cum_t, cum_e = ch["cum"], ch["cum_t"], ch["cum_e"]
        tot_e = cum_e[c - 1:c] if ch["fwd"] else cum_e[0:1]
        xdt = ch["xg"] * ch["dt_e"]
        cb = ch["cb"] * ch["mask"]
        ms = []
        for r in range(SSD_REP):
            q = ch["z"] * SSD_REP + r
            seg = cum[:, q:q + 1] - cum_t[q:q + 1, :]
            ms.append((cb * jnp.exp2(jnp.minimum(seg, 0.0))).astype(BF16))
        ch["mcat"] = jnp.concatenate(ms, axis=1)
        ch["xbd"] = (jnp.concatenate([xdt] * SSD_REP, axis=0) * lm).astype(BF16)
        ch["w"] = (xdt * jnp.exp2(tot_e - cum_e)).astype(BF16)
        ch["dece"] = jnp.exp2(cum_e)
        ch["dec"] = jnp.exp2(tot_e)
    for ch in chains:
        ch["y"] = jnp.dot(ch["mcat"], ch["xbd"], preferred_element_type=F32)
        ch["upd"] = jnp.dot(ch["bc_t"], ch["w"], preferred_element_type=F32)


def _ssd_kernel(x_ref, bm_ref, cm_ref, dt_ref, wx_ref, wb_ref, wc_ref, bx_ref, bb_ref, bcb_ref,
                dtb_ref, alog_ref, dsk_ref, s0_ref,
                tl_ref, tu_ref, ml_ref, mu_ref, eye_ref, sel16_ref, e4_ref, lm_ref,
                y_ref, s_ref, xc, bcs, ccs, dsel, dasel, sf, sb, *, t):
    c = SSD_CHUNK
    nc = t // c
    un = min(SSD_UNROLL, nc)
    pb = c
    nblk = t // pb

    shift = (LANE - 2 * SSD_REP * pl.program_id(1)) % LANE

    def prep(i):
        for in_ref, w_ref, b_ref, out in ((x_ref, wx_ref, bx_ref, xc), (bm_ref, wb_ref, bb_ref, bcs),
                                          (cm_ref, wc_ref, bcb_ref, ccs)):
            r, acc = _conv_block(in_ref, w_ref[...], b_ref[...], i, nblk, pb, t, SSD_CONV)
            val = _silu(acc)
            out[pl.ds(r, pb), :] = val
            if out is xc:
                y_ref[0, pl.ds(r, pb), :] = val * dsk_ref[...]
        r = pl.multiple_of(i * pb, pb)
        dtp = _softplus(dt_ref[0, pl.ds(r, pb), :] + dtb_ref[...])
        a2 = jnp.exp(alog_ref[...]) * (-LOG2E)
        dsel[pl.ds(r, pb), :] = pltpu.roll(dtp, shift, 1)
        dasel[pl.ds(r, pb), :] = pltpu.roll(dtp * a2, shift, 1)

    for blk in sorted(set(range(un)) | set(nc - 1 - u for u in range(un))):
        prep(jnp.int32(blk))
    sf[...] = s0_ref[0, 0, 0]
    sb[...] = s0_ref[0, 0, 1]
    n_ahead = max(nc // (2 * un) - 1, 0)

    def body(ci, carry, ahead):
        chains = []
        for fwd, z, tri_ref, mk_ref in ((True, 0, tl_ref, ml_ref), (False, 1, tu_ref, mu_ref)):
            tri, mask, e4z = tri_ref[...], mk_ref[...], e4_ref[z]
            for u in range(un):
                idx = ci * un + u
                r = pl.multiple_of((idx if fwd else nc - 1 - idx) * c, c)
                chains.append(dict(r=r, fwd=fwd, z=z, tri=tri, mask=mask, e4z=e4z,
                                   xg=xc[pl.ds(r, c), :], bc=bcs[pl.ds(r, c), :], cc=ccs[pl.ds(r, c), :],
                                   ds=dsel[pl.ds(r, c), :], das=dasel[pl.ds(r, c), :]))
        if ahead:
            for u in range(un):
                nxt = (ci + 1) * un + u
                prep(nxt)
                prep(nc - 1 - nxt)
        _ssd_group(chains, eye_ref[...], sel16_ref[...], lm_ref[...])
        for fwd, st_ref in ((True, sf), (False, sb)):
            st = st_ref[...]
            for ch in chains:
                if ch["fwd"] != fwd:
                    continue
                y = ch["y"] + jnp.dot(ch["ccb"], st.astype(BF16), preferred_element_type=F32) * ch["dece"]
                y_ref[0, pl.ds(ch["r"], c), :] += y
                st = st * ch["dec"] + ch["upd"]
            st_ref[...] = st
        return carry

    lax.fori_loop(0, n_ahead, functools.partial(body, ahead=True), 0)
    lax.fori_loop(n_ahead, nc // un, functools.partial(body, ahead=False), 0)
    s_ref[0, 0, 0] = sf[...]
    s_ref[0, 0, 1] = sb[...]


def _ssd_scan(pmain, pdt, conv_w, conv_b, dtb, alog, dskip, s0):
    b, t, _ = pmain.shape
    g = SSD_GROUPS
    assert t % (SSD_CHUNK * min(SSD_UNROLL, t // SSD_CHUNK)) == 0
    tril, triu, mtril, mtriu, eye, sel16, e4, lm = _ssd_consts()
    xblk = SSD_DI // SSD_GW
    bblk = (2 * SSD_DI) // SSD_STATE
    cblk = bblk + SSD_GN // SSD_STATE
    wbblk = SSD_DI // SSD_STATE
    wcblk = wbblk + SSD_GN // SSD_STATE
    cspec = lambda a: pl.BlockSpec(a.shape, lambda i, j: (0,) * a.ndim)
    return pl.pallas_call(
        functools.partial(_ssd_kernel, t=t),
        grid=(b, g),
        in_specs=[pl.BlockSpec((1, t, SSD_GW), lambda i, j: (i, 0, xblk + j)),
                  pl.BlockSpec((1, t, SSD_STATE), lambda i, j: (i, 0, bblk + j)),
                  pl.BlockSpec((1, t, SSD_STATE), lambda i, j: (i, 0, cblk + j)),
                  pl.BlockSpec((1, t, LANE), lambda i, j: (i, 0, 0)),
                  pl.BlockSpec((SSD_CONV, SSD_GW), lambda i, j: (0, j)),
                  pl.BlockSpec((SSD_CONV, SSD_STATE), lambda i, j: (0, wbblk + j)),
                  pl.BlockSpec((SSD_CONV, SSD_STATE), lambda i, j: (0, wcblk + j)),
                  pl.BlockSpec((1, SSD_GW), lambda i, j: (0, j)),
                  pl.BlockSpec((1, SSD_STATE), lambda i, j: (0, wbblk + j)),
                  pl.BlockSpec((1, SSD_STATE), lambda i, j: (0, wcblk + j)),
                  pl.BlockSpec((1, LANE), lambda i, j: (0, 0)),
                  pl.BlockSpec((1, LANE), lambda i, j: (0, 0)),
                  pl.BlockSpec((1, SSD_GW), lambda i, j: (0, j)),
                  pl.BlockSpec((1, 1, 2, SSD_STATE, SSD_GW), lambda i, j: (i, j, 0, 0, 0)),
                  cspec(tril), cspec(triu), cspec(mtril), cspec(mtriu), cspec(eye), cspec(sel16),
                  cspec(e4), cspec(lm)],
        out_specs=[pl.BlockSpec((1, t, SSD_GW), lambda i, j: (i, 0, j)),
                   pl.BlockSpec((1, 1, 2, SSD_STATE, SSD_GW), lambda i, j: (i, j, 0, 0, 0))],
        out_shape=[jax.ShapeDtypeStruct((b, t, SSD_DI), F32),
                   jax.ShapeDtypeStruct((b, g, 2, SSD_STATE, SSD_GW), F32)],
        scratch_shapes=[pltpu.VMEM((t, SSD_GW), F32), pltpu.VMEM((t, SSD_STATE), F32),
                        pltpu.VMEM((t, SSD_STATE), F32), pltpu.VMEM((t, LANE), F32),
                        pltpu.VMEM((t, LANE), F32),
                        pltpu.VMEM((SSD_STATE, SSD_GW), F32), pltpu.VMEM((SSD_STATE, SSD_GW), F32)],
        compiler_params=_cparams(("arbitrary", "arbitrary")),
        name="ssd_scan",
    )(pmain, pmain, pmain, pdt, conv_w, conv_w, conv_w, conv_b, conv_b, conv_b,
      dtb, alog, dskip, s0, tril, triu, mtril, mtriu, eye, sel16, e4, lm)


def _conv3_kernel(p_ref, w_ref, b_ref, *out_refs, t):
    pb = min(t, 256)
    nblk = t // pb

    def blk(i, carry):
        r, acc = _conv_block(p_ref, w_ref[...], b_ref[...], i, nblk, pb, t, HY_SHORT)
        for o_ref in out_refs:
            o_ref[0, pl.ds(r, pb), :] = acc.astype(o_ref.dtype)
        return carry

    lax.fori_loop(0, nblk, blk, 0)


def _conv3(p, w, bias, col0, ncols, dtypes):
    b, t, _ = p.shape
    tn = 256
    off = col0 // tn
    out_shape = [jax.ShapeDtypeStruct((b, t, ncols), dt) for dt in dtypes]
    out_specs = [pl.BlockSpec((1, t, tn), lambda i, j: (i, 0, j)) for _ in dtypes]
    return pl.pallas_call(
        functools.partial(_conv3_kernel, t=t),
        grid=(b, ncols // tn),
        in_specs=[pl.BlockSpec((1, t, tn), lambda i, j: (i, 0, off + j)),
                  pl.BlockSpec((HY_SHORT, tn), lambda i, j: (0, off + j)),
                  pl.BlockSpec((1, tn), lambda i, j: (0, off + j))],
        out_specs=out_specs,
        out_shape=out_shape,
        compiler_params=_cparams(("arbitrary", "arbitrary")),
        name="hy_conv3",
    )(p, w, bias)


HY_TN = 512


HY_TIME_LANE = HY_FW


def _filt_trunk_kernel(w1_ref, b1_ref, w2_ref, b2_ref, w3_ref, b3_ref, fr_ref, o_ref, *, l, tl):
    row = lax.broadcasted_iota(jnp.int32, (tl, 1), 0) + pl.program_id(1) * tl
    pos = jnp.where(pl.program_id(0) == 0, row, l - row).astype(F32)
    tt = pos * (1.0 / (l - 1))
    bands = (HY_EMB - 1) // 2
    lane = lax.broadcasted_iota(jnp.int32, (1, LANE), 1)
    band = ((lane - 1) & (bands - 1)).astype(F32) * ((bands - 1 - 1e-4) / (bands - 1)) + 1e-4
    ang = (pos * (2 * math.pi / l)) * band
    z = jnp.where(lane == 0, tt,
                  jnp.where(lane <= bands, jnp.cos(ang), jnp.where(lane <= 2 * bands, -jnp.sin(ang), 0.0)))
    fr = fr_ref[...]
    h = jnp.sin(fr * (_dot_f32x3(z, w1_ref[...]) + b1_ref[...]))
    h = jnp.sin(fr * (_dot_f32x3(h, w2_ref[...]) + b2_ref[...]))
    h = jnp.sin(fr * (_dot_f32x3(h, w3_ref[...]) + b3_ref[...]))
    o_ref[0] = jnp.where(lane == HY_TIME_LANE, tt, h)


def _filt_final_kernel(h_ref, w4_ref, dl_ref, o_ref, *, radix, tl):
    h = h_ref[0]
    tt = h[:, HY_TIME_LANE:HY_TIME_LANE + 1]
    row = lax.broadcasted_iota(jnp.int32, (tl, 1), 0) + pl.program_id(1) * tl
    valid = jnp.logical_or(pl.program_id(0) < 2 * radix - 1, row > 0)
    hh = _dot_f32x3(h, w4_ref[...]) * jnp.exp(-tt * dl_ref[...])
    o_ref[0] = jnp.where(valid, hh, 0.0)


def _hy_filters(l, lp, radix, w1, b1, w2, b2, w3, b3, w4, freq):
    d = D_MODEL
    tl = min(lp, 512)
    tn = HY_TN
    nd = d // tn
    nsub = 2 * radix - 1
    nrb = lp // tl
    assert (HY_EMB - 1) // 2 == 16
    pad2 = lambda a: jnp.zeros((LANE, LANE), F32).at[:a.shape[0], :a.shape[1]].set(a)
    pad1 = lambda a: jnp.zeros((1, LANE), F32).at[0, :a.shape[0]].set(a)
    w4p = jnp.zeros((LANE, HY_ORDER * 2 * d), F32).at[:HY_FW].set(w4)
    deltas = np.abs(np.linspace(math.log(HY_DECAY_TARGET) / HY_FAST_DECAY,
                                math.log(HY_DECAY_TARGET) / HY_SLOW_DECAY, d))
    dl = jnp.asarray(deltas[None, :], F32)
    small = [pad2(w1), pad1(b1), pad2(w2), pad1(b2), pad2(w3), pad1(b3), pad1(freq)]
    trunk = pl.pallas_call(
        functools.partial(_filt_trunk_kernel, l=l, tl=tl),
        grid=(2, l // tl),
        in_specs=[pl.BlockSpec(a.shape, lambda s, m: (0, 0)) for a in small],
        out_specs=pl.BlockSpec((1, tl, LANE), lambda s, m: (s, m, 0)),
        out_shape=jax.ShapeDtypeStruct((2, l, LANE), F32),
        compiler_params=_cparams(("arbitrary", "arbitrary")),
        name="hy_filt_trunk",
    )(*small)

    def lag_is_negative(bi):
        didx = bi % nsub
        return jnp.where(bi < nsub, didx < radix - 1, didx <= radix - 1)

    def h_map(bi, m, j):
        didx = bi % nsub
        base = jnp.abs(didx - (radix - 1)) * nrb
        desc = jnp.where(bi < nsub, didx < radix - 1, didx >= radix)
        return (desc.astype(jnp.int32), jnp.where(desc, radix * nrb - base, base) + m, 0)

    def w4_map(bi, m, j):
        return (0, ((j // nd) * 2 + lag_is_negative(bi).astype(jnp.int32)) * nd + j % nd)

    return pl.pallas_call(
        functools.partial(_filt_final_kernel, radix=radix, tl=tl),
        grid=(2 * nsub, nrb, HY_ORDER * nd),
        in_specs=[pl.BlockSpec((1, tl, LANE), h_map), pl.BlockSpec((LANE, tn), w4_map),
                  pl.BlockSpec((1, tn), lambda bi, m, j: (0, j % nd))],
        out_specs=pl.BlockSpec((1, tl, tn), lambda bi, m, j: (bi // nsub, m, j * nsub + bi % nsub)),
        out_shape=jax.ShapeDtypeStruct((2, lp, HY_ORDER * nsub * d), F32),
        compiler_params=_cparams(("arbitrary", "arbitrary", "arbitrary")),
        name="hy_filt",
    )(trunk, w4p, dl)


def _dft_tables_np(l):
    n = 2 * l
    k = np.arange(l)
    ang = 2 * math.pi * ((k[:, None] * k[None, :]) % n) / n
    cm = np.cos(ang)
    sm = -np.sin(ang)
    alt = np.where(k % 2 == 0, 1.0, -1.0)
    s_fwd = sm.copy()
    s_fwd[0, :] = alt
    s_inv = sm.copy()
    s_inv[:, 0] = alt
    return cm, s_fwd, s_inv


def _dft_tab_kernel(ca_ref, sa_ref, cb_ref, sb_ref, c_ref, sf_ref, si_ref, *, rb):
    a = pl.program_id(0)
    ca, sa = ca_ref[0], sa_ref[0]
    cb, sb = cb_ref[...], sb_ref[...]
    cm = ca * cb - sa * sb
    sm = -(sa * cb + ca * sb)
    l = cm.shape[1]
    row = lax.broadcasted_iota(jnp.int32, (rb, l), 0)
    col = lax.broadcasted_iota(jnp.int32, (rb, l), 1)
    alt_col = jnp.where(col % 2 == 0, 1.0, -1.0)
    alt_row = jnp.where(row % 2 == 0, 1.0, -1.0)
    c_ref[...] = cm.astype(BF16)
    sf_ref[...] = jnp.where(jnp.logical_and(row == 0, a == 0), alt_col, sm).astype(BF16)
    si_ref[...] = jnp.where(col == 0, alt_row, sm).astype(BF16)


def _dft_tables(l):
    if l <= 512:
        return tuple(jnp.asarray(m, BF16) for m in _dft_tables_np(l))
    rb = 64
    na = l // rb
    n = 2 * l
    nn = np.arange(l)
    aa = np.arange(na)
    bb = np.arange(rb)
    ang_a = 2 * math.pi * ((aa[:, None] * rb * nn[None, :]) % n) / n
    ang_b = 2 * math.pi * ((bb[:, None] * nn[None, :]) % n) / n
    ca = jnp.asarray(np.cos(ang_a)[:, None, :], F32)
    sa = jnp.asarray(np.sin(ang_a)[:, None, :], F32)
    cb = jnp.asarray(np.cos(ang_b), F32)
    sb = jnp.asarray(np.sin(ang_b), F32)
    rowspec = pl.BlockSpec((1, 1, l), lambda a: (a, 0, 0))
    tabspec = pl.BlockSpec((rb, l), lambda a: (0, 0))
    outspec = pl.BlockSpec((rb, l), lambda a: (a, 0))
    return tuple(pl.pallas_call(
        functools.partial(_dft_tab_kernel, rb=rb),
        grid=(na,),
        in_specs=[rowspec, rowspec, tabspec, tabspec],
        out_specs=[outspec] * 3,
        out_shape=[jax.ShapeDtypeStruct((l, l), BF16)] * 3,
        compiler_params=_cparams(("arbitrary",)),
        name="dft_tables",
    )(ca, sa, cb, sb))


def _spec_filt_kernel(c_ref, s_ref, a_ref, b_ref, h_ref, accr, acci, accn, *, l):
    kk = pl.program_id(2)

    @pl.when(kk == 0)
    def _():
        accr[...] = jnp.zeros_like(accr)
        acci[...] = jnp.zeros_like(acci)
        accn[...] = jnp.zeros_like(accn)

    a, bw = a_ref[0], b_ref[0]
    sm = (a + bw).astype(BF16)
    df = (a - bw).astype(BF16)
    accr[...] += jnp.dot(c_ref[...], sm, preferred_element_type=F32)
    acci[...] += jnp.dot(s_ref[...], df, preferred_element_type=F32)
    accn[...] += jnp.dot(s_ref[...], sm, preferred_element_type=F32)

    @pl.when(kk == pl.num_programs(2) - 1)
    def _():
        tm = accr.shape[0]
        row0 = (lax.broadcasted_iota(jnp.int32, (tm, 1), 0) + pl.program_id(0) * tm) == 0
        scale = jnp.where(row0, 0.5 / l, 1.0 / l)
        h_ref[0] = accr[...] * scale
        h_ref[1] = jnp.where(row0, accn[...], acci[...]) * scale


def _spec_filt(cm, s_fwd, fab, l):
    ncol = fab.shape[2]
    tm, tn, tk = min(l, 512), HY_TN, min(l, 1024)
    return pl.pallas_call(
        functools.partial(_spec_filt_kernel, l=l),
        grid=(l // tm, ncol // tn, l // tk),
        in_specs=[pl.BlockSpec((tm, tk), lambda m, j, k: (m, k)),
                  pl.BlockSpec((tm, tk), lambda m, j, k: (m, k)),
                  pl.BlockSpec((1, tk, tn), lambda m, j, k: (0, k, j)),
                  pl.BlockSpec((1, tk, tn), lambda m, j, k: (1, k, j))],
        out_specs=pl.BlockSpec((2, tm, tn), lambda m, j, k: (0, m, j)),
        out_shape=jax.ShapeDtypeStruct((2, l, ncol), F32),
        scratch_shapes=[pltpu.VMEM((tm, tn), F32)] * 3,
        compiler_params=_cparams(("arbitrary", "arbitrary", "arbitrary")),
        name="hy_spec_filt",
    )(cm, s_fwd, fab, fab)


def _seg_spec_kernel(c_ref, s_ref, *refs, radix):
    u_refs, g_ref, z_ref = refs[:radix], refs[radix], refs[radix + 1]
    tm, tn = z_ref.shape[2], z_ref.shape[3]
    spec = []
    for q in range(radix):
        u = u_refs[q][0]
        spec.append((jnp.dot(c_ref[...], u, preferred_element_type=F32),
                     jnp.dot(s_ref[...], u, preferred_element_type=F32)))
    row0 = (lax.broadcasted_iota(jnp.int32, (tm, 1), 0) + pl.program_id(1) * tm) == 0
    for t in range(radix):
        zr = zi = None
        for q in range(radix):
            di = t - q + radix - 1
            gr = g_ref[0, :, di * tn:(di + 1) * tn]
            gi = g_ref[1, :, di * tn:(di + 1) * tn]
            xr, xi = spec[q]
            tr = xr * gr - jnp.where(row0, 0.0, xi * gi)
            ti = jnp.where(row0, xi * gi, xr * gi + xi * gr)
            zr = tr if zr is None else zr + tr
            zi = ti if zi is None else zi + ti
        z_ref[t, 0] = zr.astype(BF16)
        z_ref[t, 1] = zi.astype(BF16)


def _seg_spec(cm, s_fwd, ub, gspec, order, radix):
    b, l, d = ub.shape
    lp = l // radix
    tm, tn = min(lp, 256), HY_TN
    nd = d // tn
    nsub = 2 * radix - 1
    u_specs = [pl.BlockSpec((1, lp, tn), functools.partial(lambda n, m, i, q: (i, q, n), q=q)) for q in range(radix)]
    return pl.pallas_call(
        functools.partial(_seg_spec_kernel, radix=radix),
        grid=(nd, lp // tm, b),
        in_specs=[pl.BlockSpec((tm, lp), lambda n, m, i: (m, 0)),
                  pl.BlockSpec((tm, lp), lambda n, m, i: (m, 0))] + u_specs
                 + [pl.BlockSpec((2, tm, nsub * tn), lambda n, m, i: (0, m, order * nd + n))],
        out_specs=pl.BlockSpec((radix, 2, tm, tn), lambda n, m, i: (0, 0, m, i * nd + n)),
        out_shape=jax.ShapeDtypeStruct((radix, 2, lp, b * d), BF16),
        compiler_params=_cparams(("arbitrary", "arbitrary", "arbitrary")),
        name="hy_seg_spec",
    )(cm, s_fwd, *([ub] * radix), gspec)


def _seg_inv_kernel(c_ref, s_ref, z_ref, u_ref, g_ref, bias_ref, *out_refs):
    conv = (jnp.dot(c_ref[...], z_ref[0, 0], preferred_element_type=F32)
            + jnp.dot(s_ref[...], z_ref[0, 1], preferred_element_type=F32))
    res = g_ref[0].astype(F32) * (conv + u_ref[0] * bias_ref[...])
    for o_ref in out_refs:
        o_ref[0] = res.astype(o_ref.dtype)


def _seg_inv(cm, s_inv, zspec, u, gate, gsel, bias, dtypes, radix):
    b, l, d = u.shape
    lp = l // radix
    tt, tn = min(lp, 512), HY_TN
    nd = d // tn
    nt = lp // tt
    blk = pl.BlockSpec((1, tt, tn), lambda t, j, r: (j // nd, r * nt + t, j % nd))
    out_shape = [jax.ShapeDtypeStruct((b, l, d), dt) for dt in dtypes]
    out_specs = [blk for _ in dtypes]
    return pl.pallas_call(
        _seg_inv_kernel,
        grid=(nt, b * nd, radix),
        in_specs=[pl.BlockSpec((tt, lp), lambda t, j, r: (t, 0)),
                  pl.BlockSpec((tt, lp), lambda t, j, r: (t, 0)),
                  pl.BlockSpec((1, 2, lp, tn), lambda t, j, r: (r, 0, 0, j)),
                  blk,
                  pl.BlockSpec((1, tt, tn), lambda t, j, r: (j // nd, r * nt + t, gsel * nd + j % nd)),
                  pl.BlockSpec((1, tn), lambda t, j, r: (0, j % nd))],
        out_specs=out_specs,
        out_shape=out_shape,
        compiler_params=_cparams(("arbitrary", "arbitrary", "arbitrary")),
        name="hy_seg_inv",
    )(cm, s_inv, zspec, u, gate, bias)


def _hyena_run(p, conv_w, conv_b, fw, h_bias):
    b, l, _ = p.shape
    d = D_MODEL
    radix = max(1, min(HY_RADIX, l // HY_MIN_BLOCK_LEN))
    lp = l // radix
    v, vb = _conv3(p, conv_w, conv_b, 0, d, (F32, BF16))
    x12 = _conv3(p, conv_w, conv_b, d, 2 * d, (BF16,))[0]
    cm, s_fwd, s_inv = _dft_tables(lp)
    gspec = _spec_filt(cm, s_fwd, _hy_filters(l, lp, radix, *fw), lp)
    z1 = _seg_spec(cm, s_fwd, vb, gspec, 0, radix)
    z, zb = _seg_inv(cm, s_inv, z1, v, x12, 0, h_bias[0:1], (F32, BF16), radix)
    z2 = _seg_spec(cm, s_fwd, zb, gspec, 1, radix)
    return _seg_inv(cm, s_inv, z2, z, x12, 1, h_bias[1:2], (BF16,), radix)[0]


def _snake_kernel(x_ref, j_ref, o_ref):
    jm = j_ref[...]
    for g in range(x_ref.shape[1] // (2 * GRID_W)):
        r0 = g * 2 * GRID_W
        o_ref[0, r0:r0 + GRID_W, :] = x_ref[0, r0:r0 + GRID_W, :]
        o_ref[0, r0 + GRID_W:r0 + 2 * GRID_W, :] = _dot_exact_l(jm, x_ref[0, r0 + GRID_W:r0 + 2 * GRID_W, :])


def _snake(h):
    b, l, ch = h.shape
    tm = 512
    jm = jnp.asarray(np.eye(GRID_W)[::-1].copy(), BF16)
    return pl.pallas_call(
        _snake_kernel,
        grid=(b, l // tm),
        in_specs=[pl.BlockSpec((1, tm, ch), lambda i, j: (i, j, 0)),
                  pl.BlockSpec((GRID_W, GRID_W), lambda i, j: (0, 0))],
        out_specs=pl.BlockSpec((1, tm, ch), lambda i, j: (i, j, 0)),
        out_shape=jax.ShapeDtypeStruct((b, l, ch), F32),
        compiler_params=_cparams(("arbitrary", "arbitrary")),
        name="snake",
    )(h, jm)


def _pad_cols(w, n):
    return jnp.zeros((w.shape[0], n), w.dtype).at[:, :w.shape[1]].set(w)


def kernel(x, c, ctx, c_ctx, ada_w, ada_b, ln_g, ln_b, ffn_w1, ffn_w2, gla_w_in, gla_w_a2, gla_b_a2, gla_norm, gla_w_out, ssd_w_in, ssd_conv_w, ssd_conv_b, ssd_dt_bias, ssd_a_log, ssd_d, ssd_norm, ssd_w_out, hy_w_in, hy_conv_w, hy_conv_b, hy_f_w1, hy_f_b1, hy_f_w2, hy_f_b2, hy_f_w3, hy_f_b3, hy_f_w4, hy_f_freq, hy_bias, hy_w_out):
    bsz, _, d = x.shape
    hl = _snake(x)
    hc = ctx.reshape(1, -1, d)
    cvec = jnp.zeros((16, d), F32).at[:bsz].set(c).at[bsz].set(c_ctx)
    mods = _ada(cvec, ada_w, ada_b).reshape(DEPTH, 16, 6, d)
    mods = jnp.concatenate([mods, jnp.zeros((DEPTH, 16, 2, d), F32)], axis=2)
    per_batch = lambda a: a.reshape(bsz, -1, a.shape[-1])
    like = lambda a, h: a.reshape(h.shape[0], -1, a.shape[-1])

    for i in range(DEPTH):
        kind, j = i % N_MIXERS, i // N_MIXERS
        need_ctx = i < DEPTH - 1
        ml = mods[i, :bsz]
        mc = mods[i, bsz][None]
        g0, b0 = ln_g[i, 0][None], ln_b[i, 0][None]
        g1, b1 = ln_g[i, 1][None], ln_b[i, 1][None]
        w1 = ffn_w1[i].astype(BF16)
        w2 = ffn_w2[i].astype(BF16)
        streams = [(hc, mc, True), (hl, ml, False)]
        if kind == 0:
            w_main = gla_w_in[j][:, :GLA_MAIN].astype(BF16)
            w_a = _pad_cols(gla_w_in[j][:, GLA_MAIN:], LANE).astype(BF16)
            w2p = jnp.zeros((2, LANE, GLA_DK), F32)
            for z in range(2):
                w2p = w2p.at[z, z * GLA_RANK:(z + 1) * GLA_RANK].set(gla_w_a2[j, z])
            b2p = gla_b_a2[j][:, None, :]
            w_out = gla_w_out[j].astype(BF16)
            ng = gla_norm[j][None]
            state = jnp.zeros((bsz, GLA_HEADS, 2, GLA_HK, GLA_HV), F32)
            new = []
            for h, m, is_ctx in streams:
                pmain = _proj(h, m, w_main, 1024, BF16)
                pa = _proj(h, m, w_a, LANE, F32)
                o, st = _gla_scan(per_batch(pmain), per_batch(pa), w2p, b2p, state)
                if is_ctx:
                    state = st
                if is_ctx and not need_ctx:
                    new.append(h)
                    continue
                new.append(_gla_out(like(o, h), pmain, h, m, ng, w_out, g0, b0))
            hc, hl = new
        elif kind == 1:
            perm = np.arange(2 * SSD_HEADS).reshape(2, SSD_GROUPS, SSD_REP).transpose(1, 0, 2).reshape(-1)
            w_main = ssd_w_in[j][:, :SSD_MAIN].astype(BF16)
            w_dt = _pad_cols(ssd_w_in[j][:, SSD_MAIN:][:, perm], LANE).astype(BF16)
            dtb = _pad_cols(ssd_dt_bias[j].reshape(1, -1)[:, perm], LANE)
            alog = _pad_cols(ssd_a_log[j].reshape(1, -1)[:, perm], LANE)
            dskip = jnp.repeat(ssd_d[j], SSD_HEADDIM)[None]
            cw = ssd_conv_w[j]
            cbias = ssd_conv_b[j][None]
            w_out = ssd_w_out[j].astype(BF16)
            ng = ssd_norm[j][None]
            state = jnp.zeros((bsz, SSD_GROUPS, 2, SSD_STATE, SSD_GW), F32)
            new = []
            for h, m, is_ctx in streams:
                pmain = _proj(h, m, w_main, 1024, BF16)
                pdt = _proj(h, m, w_dt, LANE, F32)
                y, st = _ssd_scan(per_batch(pmain), per_batch(pdt), cw, cbias, dtb, alog, dskip, state)
                if is_ctx:
                    state = st
                if is_ctx and not need_ctx:
                    new.append(h)
                    continue
                new.append(_ssd_out(like(y, h), pmain, h, m, ng, w_out, g0, b0))
            hc, hl = new
        else:
            w_in = hy_w_in[j].astype(BF16)
            w_out = hy_w_out[j].astype(BF16)
            fw = (hy_f_w1[j], hy_f_b1[j], hy_f_w2[j], hy_f_b2[j], hy_f_w3[j], hy_f_b3[j], hy_f_w4[j], hy_f_freq[j])
            new = []
            for h, m, is_ctx in streams:
                if is_ctx and not need_ctx:
                    new.append(h)
                    continue
                p = per_batch(_proj(h, m, w_in, 1024, BF16))
                zz = _hyena_run(p, hy_conv_w[j], hy_conv_b[j][None], fw, hy_bias[j])
                new.append(_hy_out(like(zz, h), h, m, w_out, g0, b0))
            hc, hl = new
        hl = _ffn(hl, ml, w1, w2, g1, b1)
        if need_ctx:
            hc = _ffn(hc, mc, w1, w2, g1, b1)
    return _snake(hl)
```
